```python
import jax, jax.numpy as jnp
from jax import lax
import numpy as np

D_MODEL = 1024
BATCH = 8
SEQ = 4096
DEPTH = 1

GRID_W = 64
CTX_LEN = 256
EPS = 1e-6
ROPE_THETA = 10000.0
V_HEAD_DIM = 128
QK_NOPE_DIM = 128
QK_ROPE_DIM = 64
Q_LORA_RANK = 256
KV_LORA_RANK = 256
MLA_HEADS = D_MODEL // (2 * V_HEAD_DIM)
MLA_WIDTH = MLA_HEADS * V_HEAD_DIM
QK_HEAD_DIM = QK_NOPE_DIM + QK_ROPE_DIM
Q_BLOCK = 128
HGRN_KEY_DIM = 128
HGRN_VAL_DIM = 128
HGRN_HEADS = D_MODEL // (2 * HGRN_VAL_DIM)
HGRN_WIDTH = HGRN_HEADS * HGRN_KEY_DIM
CHUNK = 64
MIX_WIDTH = MLA_WIDTH + HGRN_HEADS * HGRN_VAL_DIM
IN_SIZES = (Q_LORA_RANK, KV_LORA_RANK, QK_ROPE_DIM,
            HGRN_WIDTH, HGRN_WIDTH, HGRN_WIDTH, HGRN_WIDTH, HGRN_WIDTH)
IN_COLS = sum(IN_SIZES)
D_FF = -(-8 * D_MODEL // (3 * 256)) * 256

kernel_name = "hymba_mla_hgrn2_dit_block"


def rmsnorm(x, g):
    x32 = x.astype(jnp.float32)
    y = x32 * lax.rsqrt(jnp.mean(x32 * x32, axis=-1, keepdims=True) + EPS)
    return (y * g.astype(jnp.float32)).astype(x.dtype)


def modulate(h, shift, scale):
    return h * (1.0 + scale) + shift


def axial_rope_2d(n):
    rows = n // GRID_W
    row = jnp.broadcast_to(jnp.arange(rows)[:, None], (rows, GRID_W)).reshape(n)
    col = jnp.broadcast_to(jnp.arange(GRID_W)[None, :], (rows, GRID_W)).reshape(n)
    axis_dim = QK_ROPE_DIM // 2
    inv = 1.0 / (ROPE_THETA ** (jnp.arange(0, axis_dim, 2, dtype=jnp.float32) / axis_dim))
    ang = jnp.concatenate([row.astype(jnp.float32)[:, None] * inv,
                           col.astype(jnp.float32)[:, None] * inv], axis=-1)
    return jnp.cos(ang), jnp.sin(ang)


def apply_rope(x, cos, sin):
    if x.ndim == 4:
        cos, sin = cos[:, None, :], sin[:, None, :]
    nf = QK_ROPE_DIM // 4
    x32 = x.astype(jnp.float32)
    outs = []
    for a in range(2):
        xa = x32[..., a * 2 * nf:(a + 1) * 2 * nf]
        c, s = cos[..., a * nf:(a + 1) * nf], sin[..., a * nf:(a + 1) * nf]
        x1, x2 = xa[..., :nf], xa[..., nf:]
        outs.append(jnp.concatenate([x1 * c - x2 * s, x2 * c + x1 * s], axis=-1))
    return jnp.concatenate(outs, axis=-1).astype(x.dtype)


def attend_blocks(q, k, v):
    B, N, H, dq = q.shape
    nb = N // Q_BLOCK
    scale = 1.0 / float(np.sqrt(dq))
    qb = jnp.moveaxis(q.reshape(B, nb, Q_BLOCK, H, dq), 1, 0)

    def one(qblk):
        s = jnp.einsum('bqhd,bkhd->bhqk', qblk, k).astype(jnp.float32) * scale
        p = jax.nn.softmax(s, axis=-1).astype(v.dtype)
        return jnp.einsum('bhqk,bkhd->bqhd', p, v)

    o = lax.map(one, qb)
    return jnp.moveaxis(o, 0, 1).reshape(B, N, H, v.shape[-1])


def gla_chunked(q, k, v, log_f, s0):
    B, T, H, dk = q.shape
    dv = v.shape[-1]
    nc = T // CHUNK

    def to_chunks(t):
        return jnp.moveaxis(t.reshape(B, nc, CHUNK, H, t.shape[-1]), 1, 0)

    mask = jnp.tril(jnp.ones((CHUNK, CHUNK), dtype=bool))

    def step(S, inp):
        qc, kc, vc, gc = inp
        b = jnp.cumsum(gc, axis=1)
        b_ref = b[:, CHUNK // 2 - 1][:, None]
        b_last = b[:, -1]
        o_inter = jnp.einsum('bchk,bhkv->bchv', qc * jnp.exp(b), S)
        A = jnp.einsum('bthk,bshk->bhts', qc * jnp.exp(b - b_ref), kc * jnp.exp(b_ref - b))
        A = jnp.where(mask, A, 0.0)
        o_intra = jnp.einsum('bhts,bshv->bthv', A, vc)
        S_new = jnp.exp(b_last)[..., None] * S + jnp.einsum(
            'bshk,bshv->bhkv', kc * jnp.exp(b_last[:, None] - b), vc)
        return S_new, o_inter + o_intra

    s_fin, o = lax.scan(step, s0, (to_chunks(q), to_chunks(k), to_chunks(v), to_chunks(log_f)))
    return s_fin, jnp.moveaxis(o, 0, 1).reshape(B, T, H, dv)


def scan_direction(q, k, v, log_f, s0, reverse):
    if reverse:
        q, k, v, log_f = (jnp.flip(t, axis=1) for t in (q, k, v, log_f))
    s_fin, o = gla_chunked(q, k, v, log_f, s0)
    if reverse:
        o = jnp.flip(o, axis=1)
    return s_fin, o


def token_mixer(h_lat, h_ctx, cos, sin, layer, w_in, g_qn, w_uq, g_kvn, w_ukv,
                lb_fwd, lb_bwd, g_on, w_out, with_ctx_out):
    B, N, _ = h_lat.shape
    L = h_ctx.shape[1]
    offs = [int(v) for v in np.cumsum(IN_SIZES)[:-1]]
    cq_l, ckv_l, kpe_l, hq_l, hi_l, hg_l, ff_l, fb_l = jnp.split(h_lat @ w_in, offs, axis=-1)
    cq_c, ckv_c, kpe_c, hq_c, hi_c, hg_c, ff_c, fb_c = jnp.split(h_ctx @ w_in, offs, axis=-1)

    def mla_q(cq, n):
        q = (rmsnorm(cq, g_qn) @ w_uq).reshape(B, n, MLA_HEADS, QK_HEAD_DIM)
        return q[..., :QK_NOPE_DIM], q[..., QK_NOPE_DIM:]

    def mla_kv(ckv, n):
        kv = (rmsnorm(ckv, g_kvn) @ w_ukv).reshape(B, n, MLA_HEADS, QK_NOPE_DIM + V_HEAD_DIM)
        return kv[..., :QK_NOPE_DIM], kv[..., QK_NOPE_DIM:]

    def full_k(k_nope, k_pe, n):
        k_pe = jnp.broadcast_to(k_pe[:, :, None, :], (B, n, MLA_HEADS, QK_ROPE_DIM))
        return jnp.concatenate([k_nope, k_pe], axis=-1)

    qn_l, qr_l = mla_q(cq_l, N)
    q_lat = jnp.concatenate([qn_l, apply_rope(qr_l, cos, sin)], axis=-1)
    kn_l, v_l = mla_kv(ckv_l, N)
    k_lat = full_k(kn_l, apply_rope(kpe_l, cos, sin), N)
    kn_c, v_c = mla_kv(ckv_c, L)
    k_ctx = full_k(kn_c, kpe_c, L)
    K = jnp.concatenate([k_ctx, k_lat], axis=1)
    V = jnp.concatenate([v_c, v_l], axis=1)
    o_mla_lat = attend_blocks(q_lat, K, V).reshape(B, N, MLA_WIDTH)

    def heads(t, n):
        return t.astype(jnp.float32).reshape(B, n, HGRN_HEADS, HGRN_KEY_DIM)

    def gates(fraw, lb_tab, n):
        lb = jnp.cumsum(jax.nn.softmax(lb_tab.astype(jnp.float32), axis=0), axis=0)[layer]
        f = lb + (1.0 - lb) * jax.nn.sigmoid(fraw.astype(jnp.float32))
        return heads(1.0 - f, n), heads(jnp.log(f), n)

    s0 = jnp.zeros((B, HGRN_HEADS, HGRN_KEY_DIM, HGRN_VAL_DIM), jnp.float32)
    q_hl, v_hl = heads(hq_l, N), heads(hi_l, N)
    q_hc, v_hc = heads(hq_c, L), heads(hi_c, L)
    o_h_lat = 0.0
    o_h_ctx = 0.0
    for fr_l, fr_c, lb_tab, rev in ((ff_l, ff_c, lb_fwd, False), (fb_l, fb_c, lb_bwd, True)):
        k_c, lf_c = gates(fr_c, lb_tab, L)
        s_ctx, o_c = scan_direction(q_hc, k_c, v_hc, lf_c, s0, rev)
        k_l, lf_l = gates(fr_l, lb_tab, N)
        _, o_l = scan_direction(q_hl, k_l, v_hl, lf_l, s_ctx, rev)
        o_h_lat = o_h_lat + o_l
        o_h_ctx = o_h_ctx + o_c

    def hgrn_out(o, hg, n):
        o = rmsnorm(o, g_on).astype(hg.dtype).reshape(B, n, HGRN_HEADS * HGRN_VAL_DIM)
        return o * jax.nn.silu(hg)

    lat_out = jnp.concatenate([o_mla_lat, hgrn_out(o_h_lat, hg_l, N)], axis=-1) @ w_out
    if not with_ctx_out:
        return lat_out, None
    qn_c, qr_c = mla_q(cq_c, L)
    q_ctx = jnp.concatenate([qn_c, qr_c], axis=-1)
    o_mla_ctx = attend_blocks(q_ctx, k_ctx, v_c).reshape(B, L, MLA_WIDTH)
    ctx_out = jnp.concatenate([o_mla_ctx, hgrn_out(o_h_ctx, hg_c, L)], axis=-1) @ w_out
    return lat_out, ctx_out


def swiglu(h, w_gate, w_up, w_down):
    return (jax.nn.silu(h @ w_gate) * (h @ w_up)) @ w_down


def setup_inputs(seed: int = 0) -> dict:
    key = jax.random.key(seed)
    ks = jax.random.split(key, 24)

    def nrm(k, shape, scale):
        return jax.random.normal(k, shape, jnp.float32) * scale

    def gain(k, shape):
        return 1.0 + nrm(k, shape, 0.05)

    return {
        "x": nrm(ks[0], (BATCH, SEQ, D_MODEL), 1.0),
        "c": nrm(ks[1], (BATCH, D_MODEL), 1.0),
        "ctx": nrm(ks[2], (BATCH, CTX_LEN, D_MODEL), 1.0),
        "c_ctx": nrm(ks[3], (D_MODEL,), 1.0),
        "w_mod": nrm(ks[4], (DEPTH, D_MODEL, 6 * D_MODEL), 0.5 * D_MODEL ** -0.5),
        "b_mod": nrm(ks[5], (DEPTH, 6 * D_MODEL), 0.02),
        "g_norm_mix": gain(ks[6], (DEPTH, D_MODEL)),
        "g_norm_ffn": gain(ks[7], (DEPTH, D_MODEL)),
        "w_in": nrm(ks[8], (DEPTH, D_MODEL, IN_COLS), D_MODEL ** -0.5),
        "g_q_norm": gain(ks[9], (DEPTH, Q_LORA_RANK)),
        "w_uq": nrm(ks[10], (DEPTH, Q_LORA_RANK, MLA_HEADS * QK_HEAD_DIM), Q_LORA_RANK ** -0.5),
        "g_kv_norm": gain(ks[11], (DEPTH, KV_LORA_RANK)),
        "w_ukv": nrm(ks[12], (DEPTH, KV_LORA_RANK, MLA_HEADS * (QK_NOPE_DIM + V_HEAD_DIM)), KV_LORA_RANK ** -0.5),
        "lb_fwd": nrm(ks[13], (DEPTH + 1, HGRN_WIDTH), 0.1),
        "lb_bwd": nrm(ks[14], (DEPTH + 1, HGRN_WIDTH), 0.1),
        "g_hgrn_norm": gain(ks[15], (DEPTH, HGRN_VAL_DIM)),
        "w_out": nrm(ks[16], (DEPTH, MIX_WIDTH, D_MODEL), MIX_WIDTH ** -0.5),
        "w_gate": nrm(ks[17], (DEPTH, D_MODEL, D_FF), D_MODEL ** -0.5),
        "w_up": nrm(ks[18], (DEPTH, D_MODEL, D_FF), D_MODEL ** -0.5),
        "w_down": nrm(ks[19], (DEPTH, D_FF, D_MODEL), D_FF ** -0.5),
        "g_final": gain(ks[20], (D_MODEL,)),
    }


def reference(x, c, ctx, c_ctx, w_mod, b_mod, g_norm_mix, g_norm_ffn, w_in, g_q_norm, w_uq,
              g_kv_norm, w_ukv, lb_fwd, lb_bwd, g_hgrn_norm, w_out, w_gate, w_up, w_down, g_final):
    N = x.shape[1]
    cos, sin = axial_rope_2d(N)
    for layer in range(DEPTH):
        mod = jax.nn.silu(c) @ w_mod[layer] + b_mod[layer]
        sh1, sc1, gt1, sh2, sc2, gt2 = jnp.split(mod[:, None, :], 6, axis=-1)
        mod_c = jax.nn.silu(c_ctx) @ w_mod[layer] + b_mod[layer]
        csh1, csc1, cgt1, csh2, csc2, cgt2 = jnp.split(mod_c, 6, axis=-1)
        update_ctx = layer < DEPTH - 1

        h = modulate(rmsnorm(x, g_norm_mix[layer]), sh1, sc1)
        hc = modulate(rmsnorm(ctx, g_norm_mix[layer]), csh1, csc1)
        mix_lat, mix_ctx = token_mixer(h, hc, cos, sin, layer, w_in[layer], g_q_norm[layer],
                                       w_uq[layer], g_kv_norm[layer], w_ukv[layer],
                                       lb_fwd, lb_bwd, g_hgrn_norm[layer], w_out[layer], update_ctx)
        x = x + gt1 * mix_lat
        h2 = modulate(rmsnorm(x, g_norm_ffn[layer]), sh2, sc2)
        x = x + gt2 * swiglu(h2, w_gate[layer], w_up[layer], w_down[layer])
        if update_ctx:
            ctx = ctx + cgt1 * mix_ctx
            hc2 = modulate(rmsnorm(ctx, g_norm_ffn[layer]), csh2, csc2)
            ctx = ctx + cgt2 * swiglu(hc2, w_gate[layer], w_up[layer], w_down[layer])
    return rmsnorm(x, g_final)
```

```python
import functools

import numpy as np
import jax
import jax.numpy as jnp
from jax import lax
from jax.experimental import pallas as pl
from jax.experimental.pallas import tpu as pltpu

D_MODEL = 1024
GRID_W = 64
EPS = 1e-6
ROPE_THETA = 10000.0
V_HEAD_DIM = 128
QK_NOPE_DIM = 128
QK_ROPE_DIM = 64
Q_LORA_RANK = 256
KV_LORA_RANK = 256
HEADS = 4
QK_HEAD_DIM = QK_NOPE_DIM + QK_ROPE_DIM
HGRN_DIM = 128
HGRN_WIDTH = HEADS * HGRN_DIM
CHUNK = 64
IN_SIZES = (Q_LORA_RANK, KV_LORA_RANK, QK_ROPE_DIM,
            HGRN_WIDTH, HGRN_WIDTH, HGRN_WIDTH, HGRN_WIDTH, HGRN_WIDTH)
D_FF = 2816
FF_CHUNK = 256
VMEM_LIMIT_BYTES = 56 * 1024 * 1024

BF16 = jnp.bfloat16
F32 = jnp.float32


def _dot(a, b):
    return jnp.dot(a, b, preferred_element_type=F32)


def _dot_nt(a, b):
    return lax.dot_general(a, b, (((1,), (1,)), ((), ())), preferred_element_type=F32)


def _dot_tn(a, b):
    return lax.dot_general(a, b, (((0,), (0,)), ((), ())), preferred_element_type=F32)


def _silu(x):
    return x * jax.nn.sigmoid(x)


def _rms(x):
    return x * lax.rsqrt(jnp.mean(x * x, axis=-1, keepdims=True) + EPS)


def _mod_kernel(c_ref, w_ref, b_ref, o_ref):
    a = _silu(c_ref[...]).astype(BF16)
    o_ref[...] = _dot(a, w_ref[...].astype(BF16)) + b_ref[...]


def _mod_call(cc, w_mod, b_mod):
    rows = cc.shape[0]
    cols = w_mod.shape[1]
    tn = 1024
    return pl.pallas_call(
        _mod_kernel,
        grid=(cols // tn,),
        in_specs=[pl.BlockSpec((rows, D_MODEL), lambda j: (0, 0)),
                  pl.BlockSpec((D_MODEL, tn), lambda j: (0, j)),
                  pl.BlockSpec((1, tn), lambda j: (0, j))],
        out_specs=pl.BlockSpec((rows, tn), lambda j: (0, j)),
        out_shape=jax.ShapeDtypeStruct((rows, cols), F32),
        compiler_params=pltpu.CompilerParams(dimension_semantics=("arbitrary",)),
        name="mod",
    )(cc, w_mod, b_mod)


_C_CQ = 0
_C_CKV = 256
_C_HGRN = 512
_C_KPE = 512 + 5 * HGRN_WIDTH
_IN_COLS = _C_KPE + 2 * QK_ROPE_DIM


def _in_proj_kernel(x_ref, mod_ref, gmix_ref, win_ref, gq_ref, wuq_ref, gkv_ref, wukv_ref,
                    cos_ref, sin_ref,
                    q_ref, k_ref, v_ref, hq_ref, hi_ref, hg_ref, ff_ref, fb_ref):
    x = x_ref[0]
    shift = mod_ref[0, 0:1, :]
    gain = gmix_ref[...] * (1.0 + mod_ref[0, 1:2, :])
    h = (_rms(x) * gain + shift).astype(BF16)

    cos4 = cos_ref[...]
    sin4 = sin_ref[...]

    lat = _dot(h, win_ref[:, _C_CQ:_C_HGRN])
    cq = (_rms(lat[:, :Q_LORA_RANK]) * gq_ref[...]).astype(BF16)
    ckv = (_rms(lat[:, Q_LORA_RANK:]) * gkv_ref[...]).astype(BF16)

    q = _dot(cq, wuq_ref[...])
    scale = 1.0 / float(np.sqrt(QK_HEAD_DIM))
    q_rope = q[:, 512:768] * cos4 + q[:, 768:1024] * sin4
    kv = _dot(ckv, wukv_ref[...])
    kp = _dot(h, win_ref[:, _C_KPE:_IN_COLS])
    k_rope = (kp[:, :QK_ROPE_DIM] * cos4[:, :QK_ROPE_DIM]
              + kp[:, QK_ROPE_DIM:] * sin4[:, :QK_ROPE_DIM]).astype(BF16)
    for hd in range(HEADS):
        q_ref[0, hd, :, :QK_NOPE_DIM] = (q[:, hd * 128:(hd + 1) * 128] * scale).astype(BF16)
        q_ref[0, hd, :, QK_NOPE_DIM:] = (q_rope[:, hd * 64:(hd + 1) * 64] * scale).astype(BF16)
        k_ref[0, hd, :, :QK_NOPE_DIM] = kv[:, hd * 128:(hd + 1) * 128].astype(BF16)
        k_ref[0, hd, :, QK_NOPE_DIM:] = k_rope
        v_ref[0, hd] = kv[:, 512 + hd * 128:512 + (hd + 1) * 128].astype(BF16)

    for j, o_ref in enumerate((hq_ref, hi_ref, hg_ref, ff_ref, fb_ref)):
        c0 = _C_HGRN + j * HGRN_WIDTH
        t = _dot(h, win_ref[:, c0:c0 + HGRN_WIDTH])
        for hd in range(HEADS):
            o_ref[0, hd] = t[:, hd * 128:(hd + 1) * 128].astype(o_ref.dtype)


def _in_proj_call(x, mod_rows, g_mix, w_in, g_q, w_uq, g_kv, w_ukv, cos4, sin4, tm):
    B, n, _ = x.shape
    const = lambda b, i: (0, 0)
    head_blk = lambda w: pl.BlockSpec((1, HEADS, tm, w), lambda b, i: (b, 0, i, 0))
    hshape = lambda w, dt: jax.ShapeDtypeStruct((B, HEADS, n, w), dt)
    return pl.pallas_call(
        _in_proj_kernel,
        grid=(B, n // tm),
        in_specs=[pl.BlockSpec((1, tm, D_MODEL), lambda b, i: (b, i, 0)),
                  pl.BlockSpec((1, 8, D_MODEL), lambda b, i: (b, 0, 0)),
                  pl.BlockSpec((1, D_MODEL), const),
                  pl.BlockSpec((D_MODEL, _IN_COLS), const),
                  pl.BlockSpec((1, Q_LORA_RANK), const),
                  pl.BlockSpec((Q_LORA_RANK, 1024), const),
                  pl.BlockSpec((1, KV_LORA_RANK), const),
                  pl.BlockSpec((KV_LORA_RANK, 1024), const),
                  pl.BlockSpec((tm, 256), lambda b, i: (i, 0)),
                  pl.BlockSpec((tm, 256), lambda b, i: (i, 0))],
        out_specs=[head_blk(QK_HEAD_DIM), head_blk(QK_HEAD_DIM), head_blk(V_HEAD_DIM),
                   head_blk(128), head_blk(128), head_blk(128), head_blk(128), head_blk(128)],
        out_shape=[hshape(QK_HEAD_DIM, BF16), hshape(QK_HEAD_DIM, BF16), hshape(V_HEAD_DIM, BF16),
                   hshape(128, BF16), hshape(128, BF16), hshape(128, BF16),
                   hshape(128, F32), hshape(128, F32)],
        compiler_params=pltpu.CompilerParams(
            dimension_semantics=("parallel", "parallel"), vmem_limit_bytes=VMEM_LIMIT_BYTES),
        name="in_proj",
    )(x, mod_rows, g_mix, w_in, g_q, w_uq, g_kv, w_ukv, cos4, sin4)


def _attn_kernel(q_ref, kc_ref, kl_ref, vc_ref, vl_ref, o_ref, *, tk):
    q = q_ref[0, 0]
    n_lat = kl_ref.shape[2]

    s = _dot_nt(q, kc_ref[0, 0])
    m = jnp.max(s, axis=-1, keepdims=True)
    p = jnp.exp(s - m)
    l = jnp.sum(p, axis=-1, keepdims=True)
    acc = _dot(p.astype(BF16), vc_ref[0, 0])

    for j in range(n_lat // tk):
        k = kl_ref[0, 0, j * tk:(j + 1) * tk, :]
        v = vl_ref[0, 0, j * tk:(j + 1) * tk, :]
        s = _dot_nt(q, k)
        m_new = jnp.maximum(m, jnp.max(s, axis=-1, keepdims=True))
        alpha = jnp.exp(m - m_new)
        p = jnp.exp(s - m_new)
        l = alpha * l + jnp.sum(p, axis=-1, keepdims=True)
        acc = alpha * acc + _dot(p.astype(BF16), v)
        m = m_new
    o_ref[0] = (acc / l).astype(o_ref.dtype)


def _attn_call(q, k_ctx, k_lat, v_ctx, v_lat, tq, tk):
    B, H, n, _ = q.shape
    n_ctx = k_ctx.shape[2]
    return pl.pallas_call(
        functools.partial(_attn_kernel, tk=tk),
        grid=(B, H, n // tq),
        in_specs=[pl.BlockSpec((1, 1, tq, QK_HEAD_DIM), lambda b, h, i: (b, h, i, 0)),
                  pl.BlockSpec((1, 1, n_ctx, QK_HEAD_DIM), lambda b, h, i: (b, h, 0, 0)),
                  pl.BlockSpec((1, 1, n, QK_HEAD_DIM), lambda b, h, i: (b, h, 0, 0)),
                  pl.BlockSpec((1, 1, n_ctx, V_HEAD_DIM), lambda b, h, i: (b, h, 0, 0)),
                  pl.BlockSpec((1, 1, n, V_HEAD_DIM), lambda b, h, i: (b, h, 0, 0))],
        out_specs=pl.BlockSpec((1, tq, V_HEAD_DIM), lambda b, h, i: (b, i, h)),
        out_shape=jax.ShapeDtypeStruct((B, n, H * V_HEAD_DIM), BF16),
        compiler_params=pltpu.CompilerParams(
            dimension_semantics=("parallel", "parallel", "arbitrary"),
            vmem_limit_bytes=VMEM_LIMIT_BYTES),
        name="attn",
    )(q, k_ctx, k_lat, v_ctx, v_lat)


def _split3(g):
    hi = g.astype(BF16)
    r = g - hi.astype(F32)
    mid = r.astype(BF16)
    lo = (r - mid.astype(F32)).astype(BF16)
    return hi, mid, lo


def _hgrn_kernel(vc_ref, ffc_ref, fbc_ref, q_ref, v_ref, hg_ref, ff_ref, fb_ref,
                 lbf_ref, lbb_ref, gon_ref, o_ref, acc_ref):
    n_ctx = vc_ref.shape[2]
    n_lat = q_ref.shape[2]
    nc_ctx = n_ctx // CHUNK
    nc_lat = n_lat // CHUNK

    row = lax.broadcasted_iota(jnp.int32, (CHUNK, CHUNK), 0)
    col = lax.broadcasted_iota(jnp.int32, (CHUNK, CHUNK), 1)
    lower = col <= row
    upper = col >= row
    tri = (jnp.where(lower, 1.0, 0.0).astype(BF16), jnp.where(upper, 1.0, 0.0).astype(BF16))
    masks = (lower, upper)
    ref_rows = ((CHUNK // 2 - 1, CHUNK - 1), (CHUNK // 2, 0))

    def lower_bound(tab_ref):
        t = tab_ref[...]
        e = jnp.exp(t - jnp.max(t, axis=0, keepdims=True))
        return e[0:1] / jnp.sum(e, axis=0, keepdims=True)

    lbs = (lower_bound(lbf_ref), lower_bound(lbb_ref))

    def decay(fraw, d):
        f = lbs[d] + (1.0 - lbs[d]) * jax.nn.sigmoid(fraw)
        g = jnp.log(f)
        hi, mid, lo = _split3(g)
        b = _dot(tri[d], hi) + _dot(tri[d], mid) + _dot(tri[d], lo)
        return 1.0 - f, b

    def state_update(st, kk, b, v, d):
        b_last = b[ref_rows[d][1]:ref_rows[d][1] + 1]
        kd = (kk * jnp.exp(b_last - b)).astype(BF16)
        return st * jnp.exp(b_last) + _dot_tn(v, kd)

    def lat_step(st, c, d):
        r0 = pl.multiple_of(c * CHUNK, CHUNK)
        q = q_ref[0, 0, pl.ds(r0, CHUNK), :].astype(F32)
        v = v_ref[0, 0, pl.ds(r0, CHUNK), :]
        fraw = (ff_ref, fb_ref)[d][0, 0, pl.ds(r0, CHUNK), :]
        kk, b = decay(fraw, d)
        b_ref = b[ref_rows[d][0]:ref_rows[d][0] + 1]
        qd = (q * jnp.exp(b)).astype(BF16)
        qa = (q * jnp.exp(b - b_ref)).astype(BF16)
        ka = (kk * jnp.exp(b_ref - b)).astype(BF16)
        a = jnp.where(masks[d], _dot_nt(qa, ka), 0.0).astype(BF16)
        o = _dot_nt(qd, st.astype(BF16)) + _dot(a, v)
        acc_ref[pl.ds(r0, CHUNK), :] += o
        return state_update(st, kk, b, v, d)

    acc_ref[...] = jnp.zeros_like(acc_ref)
    st_f = jnp.zeros((HGRN_DIM, HGRN_DIM), F32)
    st_b = jnp.zeros((HGRN_DIM, HGRN_DIM), F32)
    for i in range(nc_ctx):
        cf, cb = i, nc_ctx - 1 - i
        v = vc_ref[0, 0, cf * CHUNK:(cf + 1) * CHUNK, :]
        kk, b = decay(ffc_ref[0, 0, cf * CHUNK:(cf + 1) * CHUNK, :], 0)
        st_f = state_update(st_f, kk, b, v, 0)
        v = vc_ref[0, 0, cb * CHUNK:(cb + 1) * CHUNK, :]
        kk, b = decay(fbc_ref[0, 0, cb * CHUNK:(cb + 1) * CHUNK, :], 1)
        st_b = state_update(st_b, kk, b, v, 1)

    def body(i, carry):
        s_f, s_b = carry
        return lat_step(s_f, i, 0), lat_step(s_b, nc_lat - 1 - i, 1)

    lax.fori_loop(0, nc_lat, body, (st_f, st_b))

    blk = 512
    def finish(i, _):
        r0 = pl.multiple_of(i * blk, blk)
        o = acc_ref[pl.ds(r0, blk), :]
        y = _rms(o) * gon_ref[...]
        o_ref[0, pl.ds(r0, blk), :] = (y * _silu(hg_ref[0, 0, pl.ds(r0, blk), :].astype(F32))
                                       ).astype(o_ref.dtype)
        return 0

    lax.fori_loop(0, n_lat // blk, finish, 0)


def _hgrn_call(vc, ffc, fbc, hq, hi, hg, ff, fb, lb_fwd, lb_bwd, g_on):
    B, H, n, _ = hq.shape
    n_ctx = vc.shape[2]
    ctx_blk = pl.BlockSpec((1, 1, n_ctx, 128), lambda b, h: (b, h, 0, 0))
    lat_blk = pl.BlockSpec((1, 1, n, 128), lambda b, h: (b, h, 0, 0))
    lb_blk = pl.BlockSpec((lb_fwd.shape[0], 128), lambda b, h: (0, h))
    return pl.pallas_call(
        _hgrn_kernel,
        grid=(B, H),
        in_specs=[ctx_blk, ctx_blk, ctx_blk, lat_blk, lat_blk, lat_blk, lat_blk, lat_blk,
                  lb_blk, lb_blk, pl.BlockSpec((1, 128), lambda b, h: (0, 0))],
        out_specs=pl.BlockSpec((1, n, 128), lambda b, h: (b, 0, h)),
        out_shape=jax.ShapeDtypeStruct((B, n, H * 128), BF16),
        scratch_shapes=[pltpu.VMEM((n, 128), F32)],
        compiler_params=pltpu.CompilerParams(
            dimension_semantics=("parallel", "parallel"), vmem_limit_bytes=VMEM_LIMIT_BYTES),
        name="hgrn",
    )(vc, ffc, fbc, hq, hi, hg, ff, fb, lb_fwd, lb_bwd, g_on)


def _out_ffn_kernel(x_ref, om_ref, oh_ref, mod_ref, gffn_ref, wout_ref, wg_ref, wu_ref, wd_ref,
                    gfin_ref, o_ref):
    x = x_ref[0]
    mix = _dot(om_ref[0], wout_ref[:HEADS * V_HEAD_DIM, :]) + _dot(oh_ref[0], wout_ref[HEADS * V_HEAD_DIM:, :])
    x1 = x + mod_ref[0, 2:3, :] * mix
    gain = gffn_ref[...] * (1.0 + mod_ref[0, 4:5, :])
    h2 = (_rms(x1) * gain + mod_ref[0, 3:4, :]).astype(BF16)
    acc = jnp.zeros((x.shape[0], D_MODEL), F32)
    for j in range(D_FF // FF_CHUNK):
        c0 = j * FF_CHUNK
        g = _dot(h2, wg_ref[:, c0:c0 + FF_CHUNK])
        u = _dot(h2, wu_ref[:, c0:c0 + FF_CHUNK])
        a = (_silu(g) * u).astype(BF16)
        acc = acc + _dot(a, wd_ref[c0:c0 + FF_CHUNK, :])
    x2 = x1 + mod_ref[0, 5:6, :] * acc
    o_ref[0] = _rms(x2) * gfin_ref[...]


def _out_ffn_call(x, o_mla, o_hgrn, mod_rows, g_ffn, w_out, w_gate, w_up, w_down, g_final, tm):
    B, n, _ = x.shape
    const = lambda b, i: (0, 0)
    resident = lambda shape: pl.BlockSpec(shape, const, pipeline_mode=pl.Buffered(1))
    return pl.pallas_call(
        _out_ffn_kernel,
        grid=(B, n // tm),
        in_specs=[pl.BlockSpec((1, tm, D_MODEL), lambda b, i: (b, i, 0)),
                  pl.BlockSpec((1, tm, 512), lambda b, i: (b, i, 0)),
                  pl.BlockSpec((1, tm, 512), lambda b, i: (b, i, 0)),
                  pl.BlockSpec((1, 8, D_MODEL), lambda b, i: (b, 0, 0)),
                  pl.BlockSpec((1, D_MODEL), const),
                  resident((D_MODEL, D_MODEL)),
                  resident((D_MODEL, D_FF)),
                  resident((D_MODEL, D_FF)),
                  resident((D_FF, D_MODEL)),
                  pl.BlockSpec((1, D_MODEL), const)],
        out_specs=pl.BlockSpec((1, tm, D_MODEL), lambda b, i: (b, i, 0)),
        out_shape=jax.ShapeDtypeStruct((B, n, D_MODEL), F32),
        compiler_params=pltpu.CompilerParams(
            dimension_semantics=("parallel", "parallel"), vmem_limit_bytes=VMEM_LIMIT_BYTES),
        name="out_ffn",
    )(x, o_mla, o_hgrn, mod_rows, g_ffn, w_out, w_gate, w_up, w_down, g_final)


_HALF_SWAP = np.concatenate([np.arange(16, 32), np.arange(0, 16), np.arange(48, 64), np.arange(32, 48)])


def _prep_weights(w_in, w_uq, w_ukv):
    offs = np.cumsum((0,) + IN_SIZES)
    cq, ckv, kpe, hq, hi, hg, ff, fb = (w_in[:, offs[i]:offs[i + 1]] for i in range(8))
    w_in_p = jnp.concatenate([cq, ckv, hq, hi, hg, ff, fb, kpe, kpe[:, _HALF_SWAP]], axis=1).astype(BF16)
    uq = w_uq.reshape(Q_LORA_RANK, HEADS, QK_HEAD_DIM)
    rope = uq[:, :, QK_NOPE_DIM:]
    w_uq_p = jnp.concatenate([uq[:, :, :QK_NOPE_DIM].reshape(Q_LORA_RANK, -1),
                              rope.reshape(Q_LORA_RANK, -1),
                              rope[:, :, _HALF_SWAP].reshape(Q_LORA_RANK, -1)], axis=1).astype(BF16)
    ukv = w_ukv.reshape(KV_LORA_RANK, HEADS, QK_NOPE_DIM + V_HEAD_DIM)
    w_ukv_p = jnp.concatenate([ukv[:, :, :QK_NOPE_DIM].reshape(KV_LORA_RANK, -1),
                               ukv[:, :, QK_NOPE_DIM:].reshape(KV_LORA_RANK, -1)], axis=1).astype(BF16)
    return w_in_p, w_uq_p, w_ukv_p


def _rope_tables(n):
    rows = n // GRID_W
    row = jnp.broadcast_to(jnp.arange(rows)[:, None], (rows, GRID_W)).reshape(n)
    col = jnp.broadcast_to(jnp.arange(GRID_W)[None, :], (rows, GRID_W)).reshape(n)
    axis_dim = QK_ROPE_DIM // 2
    inv = 1.0 / (ROPE_THETA ** (jnp.arange(0, axis_dim, 2, dtype=F32) / axis_dim))
    ang_r = row.astype(F32)[:, None] * inv
    ang_c = col.astype(F32)[:, None] * inv
    cos = jnp.concatenate([jnp.cos(ang_r)] * 2 + [jnp.cos(ang_c)] * 2, axis=-1)
    sin = jnp.concatenate([-jnp.sin(ang_r), jnp.sin(ang_r), -jnp.sin(ang_c), jnp.sin(ang_c)], axis=-1)
    return jnp.tile(cos, (1, HEADS)), jnp.tile(sin, (1, HEADS))


def kernel(x, c, ctx, c_ctx, w_mod, b_mod, g_norm_mix, g_norm_ffn, w_in, g_q_norm, w_uq, g_kv_norm,
           w_ukv, lb_fwd, lb_bwd, g_hgrn_norm, w_out, w_gate, w_up, w_down, g_final):
    B, N, D = x.shape
    L = ctx.shape[1]
    layer = 0

    cc = jnp.concatenate([c, c_ctx[None, :], jnp.zeros((7, D), F32)], axis=0)
    mod = _mod_call(cc, w_mod[layer], b_mod[layer][None, :])
    pad = jnp.zeros((B, 2, D), F32)
    mod_lat = jnp.concatenate([mod[:B].reshape(B, 6, D), pad], axis=1)
    mod_ctx = jnp.concatenate([jnp.broadcast_to(mod[B].reshape(1, 6, D), (B, 6, D)), pad], axis=1)

    w_in_p, w_uq_p, w_ukv_p = _prep_weights(w_in[layer], w_uq[layer], w_ukv[layer])
    cos4, sin4 = _rope_tables(N)
    row2 = lambda v: v.reshape(1, -1)
    proj_args = (row2(g_norm_mix[layer]), w_in_p, row2(g_q_norm[layer]), w_uq_p,
                 row2(g_kv_norm[layer]), w_ukv_p)

    q_l, k_l, v_l, hq_l, hi_l, hg_l, ff_l, fb_l = _in_proj_call(
        x, mod_lat, *proj_args, cos4, sin4, tm=512)
    _, k_c, v_c, _, hi_c, _, ff_c, fb_c = _in_proj_call(
        ctx, mod_ctx, *proj_args, jnp.ones((L, 256), F32), jnp.zeros((L, 256), F32), tm=L)

    o_mla = _attn_call(q_l, k_c, k_l, v_c, v_l, tq=512, tk=512)
    o_hgrn = _hgrn_call(hi_c, ff_c, fb_c, hq_l, hi_l, hg_l, ff_l, fb_l,
                        lb_fwd, lb_bwd, row2(g_hgrn_norm[layer]))

    return _out_ffn_call(x, o_mla, o_hgrn, mod_lat, row2(g_norm_ffn[layer]),
                         w_out[layer].astype(BF16), w_gate[layer].astype(BF16),
                         w_up[layer].astype(BF16), w_down[layer].astype(BF16),
                         row2(g_final), tm=512)
```

```python
import functools

import numpy as np
import jax
import jax.numpy as jnp
from jax import lax
from jax.experimental import pallas as pl
from jax.experimental.pallas import tpu as pltpu

D_MODEL = 1024
GRID_W = 64
EPS = 1e-6
ROPE_THETA = 10000.0
V_HEAD_DIM = 128
QK_NOPE_DIM = 128
QK_ROPE_DIM = 64
Q_LORA_RANK = 256
KV_LORA_RANK = 256
HEADS = 4
QK_HEAD_DIM = QK_NOPE_DIM + QK_ROPE_DIM
HGRN_DIM = 128
HGRN_WIDTH = HEADS * HGRN_DIM
CHUNK = 64
IN_SIZES = (Q_LORA_RANK, KV_LORA_RANK, QK_ROPE_DIM,
            HGRN_WIDTH, HGRN_WIDTH, HGRN_WIDTH, HGRN_WIDTH, HGRN_WIDTH)
D_FF = 2816
FF_CHUNK = 256
HGRN_BLOCK = 256
VMEM_LIMIT_BYTES = 56 * 1024 * 1024

BF16 = jnp.bfloat16
F32 = jnp.float32


def _dot(a, b):
    return jnp.dot(a, b, preferred_element_type=F32)


def _dot_nt(a, b):
    return lax.dot_general(a, b, (((1,), (1,)), ((), ())), preferred_element_type=F32)


def _dot_tn(a, b):
    return lax.dot_general(a, b, (((0,), (0,)), ((), ())), preferred_element_type=F32)


def _silu(x):
    return x * jax.nn.sigmoid(x)


def _rms(x):
    return x * lax.rsqrt(jnp.mean(x * x, axis=-1, keepdims=True) + EPS)


def _mod_kernel(c_ref, w_ref, b_ref, o_ref):
    a = _silu(c_ref[...]).astype(BF16)
    o_ref[...] = _dot(a, w_ref[...].astype(BF16)) + b_ref[...]


def _mod_call(cc, w_mod, b_mod):
    rows = cc.shape[0]
    cols = w_mod.shape[1]
    tn = 1024
    return pl.pallas_call(
        _mod_kernel,
        grid=(cols // tn,),
        in_specs=[pl.BlockSpec((rows, D_MODEL), lambda j: (0, 0)),
                  pl.BlockSpec((D_MODEL, tn), lambda j: (0, j)),
                  pl.BlockSpec((1, tn), lambda j: (0, j))],
        out_specs=pl.BlockSpec((rows, tn), lambda j: (0, j)),
        out_shape=jax.ShapeDtypeStruct((rows, cols), F32),
        compiler_params=pltpu.CompilerParams(dimension_semantics=("arbitrary",)),
        name="mod",
    )(cc, w_mod, b_mod)


_C_CQ = 0
_C_CKV = 256
_C_HGRN = 512
_C_KPE = 512 + 5 * HGRN_WIDTH
_IN_COLS = _C_KPE + 2 * QK_ROPE_DIM


def _in_proj_kernel(x_ref, mod_ref, gmix_ref, win_ref, gq_ref, wuq_ref, gkv_ref, wukv_ref,
                    cos_ref, sin_ref,
                    q_ref, k_ref, v_ref, hq_ref, hi_ref, hg_ref, ff_ref, fb_ref):
    x = x_ref[0]
    shift = mod_ref[0, 0:1, :]
    gain = gmix_ref[...] * (1.0 + mod_ref[0, 1:2, :])
    h = (_rms(x) * gain + shift).astype(BF16)

    cos4 = cos_ref[...]
    sin4 = sin_ref[...]

    lat = _dot(h, win_ref[:, _C_CQ:_C_HGRN])
    cq = (_rms(lat[:, :Q_LORA_RANK]) * gq_ref[...]).astype(BF16)
    ckv = (_rms(lat[:, Q_LORA_RANK:]) * gkv_ref[...]).astype(BF16)

    q = _dot(cq, wuq_ref[...])
    scale = 1.0 / float(np.sqrt(QK_HEAD_DIM))
    q_rope = q[:, 512:768] * cos4 + q[:, 768:1024] * sin4
    kv = _dot(ckv, wukv_ref[...])
    kp = _dot(h, win_ref[:, _C_KPE:_IN_COLS])
    k_rope = (kp[:, :QK_ROPE_DIM] * cos4[:, :QK_ROPE_DIM]
              + kp[:, QK_ROPE_DIM:] * sin4[:, :QK_ROPE_DIM]).astype(BF16)
    for hd in range(HEADS):
        q_ref[0, hd, :, :QK_NOPE_DIM] = (q[:, hd * 128:(hd + 1) * 128] * scale).astype(BF16)
        q_ref[0, hd, :, QK_NOPE_DIM:] = (q_rope[:, hd * 64:(hd + 1) * 64] * scale).astype(BF16)
        k_ref[0, hd, :, :QK_NOPE_DIM] = kv[:, hd * 128:(hd + 1) * 128].astype(BF16)
        k_ref[0, hd, :, QK_NOPE_DIM:] = k_rope
        v_ref[0, hd] = kv[:, 512 + hd * 128:512 + (hd + 1) * 128].astype(BF16)

    for j, o_ref in enumerate((hq_ref, hi_ref, hg_ref, ff_ref, fb_ref)):
        c0 = _C_HGRN + j * HGRN_WIDTH
        t = _dot(h, win_ref[:, c0:c0 + HGRN_WIDTH])
        for hd in range(HEADS):
            o_ref[0, hd] = t[:, hd * 128:(hd + 1) * 128].astype(o_ref.dtype)


def _in_proj_call(x, mod_rows, g_mix, w_in, g_q, w_uq, g_kv, w_ukv, cos4, sin4, tm):
    B, n, _ = x.shape
    const = lambda b, i: (0, 0)
    head_blk = lambda w: pl.BlockSpec((1, HEADS, tm, w), lambda b, i: (b, 0, i, 0))
    hshape = lambda w, dt: jax.ShapeDtypeStruct((B, HEADS, n, w), dt)
    return pl.pallas_call(
        _in_proj_kernel,
        grid=(B, n // tm),
        in_specs=[pl.BlockSpec((1, tm, D_MODEL), lambda b, i: (b, i, 0)),
                  pl.BlockSpec((1, 8, D_MODEL), lambda b, i: (b, 0, 0)),
                  pl.BlockSpec((1, D_MODEL), const),
                  pl.BlockSpec((D_MODEL, _IN_COLS), const),
                  pl.BlockSpec((1, Q_LORA_RANK), const),
                  pl.BlockSpec((Q_LORA_RANK, 1024), const),
                  pl.BlockSpec((1, KV_LORA_RANK), const),
                  pl.BlockSpec((KV_LORA_RANK, 1024), const),
                  pl.BlockSpec((tm, 256), lambda b, i: (i, 0)),
                  pl.BlockSpec((tm, 256), lambda b, i: (i, 0))],
        out_specs=[head_blk(QK_HEAD_DIM), head_blk(QK_HEAD_DIM), head_blk(V_HEAD_DIM),
                   head_blk(128), head_blk(128), head_blk(128), head_blk(128), head_blk(128)],
        out_shape=[hshape(QK_HEAD_DIM, BF16), hshape(QK_HEAD_DIM, BF16), hshape(V_HEAD_DIM, BF16),
                   hshape(128, BF16), hshape(128, BF16), hshape(128, BF16),
                   hshape(128, F32), hshape(128, F32)],
        compiler_params=pltpu.CompilerParams(
            dimension_semantics=("parallel", "parallel"), vmem_limit_bytes=VMEM_LIMIT_BYTES),
        name="in_proj",
    )(x, mod_rows, g_mix, w_in, g_q, w_uq, g_kv, w_ukv, cos4, sin4)


def _attn_kernel(q_ref, kc_ref, kl_ref, vc_ref, vl_ref, o_ref, *, tk):
    q = q_ref[0, 0]
    n_lat = kl_ref.shape[2]

    s = _dot_nt(q, kc_ref[0, 0])
    m = jnp.max(s, axis=-1, keepdims=True)
    p = jnp.exp(s - m)
    l = jnp.sum(p, axis=-1, keepdims=True)
    acc = _dot(p.astype(BF16), vc_ref[0, 0])

    for j in range(n_lat // tk):
        k = kl_ref[0, 0, j * tk:(j + 1) * tk, :]
        v = vl_ref[0, 0, j * tk:(j + 1) * tk, :]
        s = _dot_nt(q, k)
        m_new = jnp.maximum(m, jnp.max(s, axis=-1, keepdims=True))
        alpha = jnp.exp(m - m_new)
        p = jnp.exp(s - m_new)
        l = alpha * l + jnp.sum(p, axis=-1, keepdims=True)
        acc = alpha * acc + _dot(p.astype(BF16), v)
        m = m_new
    o_ref[0] = (acc / l).astype(o_ref.dtype)


def _attn_call(q, k_ctx, k_lat, v_ctx, v_lat, tq, tk):
    B, H, n, _ = q.shape
    n_ctx = k_ctx.shape[2]
    return pl.pallas_call(
        functools.partial(_attn_kernel, tk=tk),
        grid=(B, H, n // tq),
        in_specs=[pl.BlockSpec((1, 1, tq, QK_HEAD_DIM), lambda b, h, i: (b, h, i, 0)),
                  pl.BlockSpec((1, 1, n_ctx, QK_HEAD_DIM), lambda b, h, i: (b, h, 0, 0)),
                  pl.BlockSpec((1, 1, n, QK_HEAD_DIM), lambda b, h, i: (b, h, 0, 0)),
                  pl.BlockSpec((1, 1, n_ctx, V_HEAD_DIM), lambda b, h, i: (b, h, 0, 0)),
                  pl.BlockSpec((1, 1, n, V_HEAD_DIM), lambda b, h, i: (b, h, 0, 0))],
        out_specs=pl.BlockSpec((1, tq, V_HEAD_DIM), lambda b, h, i: (b, i, h)),
        out_shape=jax.ShapeDtypeStruct((B, n, H * V_HEAD_DIM), BF16),
        compiler_params=pltpu.CompilerParams(
            dimension_semantics=("parallel", "parallel", "arbitrary"),
            vmem_limit_bytes=VMEM_LIMIT_BYTES),
        name="attn",
    )(q, k_ctx, k_lat, v_ctx, v_lat)


def _split3(g):
    hi = g.astype(BF16)
    r = g - hi.astype(F32)
    mid = r.astype(BF16)
    lo = (r - mid.astype(F32)).astype(BF16)
    return hi, mid, lo


def _hgrn_kernel(vc_ref, ffc_ref, fbc_ref, q_ref, v_ref, hg_ref, ff_ref, fb_ref,
                 lbf_ref, lbb_ref, gon_ref, o_ref, accf_ref, accb_ref):
    n_ctx = vc_ref.shape[2]
    n_lat = q_ref.shape[2]
    nb_lat = n_lat // HGRN_BLOCK
    cpb = HGRN_BLOCK // CHUNK

    row = lax.broadcasted_iota(jnp.int32, (HGRN_BLOCK, HGRN_BLOCK), 0)
    col = lax.broadcasted_iota(jnp.int32, (HGRN_BLOCK, HGRN_BLOCK), 1)
    same_chunk = (row // CHUNK) == (col // CHUNK)
    masks = (same_chunk & (col <= row), same_chunk & (col >= row))
    tri = tuple(jnp.where(m, 1.0, 0.0).astype(BF16) for m in masks)
    ref_rows = ((CHUNK // 2 - 1, CHUNK - 1), (CHUNK // 2, 0))

    def lower_bound(tab_ref):
        t = tab_ref[...]
        e = jnp.exp(t - jnp.max(t, axis=0, keepdims=True))
        return e[0:1] / jnp.sum(e, axis=0, keepdims=True)

    lbs = (lower_bound(lbf_ref), lower_bound(lbb_ref))

    def chunk_rows(x, r):
        return jnp.concatenate(
            [jnp.broadcast_to(x[c * CHUNK + r:c * CHUNK + r + 1], (CHUNK, x.shape[1])) for c in range(cpb)],
            axis=0)

    def block(st, q, v, fraw, d):
        f = lbs[d] + (1.0 - lbs[d]) * jax.nn.sigmoid(fraw)
        kk = 1.0 - f
        hi, mid, lo = _split3(jnp.log(f))
        b = _dot(tri[d], hi) + _dot(tri[d], mid) + _dot(tri[d], lo)
        r_ref, r_last = ref_rows[d]
        b_last = chunk_rows(b, r_last)
        kd = (kk * jnp.exp(b_last - b)).astype(BF16)
        order = range(cpb) if d == 0 else range(cpb - 1, -1, -1)
        sl = lambda x, c: x[c * CHUNK:(c + 1) * CHUNK]
        upd = [_dot_tn(sl(v, c), sl(kd, c)) for c in range(cpb)]
        dec = [jnp.exp(b[c * CHUNK + r_last:c * CHUNK + r_last + 1]) for c in range(cpb)]
        if q is None:
            for c in order:
                st = st * dec[c] + upd[c]
            return st, None
        b_ref = chunk_rows(b, r_ref)
        qd = (q * jnp.exp(b)).astype(BF16)
        qa = (q * jnp.exp(b - b_ref)).astype(BF16)
        ka = (kk * jnp.exp(b_ref - b)).astype(BF16)
        a = jnp.where(masks[d], _dot_nt(qa, ka), 0.0).astype(BF16)
        o_intra = _dot(a, v)
        inter = [None] * cpb
        for c in order:
            inter[c] = _dot_nt(sl(qd, c), st.astype(BF16))
            st = st * dec[c] + upd[c]
        return st, o_intra + jnp.concatenate(inter, axis=0)

    st_f = jnp.zeros((HGRN_DIM, HGRN_DIM), F32)
    st_b = jnp.zeros((HGRN_DIM, HGRN_DIM), F32)
    for i in range(n_ctx // HGRN_BLOCK):
        rf = slice(i * HGRN_BLOCK, (i + 1) * HGRN_BLOCK)
        rb = slice(n_ctx - (i + 1) * HGRN_BLOCK, n_ctx - i * HGRN_BLOCK)
        st_f, _ = block(st_f, None, vc_ref[0, 0, rf, :], ffc_ref[0, 0, rf, :], 0)
        st_b, _ = block(st_b, None, vc_ref[0, 0, rb, :], fbc_ref[0, 0, rb, :], 1)

    def body(i, carry):
        s_f, s_b = carry
        rf = pl.ds(pl.multiple_of(i * HGRN_BLOCK, HGRN_BLOCK), HGRN_BLOCK)
        rb = pl.ds(pl.multiple_of((nb_lat - 1 - i) * HGRN_BLOCK, HGRN_BLOCK), HGRN_BLOCK)
        qf, vf, gf = q_ref[0, 0, rf, :].astype(F32), v_ref[0, 0, rf, :], ff_ref[0, 0, rf, :]
        qb, vb, gb = q_ref[0, 0, rb, :].astype(F32), v_ref[0, 0, rb, :], fb_ref[0, 0, rb, :]
        s_f, o_f = block(s_f, qf, vf, gf, 0)
        s_b, o_b = block(s_b, qb, vb, gb, 1)
        accf_ref[rf, :] = o_f
        accb_ref[rb, :] = o_b
        return s_f, s_b

    lax.fori_loop(0, nb_lat, body, (st_f, st_b), unroll=2)

    blk = 512
    def finish(i, _):
        r0 = pl.multiple_of(i * blk, blk)
        o = accf_ref[pl.ds(r0, blk), :] + accb_ref[pl.ds(r0, blk), :]
        y = _rms(o) * gon_ref[...]
        o_ref[0, pl.ds(r0, blk), :] = (y * _silu(hg_ref[0, 0, pl.ds(r0, blk), :].astype(F32))
                                       ).astype(o_ref.dtype)
        return 0

    lax.fori_loop(0, n_lat // blk, finish, 0)


def _hgrn_call(vc, ffc, fbc, hq, hi, hg, ff, fb, lb_fwd, lb_bwd, g_on):
    B, H, n, _ = hq.shape
    n_ctx = vc.shape[2]
    ctx_blk = pl.BlockSpec((1, 1, n_ctx, 128), lambda b, h: (b, h, 0, 0))
    lat_blk = pl.BlockSpec((1, 1, n, 128), lambda b, h: (b, h, 0, 0))
    lb_blk = pl.BlockSpec((lb_fwd.shape[0], 128), lambda b, h: (0, h))
    return pl.pallas_call(
        _hgrn_kernel,
        grid=(B, H),
        in_specs=[ctx_blk, ctx_blk, ctx_blk, lat_blk, lat_blk, lat_blk, lat_blk, lat_blk,
                  lb_blk, lb_blk, pl.BlockSpec((1, 128), lambda b, h: (0, 0))],
        out_specs=pl.BlockSpec((1, n, 128), lambda b, h: (b, 0, h)),
        out_shape=jax.ShapeDtypeStruct((B, n, H * 128), BF16),
        scratch_shapes=[pltpu.VMEM((n, 128), F32), pltpu.VMEM((n, 128), F32)],
        compiler_params=pltpu.CompilerParams(
            dimension_semantics=("parallel", "parallel"), vmem_limit_bytes=VMEM_LIMIT_BYTES),
        name="hgrn",
    )(vc, ffc, fbc, hq, hi, hg, ff, fb, lb_fwd, lb_bwd, g_on)


def _out_ffn_kernel(x_ref, om_ref, oh_ref, mod_ref, gffn_ref, wout_ref, wg_ref, wu_ref, wd_ref,
                    gfin_ref, o_ref):
    x = x_ref[0]
    mix = _dot(om_ref[0], wout_ref[:HEADS * V_HEAD_DIM, :]) + _dot(oh_ref[0], wout_ref[HEADS * V_HEAD_DIM:, :])
    x1 = x + mod_ref[0, 2:3, :] * mix
    gain = gffn_ref[...] * (1.0 + mod_ref[0, 4:5, :])
    h2 = (_rms(x1) * gain + mod_ref[0, 3:4, :]).astype(BF16)
    acc = jnp.zeros((x.shape[0], D_MODEL), F32)
    for j in range(D_FF // FF_CHUNK):
        c0 = j * FF_CHUNK
        g = _dot(h2, wg_ref[:, c0:c0 + FF_CHUNK])
        u = _dot(h2, wu_ref[:, c0:c0 + FF_CHUNK])
        a = (_silu(g) * u).astype(BF16)
        acc = acc + _dot(a, wd_ref[c0:c0 + FF_CHUNK, :])
    x2 = x1 + mod_ref[0, 5:6, :] * acc
    o_ref[0] = _rms(x2) * gfin_ref[...]


def _out_ffn_call(x, o_mla, o_hgrn, mod_rows, g_ffn, w_out, w_gate, w_up, w_down, g_final, tm):
    B, n, _ = x.shape
    const = lambda b, i: (0, 0)
    resident = lambda shape: pl.BlockSpec(shape, const, pipeline_mode=pl.Buffered(1))
    return pl.pallas_call(
        _out_ffn_kernel,
        grid=(B, n // tm),
        in_specs=[pl.BlockSpec((1, tm, D_MODEL), lambda b, i: (b, i, 0)),
                  pl.BlockSpec((1, tm, 512), lambda b, i: (b, i, 0)),
                  pl.BlockSpec((1, tm, 512), lambda b, i: (b, i, 0)),
                  pl.BlockSpec((1, 8, D_MODEL), lambda b, i: (b, 0, 0)),
                  pl.BlockSpec((1, D_MODEL), const),
                  resident((D_MODEL, D_MODEL)),
                  resident((D_MODEL, D_FF)),
                  resident((D_MODEL, D_FF)),
                  resident((D_FF, D_MODEL)),
                  pl.BlockSpec((1, D_MODEL), const)],
        out_specs=pl.BlockSpec((1, tm, D_MODEL), lambda b, i: (b, i, 0)),
        out_shape=jax.ShapeDtypeStruct((B, n, D_MODEL), F32),
        compiler_params=pltpu.CompilerParams(
            dimension_semantics=("parallel", "parallel"), vmem_limit_bytes=VMEM_LIMIT_BYTES),
        name="out_ffn",
    )(x, o_mla, o_hgrn, mod_rows, g_ffn, w_out, w_gate, w_up, w_down, g_final)


_HALF_SWAP = np.concatenate([np.arange(16, 32), np.arange(0, 16), np.arange(48, 64), np.arange(32, 48)])


def _prep_weights(w_in, w_uq, w_ukv):
    offs = np.cumsum((0,) + IN_SIZES)
    cq, ckv, kpe, hq, hi, hg, ff, fb = (w_in[:, offs[i]:offs[i + 1]] for i in range(8))
    w_in_p = jnp.concatenate([cq, ckv, hq, hi, hg, ff, fb, kpe, kpe[:, _HALF_SWAP]], axis=1).astype(BF16)
    uq = w_uq.reshape(Q_LORA_RANK, HEADS, QK_HEAD_DIM)
    rope = uq[:, :, QK_NOPE_DIM:]
    w_uq_p = jnp.concatenate([uq[:, :, :QK_NOPE_DIM].reshape(Q_LORA_RANK, -1),
                              rope.reshape(Q_LORA_RANK, -1),
                              rope[:, :, _HALF_SWAP].reshape(Q_LORA_RANK, -1)], axis=1).astype(BF16)
    ukv = w_ukv.reshape(KV_LORA_RANK, HEADS, QK_NOPE_DIM + V_HEAD_DIM)
    w_ukv_p = jnp.concatenate([ukv[:, :, :QK_NOPE_DIM].reshape(KV_LORA_RANK, -1),
                               ukv[:, :, QK_NOPE_DIM:].reshape(KV_LORA_RANK, -1)], axis=1).astype(BF16)
    return w_in_p, w_uq_p, w_ukv_p


def _rope_tables(n):
    rows = n // GRID_W
    row = jnp.broadcast_to(jnp.arange(rows)[:, None], (rows, GRID_W)).reshape(n)
    col = jnp.broadcast_to(jnp.arange(GRID_W)[None, :], (rows, GRID_W)).reshape(n)
    axis_dim = QK_ROPE_DIM // 2
    inv = 1.0 / (ROPE_THETA ** (jnp.arange(0, axis_dim, 2, dtype=F32) / axis_dim))
    ang_r = row.astype(F32)[:, None] * inv
    ang_c = col.astype(F32)[:, None] * inv
    cos = jnp.concatenate([jnp.cos(ang_r)] * 2 + [jnp.cos(ang_c)] * 2, axis=-1)
    sin = jnp.concatenate([-jnp.sin(ang_r), jnp.sin(ang_r), -jnp.sin(ang_c), jnp.sin(ang_c)], axis=-1)
    return jnp.tile(cos, (1, HEADS)), jnp.tile(sin, (1, HEADS))


def kernel(x, c, ctx, c_ctx, w_mod, b_mod, g_norm_mix, g_norm_ffn, w_in, g_q_norm, w_uq, g_kv_norm,
           w_ukv, lb_fwd, lb_bwd, g_hgrn_norm, w_out, w_gate, w_up, w_down, g_final):
    B, N, D = x.shape
    L = ctx.shape[1]
    layer = 0

    cc = jnp.concatenate([c, c_ctx[None, :], jnp.zeros((7, D), F32)], axis=0)
    mod = _mod_call(cc, w_mod[layer], b_mod[layer][None, :])
    pad = jnp.zeros((B, 2, D), F32)
    mod_lat = jnp.concatenate([mod[:B].reshape(B, 6, D), pad], axis=1)
    mod_ctx = jnp.concatenate([jnp.broadcast_to(mod[B].reshape(1, 6, D), (B, 6, D)), pad], axis=1)

    w_in_p, w_uq_p, w_ukv_p = _prep_weights(w_in[layer], w_uq[layer], w_ukv[layer])
    cos4, sin4 = _rope_tables(N)
    row2 = lambda v: v.reshape(1, -1)
    proj_args = (row2(g_norm_mix[layer]), w_in_p, row2(g_q_norm[layer]), w_uq_p,
                 row2(g_kv_norm[layer]), w_ukv_p)

    q_l, k_l, v_l, hq_l, hi_l, hg_l, ff_l, fb_l = _in_proj_call(
        x, mod_lat, *proj_args, cos4, sin4, tm=512)
    _, k_c, v_c, _, hi_c, _, ff_c, fb_c = _in_proj_call(
        ctx, mod_ctx, *proj_args, jnp.ones((L, 256), F32), jnp.zeros((L, 256), F32), tm=L)

    o_mla = _attn_call(q_l, k_c, k_l, v_c, v_l, tq=512, tk=512)
    o_hgrn = _hgrn_call(hi_c, ff_c, fb_c, hq_l, hi_l, hg_l, ff_l, fb_l,
                        lb_fwd, lb_bwd, row2(g_hgrn_norm[layer]))

    return _out_ffn_call(x, o_mla, o_hgrn, mod_lat, row2(g_norm_ffn[layer]),
                         w_out[layer].astype(BF16), w_gate[layer].astype(BF16),
                         w_up[layer].astype(BF16), w_down[layer].astype(BF16),
                         row2(g_final), tm=512)
```

```python
import functools

import numpy as np
import jax
import jax.numpy as jnp
from jax import lax
from jax.experimental import pallas as pl
from jax.experimental.pallas import tpu as pltpu

D_MODEL = 1024
GRID_W = 64
EPS = 1e-6
ROPE_THETA = 10000.0
V_HEAD_DIM = 128
QK_NOPE_DIM = 128
QK_ROPE_DIM = 64
Q_LORA_RANK = 256
KV_LORA_RANK = 256
HEADS = 4
QK_HEAD_DIM = QK_NOPE_DIM + QK_ROPE_DIM
HGRN_DIM = 128
HGRN_WIDTH = HEADS * HGRN_DIM
CHUNK = 64
IN_SIZES = (Q_LORA_RANK, KV_LORA_RANK, QK_ROPE_DIM,
            HGRN_WIDTH, HGRN_WIDTH, HGRN_WIDTH, HGRN_WIDTH, HGRN_WIDTH)
D_FF = 2816
FF_CHUNK = 256
HGRN_BLOCK = 256
VMEM_LIMIT_BYTES = 56 * 1024 * 1024
Q_SCALE = float(np.log2(np.e) / np.sqrt(QK_HEAD_DIM))

BF16 = jnp.bfloat16
F32 = jnp.float32


def _dot(a, b):
    return jnp.dot(a, b, preferred_element_type=F32)


def _dot_nt(a, b):
    return lax.dot_general(a, b, (((1,), (1,)), ((), ())), preferred_element_type=F32)


def _dot_tn(a, b):
    return lax.dot_general(a, b, (((0,), (0,)), ((), ())), preferred_element_type=F32)


def _silu(x):
    return x * jax.nn.sigmoid(x)


def _rms(x):
    return x * lax.rsqrt(jnp.mean(x * x, axis=-1, keepdims=True) + EPS)


def _mod_kernel(c_ref, w_ref, b_ref, o_ref):
    a = _silu(c_ref[...]).astype(BF16)
    o_ref[...] = _dot(a, w_ref[...].astype(BF16)) + b_ref[...]


def _mod_call(cc, w_mod, b_mod):
    rows = cc.shape[0]
    cols = w_mod.shape[1]
    tn = 1024
    return pl.pallas_call(
        _mod_kernel,
        grid=(cols // tn,),
        in_specs=[pl.BlockSpec((rows, D_MODEL), lambda j: (0, 0)),
                  pl.BlockSpec((D_MODEL, tn), lambda j: (0, j)),
                  pl.BlockSpec((1, tn), lambda j: (0, j))],
        out_specs=pl.BlockSpec((rows, tn), lambda j: (0, j)),
        out_shape=jax.ShapeDtypeStruct((rows, cols), F32),
        compiler_params=pltpu.CompilerParams(dimension_semantics=("arbitrary",)),
        name="mod",
    )(cc, w_mod, b_mod)


_C_CQ = 0
_C_CKV = 256
_C_HGRN = 512
_C_KPE = 512 + 5 * HGRN_WIDTH
_IN_COLS = _C_KPE + 2 * QK_ROPE_DIM


def _in_proj_kernel(x_ref, mod_ref, gmix_ref, win_ref, gq_ref, wuq_ref, gkv_ref, wukv_ref,
                    cos_ref, sin_ref,
                    qt_ref, k_ref, vt_ref, hq_ref, hi_ref, hg_ref, ff_ref, fb_ref):
    x = x_ref[0]
    shift = mod_ref[0, 0:1, :]
    gain = gmix_ref[...] * (1.0 + mod_ref[0, 1:2, :])
    h = (_rms(x) * gain + shift).astype(BF16)

    cos4 = cos_ref[...]
    sin4 = sin_ref[...]

    lat = _dot(h, win_ref[:, _C_CQ:_C_HGRN])
    cq = (_rms(lat[:, :Q_LORA_RANK]) * gq_ref[...]).astype(BF16)
    ckv = (_rms(lat[:, Q_LORA_RANK:]) * gkv_ref[...]).astype(BF16)

    q = _dot(cq, wuq_ref[...])
    q_nope_t = (q[:, :512] * Q_SCALE).T
    q_rope_t = ((q[:, 512:768] * cos4 + q[:, 768:1024] * sin4) * Q_SCALE).T
    kv = _dot(ckv, wukv_ref[...])
    kp = _dot(h, win_ref[:, _C_KPE:_IN_COLS])
    k_rope = (kp[:, :QK_ROPE_DIM] * cos4[:, :QK_ROPE_DIM]
              + kp[:, QK_ROPE_DIM:] * sin4[:, :QK_ROPE_DIM]).astype(BF16)
    v_t = kv[:, 512:].T
    for hd in range(HEADS):
        qt_ref[0, hd, :QK_NOPE_DIM, :] = q_nope_t[hd * 128:(hd + 1) * 128].astype(BF16)
        qt_ref[0, hd, QK_NOPE_DIM:, :] = q_rope_t[hd * 64:(hd + 1) * 64].astype(BF16)
        k_ref[0, hd, :, :QK_NOPE_DIM] = kv[:, hd * 128:(hd + 1) * 128].astype(BF16)
        k_ref[0, hd, :, QK_NOPE_DIM:] = k_rope
        vt_ref[0, hd] = v_t[hd * 128:(hd + 1) * 128].astype(BF16)

    for j, o_ref in enumerate((hq_ref, hi_ref, hg_ref, ff_ref, fb_ref)):
        c0 = _C_HGRN + j * HGRN_WIDTH
        t = _dot(h, win_ref[:, c0:c0 + HGRN_WIDTH])
        for hd in range(HEADS):
            o_ref[0, hd] = t[:, hd * 128:(hd + 1) * 128].astype(o_ref.dtype)


def _in_proj_call(x, mod_rows, g_mix, w_in, g_q, w_uq, g_kv, w_ukv, cos4, sin4, tm):
    B, n, _ = x.shape
    const = lambda b, i: (0, 0)
    head_blk = lambda w: pl.BlockSpec((1, HEADS, tm, w), lambda b, i: (b, 0, i, 0))
    hshape = lambda w, dt: jax.ShapeDtypeStruct((B, HEADS, n, w), dt)
    head_blk_t = lambda w: pl.BlockSpec((1, HEADS, w, tm), lambda b, i: (b, 0, 0, i))
    hshape_t = lambda w: jax.ShapeDtypeStruct((B, HEADS, w, n), BF16)
    return pl.pallas_call(
        _in_proj_kernel,
        grid=(B, n // tm),
        in_specs=[pl.BlockSpec((1, tm, D_MODEL), lambda b, i: (b, i, 0)),
                  pl.BlockSpec((1, 8, D_MODEL), lambda b, i: (b, 0, 0)),
                  pl.BlockSpec((1, D_MODEL), const),
                  pl.BlockSpec((D_MODEL, _IN_COLS), const),
                  pl.BlockSpec((1, Q_LORA_RANK), const),
                  pl.BlockSpec((Q_LORA_RANK, 1024), const),
                  pl.BlockSpec((1, KV_LORA_RANK), const),
                  pl.BlockSpec((KV_LORA_RANK, 1024), const),
                  pl.BlockSpec((tm, 256), lambda b, i: (i, 0)),
                  pl.BlockSpec((tm, 256), lambda b, i: (i, 0))],
        out_specs=[head_blk_t(QK_HEAD_DIM), head_blk(QK_HEAD_DIM), head_blk_t(V_HEAD_DIM),
                   head_blk(128), head_blk(128), head_blk(128), head_blk(128), head_blk(128)],
        out_shape=[hshape_t(QK_HEAD_DIM), hshape(QK_HEAD_DIM, BF16), hshape_t(V_HEAD_DIM),
                   hshape(128, BF16), hshape(128, BF16), hshape(128, BF16),
                   hshape(128, F32), hshape(128, F32)],
        compiler_params=pltpu.CompilerParams(
            dimension_semantics=("parallel", "parallel"), vmem_limit_bytes=VMEM_LIMIT_BYTES),
        name="in_proj",
    )(x, mod_rows, g_mix, w_in, g_q, w_uq, g_kv, w_ukv, cos4, sin4)


def _attn_kernel(qt_ref, kc_ref, kl_ref, vtc_ref, vtl_ref, o_ref, s_ref, *, tk):
    qt = qt_ref[0, 0]
    n_ctx = kc_ref.shape[2]
    n_lat = kl_ref.shape[2]
    chunks = [(n_ctx, lambda: kc_ref[0, 0], lambda: vtc_ref[0, 0])]
    for j in range(n_lat // tk):
        chunks.append((tk, lambda j=j: kl_ref[0, 0, j * tk:(j + 1) * tk, :],
                       lambda j=j: vtl_ref[0, 0, :, j * tk:(j + 1) * tk]))

    def scores(j):
        rows, keys, _ = chunks[j]
        s_ref[j % 2, :rows, :] = _dot(keys(), qt)

    scores(0)
    m = l = acc = None
    for j, (rows, _, values_t) in enumerate(chunks):
        if j + 1 < len(chunks):
            scores(j + 1)
        s = s_ref[j % 2, :rows, :]
        m_chunk = jnp.max(s, axis=0, keepdims=True)
        m_new = m_chunk if m is None else jnp.maximum(m, m_chunk)
        p = jnp.exp2(s - m_new)
        l_chunk = jnp.sum(p, axis=0, keepdims=True)
        pv = _dot(values_t(), p.astype(BF16))
        if m is None:
            l, acc = l_chunk, pv
        else:
            alpha = jnp.exp2(m - m_new)
            l = alpha * l + l_chunk
            acc = alpha * acc + pv
        m = m_new
    o_ref[0] = (acc * (1.0 / l)).T.astype(o_ref.dtype)


def _attn_call(q_t, k_ctx, k_lat, vt_ctx, vt_lat, tq, tk):
    B, H, _, n = q_t.shape
    n_ctx = k_ctx.shape[2]
    return pl.pallas_call(
        functools.partial(_attn_kernel, tk=tk),
        grid=(B, H, n // tq),
        in_specs=[pl.BlockSpec((1, 1, QK_HEAD_DIM, tq), lambda b, h, i: (b, h, 0, i)),
                  pl.BlockSpec((1, 1, n_ctx, QK_HEAD_DIM), lambda b, h, i: (b, h, 0, 0)),
                  pl.BlockSpec((1, 1, n, QK_HEAD_DIM), lambda b, h, i: (b, h, 0, 0)),
                  pl.BlockSpec((1, 1, V_HEAD_DIM, n_ctx), lambda b, h, i: (b, h, 0, 0)),
                  pl.BlockSpec((1, 1, V_HEAD_DIM, n), lambda b, h, i: (b, h, 0, 0))],
        out_specs=pl.BlockSpec((1, tq, V_HEAD_DIM), lambda b, h, i: (b, i, h)),
        out_shape=jax.ShapeDtypeStruct((B, n, H * V_HEAD_DIM), BF16),
        scratch_shapes=[pltpu.VMEM((2, max(tk, n_ctx), tq), F32)],
        compiler_params=pltpu.CompilerParams(
            dimension_semantics=("parallel", "parallel", "arbitrary"),
            vmem_limit_bytes=VMEM_LIMIT_BYTES),
        name="attn",
    )(q_t, k_ctx, k_lat, vt_ctx, vt_lat)


def _split3(g):
    hi = g.astype(BF16)
    r = g - hi.astype(F32)
    mid = r.astype(BF16)
    lo = (r - mid.astype(F32)).astype(BF16)
    return hi, mid, lo


def _hgrn_kernel(vc_ref, ffc_ref, fbc_ref, q_ref, v_ref, hg_ref, ff_ref, fb_ref,
                 lbf_ref, lbb_ref, gon_ref, o_ref, accf_ref, accb_ref):
    n_ctx = vc_ref.shape[2]
    n_lat = q_ref.shape[2]
    nb_lat = n_lat // HGRN_BLOCK
    cpb = HGRN_BLOCK // CHUNK

    row = lax.broadcasted_iota(jnp.int32, (HGRN_BLOCK, HGRN_BLOCK), 0)
    col = lax.broadcasted_iota(jnp.int32, (HGRN_BLOCK, HGRN_BLOCK), 1)
    same_chunk = (row // CHUNK) == (col // CHUNK)
    masks = (same_chunk & (col <= row), same_chunk & (col >= row))
    tri = tuple(jnp.where(m, 1.0, 0.0).astype(BF16) for m in masks)
    ref_rows = ((CHUNK // 2 - 1, CHUNK - 1), (CHUNK // 2, 0))

    def lower_bound(tab_ref):
        t = tab_ref[...]
        e = jnp.exp(t - jnp.max(t, axis=0, keepdims=True))
        return e[0:1] / jnp.sum(e, axis=0, keepdims=True)

    lbs = (lower_bound(lbf_ref), lower_bound(lbb_ref))

    def chunk_rows(x, r):
        return jnp.concatenate(
            [jnp.broadcast_to(x[c * CHUNK + r:c * CHUNK + r + 1], (CHUNK, x.shape[1])) for c in range(cpb)],
            axis=0)

    def block(st, q, v, fraw, d):
        f = lbs[d] + (1.0 - lbs[d]) * jax.nn.sigmoid(fraw)
        kk = 1.0 - f
        hi, mid, lo = _split3(jnp.log(f))
        b = _dot(tri[d], hi) + _dot(tri[d], mid) + _dot(tri[d], lo)
        r_ref, r_last = ref_rows[d]
        b_last = chunk_rows(b, r_last)
        kd = (kk * jnp.exp(b_last - b)).astype(BF16)
        order = range(cpb) if d == 0 else range(cpb - 1, -1, -1)
        sl = lambda x, c: x[c * CHUNK:(c + 1) * CHUNK]
        upd = [_dot_tn(sl(v, c), sl(kd, c)) for c in range(cpb)]
        dec = [jnp.exp(b[c * CHUNK + r_last:c * CHUNK + r_last + 1]) for c in range(cpb)]
        if q is None:
            for c in order:
                st = st * dec[c] + upd[c]
            return st, None
        b_ref = chunk_rows(b, r_ref)
        qd = (q * jnp.exp(b)).astype(BF16)
        qa = (q * jnp.exp(b - b_ref)).astype(BF16)
        ka = (kk * jnp.exp(b_ref - b)).astype(BF16)
        a = jnp.where(masks[d], _dot_nt(qa, ka), 0.0).astype(BF16)
        o_intra = _dot(a, v)
        inter = [None] * cpb
        for c in order:
            inter[c] = _dot_nt(sl(qd, c), st.astype(BF16))
            st = st * dec[c] + upd[c]
        return st, o_intra + jnp.concatenate(inter, axis=0)

    st_f = jnp.zeros((HGRN_DIM, HGRN_DIM), F32)
    st_b = jnp.zeros((HGRN_DIM, HGRN_DIM), F32)
    for i in range(n_ctx // HGRN_BLOCK):
        rf = slice(i * HGRN_BLOCK, (i + 1) * HGRN_BLOCK)
        rb = slice(n_ctx - (i + 1) * HGRN_BLOCK, n_ctx - i * HGRN_BLOCK)
        st_f, _ = block(st_f, None, vc_ref[0, 0, rf, :], ffc_ref[0, 0, rf, :], 0)
        st_b, _ = block(st_b, None, vc_ref[0, 0, rb, :], fbc_ref[0, 0, rb, :], 1)

    def body(i, carry):
        s_f, s_b = carry
        rf = pl.ds(pl.multiple_of(i * HGRN_BLOCK, HGRN_BLOCK), HGRN_BLOCK)
        rb = pl.ds(pl.multiple_of((nb_lat - 1 - i) * HGRN_BLOCK, HGRN_BLOCK), HGRN_BLOCK)
        qf, vf, gf = q_ref[0, 0, rf, :].astype(F32), v_ref[0, 0, rf, :], ff_ref[0, 0, rf, :]
        qb, vb, gb = q_ref[0, 0, rb, :].astype(F32), v_ref[0, 0, rb, :], fb_ref[0, 0, rb, :]
        s_f, o_f = block(s_f, qf, vf, gf, 0)
        s_b, o_b = block(s_b, qb, vb, gb, 1)
        accf_ref[rf, :] = o_f
        accb_ref[rb, :] = o_b
        return s_f, s_b

    lax.fori_loop(0, nb_lat, body, (st_f, st_b), unroll=2)

    blk = 512
    def finish(i, _):
        r0 = pl.multiple_of(i * blk, blk)
        o = accf_ref[pl.ds(r0, blk), :] + accb_ref[pl.ds(r0, blk), :]
        y = _rms(o) * gon_ref[...]
        o_ref[0, pl.ds(r0, blk), :] = (y * _silu(hg_ref[0, 0, pl.ds(r0, blk), :].astype(F32))
                                       ).astype(o_ref.dtype)
        return 0

    lax.fori_loop(0, n_lat // blk, finish, 0)


def _hgrn_call(vc, ffc, fbc, hq, hi, hg, ff, fb, lb_fwd, lb_bwd, g_on):
    B, H, n, _ = hq.shape
    n_ctx = vc.shape[2]
    ctx_blk = pl.BlockSpec((1, 1, n_ctx, 128), lambda b, h: (b, h, 0, 0))
    lat_blk = pl.BlockSpec((1, 1, n, 128), lambda b, h: (b, h, 0, 0))
    lb_blk = pl.BlockSpec((lb_fwd.shape[0], 128), lambda b, h: (0, h))
    return pl.pallas_call(
        _hgrn_kernel,
        grid=(B, H),
        in_specs=[ctx_blk, ctx_blk, ctx_blk, lat_blk, lat_blk, lat_blk, lat_blk, lat_blk,
                  lb_blk, lb_blk, pl.BlockSpec((1, 128), lambda b, h: (0, 0))],
        out_specs=pl.BlockSpec((1, n, 128), lambda b, h: (b, 0, h)),
        out_shape=jax.ShapeDtypeStruct((B, n, H * 128), BF16),
        scratch_shapes=[pltpu.VMEM((n, 128), F32), pltpu.VMEM((n, 128), F32)],
        compiler_params=pltpu.CompilerParams(
            dimension_semantics=("parallel", "parallel"), vmem_limit_bytes=VMEM_LIMIT_BYTES),
        name="hgrn",
    )(vc, ffc, fbc, hq, hi, hg, ff, fb, lb_fwd, lb_bwd, g_on)


def _out_ffn_kernel(x_ref, om_ref, oh_ref, mod_ref, gffn_ref, wout_ref, wg_ref, wu_ref, wd_ref,
                    gfin_ref, o_ref):
    x = x_ref[0]
    mix = _dot(om_ref[0], wout_ref[:HEADS * V_HEAD_DIM, :]) + _dot(oh_ref[0], wout_ref[HEADS * V_HEAD_DIM:, :])
    x1 = x + mod_ref[0, 2:3, :] * mix
    gain = gffn_ref[...] * (1.0 + mod_ref[0, 4:5, :])
    h2 = (_rms(x1) * gain + mod_ref[0, 3:4, :]).astype(BF16)
    acc = jnp.zeros((x.shape[0], D_MODEL), F32)
    for j in range(D_FF // FF_CHUNK):
        c0 = j * FF_CHUNK
        g = _dot(h2, wg_ref[:, c0:c0 + FF_CHUNK])
        u = _dot(h2, wu_ref[:, c0:c0 + FF_CHUNK])
        a = (_silu(g) * u).astype(BF16)
        acc = acc + _dot(a, wd_ref[c0:c0 + FF_CHUNK, :])
    x2 = x1 + mod_ref[0, 5:6, :] * acc
    o_ref[0] = _rms(x2) * gfin_ref[...]


def _out_ffn_call(x, o_mla, o_hgrn, mod_rows, g_ffn, w_out, w_gate, w_up, w_down, g_final, tm):
    B, n, _ = x.shape
    const = lambda b, i: (0, 0)
    resident = lambda shape: pl.BlockSpec(shape, const, pipeline_mode=pl.Buffered(1))
    return pl.pallas_call(
        _out_ffn_kernel,
        grid=(B, n // tm),
        in_specs=[pl.BlockSpec((1, tm, D_MODEL), lambda b, i: (b, i, 0)),
                  pl.BlockSpec((1, tm, 512), lambda b, i: (b, i, 0)),
                  pl.BlockSpec((1, tm, 512), lambda b, i: (b, i, 0)),
                  pl.BlockSpec((1, 8, D_MODEL), lambda b, i: (b, 0, 0)),
                  pl.BlockSpec((1, D_MODEL), const),
                  resident((D_MODEL, D_MODEL)),
                  resident((D_MODEL, D_FF)),
                  resident((D_MODEL, D_FF)),
                  resident((D_FF, D_MODEL)),
                  pl.BlockSpec((1, D_MODEL), const)],
        out_specs=pl.BlockSpec((1, tm, D_MODEL), lambda b, i: (b, i, 0)),
        out_shape=jax.ShapeDtypeStruct((B, n, D_MODEL), F32),
        compiler_params=pltpu.CompilerParams(
            dimension_semantics=("parallel", "parallel"), vmem_limit_bytes=VMEM_LIMIT_BYTES),
        name="out_ffn",
    )(x, o_mla, o_hgrn, mod_rows, g_ffn, w_out, w_gate, w_up, w_down, g_final)


_HALF_SWAP = np.concatenate([np.arange(16, 32), np.arange(0, 16), np.arange(48, 64), np.arange(32, 48)])


def _prep_weights(w_in, w_uq, w_ukv):
    offs = np.cumsum((0,) + IN_SIZES)
    cq, ckv, kpe, hq, hi, hg, ff, fb = (w_in[:, offs[i]:offs[i + 1]] for i in range(8))
    w_in_p = jnp.concatenate([cq, ckv, hq, hi, hg, ff, fb, kpe, kpe[:, _HALF_SWAP]], axis=1).astype(BF16)
    uq = w_uq.reshape(Q_LORA_RANK, HEADS, QK_HEAD_DIM)
    rope = uq[:, :, QK_NOPE_DIM:]
    w_uq_p = jnp.concatenate([uq[:, :, :QK_NOPE_DIM].reshape(Q_LORA_RANK, -1),
                              rope.reshape(Q_LORA_RANK, -1),
                              rope[:, :, _HALF_SWAP].reshape(Q_LORA_RANK, -1)], axis=1).astype(BF16)
    ukv = w_ukv.reshape(KV_LORA_RANK, HEADS, QK_NOPE_DIM + V_HEAD_DIM)
    w_ukv_p = jnp.concatenate([ukv[:, :, :QK_NOPE_DIM].reshape(KV_LORA_RANK, -1),
                               ukv[:, :, QK_NOPE_DIM:].reshape(KV_LORA_RANK, -1)], axis=1).astype(BF16)
    return w_in_p, w_uq_p, w_ukv_p


def _rope_tables(n):
    rows = n // GRID_W
    row = jnp.broadcast_to(jnp.arange(rows)[:, None], (rows, GRID_W)).reshape(n)
    col = jnp.broadcast_to(jnp.arange(GRID_W)[None, :], (rows, GRID_W)).reshape(n)
    axis_dim = QK_ROPE_DIM // 2
    inv = 1.0 / (ROPE_THETA ** (jnp.arange(0, axis_dim, 2, dtype=F32) / axis_dim))
    ang_r = row.astype(F32)[:, None] * inv
    ang_c = col.astype(F32)[:, None] * inv
    cos = jnp.concatenate([jnp.cos(ang_r)] * 2 + [jnp.cos(ang_c)] * 2, axis=-1)
    sin = jnp.concatenate([-jnp.sin(ang_r), jnp.sin(ang_r), -jnp.sin(ang_c), jnp.sin(ang_c)], axis=-1)
    return jnp.tile(cos, (1, HEADS)), jnp.tile(sin, (1, HEADS))


def kernel(x, c, ctx, c_ctx, w_mod, b_mod, g_norm_mix, g_norm_ffn, w_in, g_q_norm, w_uq, g_kv_norm,
           w_ukv, lb_fwd, lb_bwd, g_hgrn_norm, w_out, w_gate, w_up, w_down, g_final):
    B, N, D = x.shape
    L = ctx.shape[1]
    layer = 0

    cc = jnp.concatenate([c, c_ctx[None, :], jnp.zeros((7, D), F32)], axis=0)
    mod = _mod_call(cc, w_mod[layer], b_mod[layer][None, :])
    pad = jnp.zeros((B, 2, D), F32)
    mod_lat = jnp.concatenate([mod[:B].reshape(B, 6, D), pad], axis=1)
    mod_ctx = jnp.concatenate([jnp.broadcast_to(mod[B].reshape(1, 6, D), (B, 6, D)), pad], axis=1)

    w_in_p, w_uq_p, w_ukv_p = _prep_weights(w_in[layer], w_uq[layer], w_ukv[layer])
    cos4, sin4 = _rope_tables(N)
    row2 = lambda v: v.reshape(1, -1)
    proj_args = (row2(g_norm_mix[layer]), w_in_p, row2(g_q_norm[layer]), w_uq_p,
                 row2(g_kv_norm[layer]), w_ukv_p)

    q_l, k_l, v_l, hq_l, hi_l, hg_l, ff_l, fb_l = _in_proj_call(
        x, mod_lat, *proj_args, cos4, sin4, tm=512)
    _, k_c, v_c, _, hi_c, _, ff_c, fb_c = _in_proj_call(
        ctx, mod_ctx, *proj_args, jnp.ones((L, 256), F32), jnp.zeros((L, 256), F32), tm=L)

    o_mla = _attn_call(q_l, k_c, k_l, v_c, v_l, tq=512, tk=1024)
    o_hgrn = _hgrn_call(hi_c, ff_c, fb_c, hq_l, hi_l, hg_l, ff_l, fb_l,
                        lb_fwd, lb_bwd, row2(g_hgrn_norm[layer]))

    return _out_ffn_call(x, o_mla, o_hgrn, mod_lat, row2(g_norm_ffn[layer]),
                         w_out[layer].astype(BF16), w_gate[layer].astype(BF16),
                         w_up[layer].astype(BF16), w_down[layer].astype(BF16),
                         row2(g_final), tm=512)
```

```python
import functools

import numpy as np
import jax
import jax.numpy as jnp
from jax import lax
from jax.experimental import pallas as pl
from jax.experimental.pallas import tpu as pltpu

D_MODEL = 1024
GRID_W = 64
EPS = 1e-6
ROPE_THETA = 10000.0
V_HEAD_DIM = 128
QK_NOPE_DIM = 128
QK_ROPE_DIM = 64
Q_LORA_RANK = 256
KV_LORA_RANK = 256
HEADS = 4
QK_HEAD_DIM = QK_NOPE_DIM + QK_ROPE_DIM
HGRN_DIM = 128
HGRN_WIDTH = HEADS * HGRN_DIM
CHUNK = 64
IN_SIZES = (Q_LORA_RANK, KV_LORA_RANK, QK_ROPE_DIM,
            HGRN_WIDTH, HGRN_WIDTH, HGRN_WIDTH, HGRN_WIDTH, HGRN_WIDTH)
D_FF = 2816
FF_CHUNK = 256
HGRN_BLOCK = 256
VMEM_LIMIT_BYTES = 56 * 1024 * 1024
Q_SCALE = float(np.log2(np.e) / np.sqrt(QK_HEAD_DIM))

BF16 = jnp.bfloat16
F32 = jnp.float32


def _dot(a, b):
    return jnp.dot(a, b, preferred_element_type=F32)


def _dot_nt(a, b):
    return lax.dot_general(a, b, (((1,), (1,)), ((), ())), preferred_element_type=F32)


def _dot_tn(a, b):
    return lax.dot_general(a, b, (((0,), (0,)), ((), ())), preferred_element_type=F32)


def _silu(x):
    return x * jax.nn.sigmoid(x)


def _rms(x):
    return x * lax.rsqrt(jnp.mean(x * x, axis=-1, keepdims=True) + EPS)


def _mod_kernel(c_ref, w_ref, b_ref, o_ref):
    a = _silu(c_ref[...]).astype(BF16)
    o_ref[...] = _dot(a, w_ref[...].astype(BF16)) + b_ref[...]


def _mod_call(cc, w_mod, b_mod):
    rows = cc.shape[0]
    cols = w_mod.shape[1]
    tn = 1024
    return pl.pallas_call(
        _mod_kernel,
        grid=(cols // tn,),
        in_specs=[pl.BlockSpec((rows, D_MODEL), lambda j: (0, 0)),
                  pl.BlockSpec((D_MODEL, tn), lambda j: (0, j)),
                  pl.BlockSpec((1, tn), lambda j: (0, j))],
        out_specs=pl.BlockSpec((rows, tn), lambda j: (0, j)),
        out_shape=jax.ShapeDtypeStruct((rows, cols), F32),
        compiler_params=pltpu.CompilerParams(dimension_semantics=("arbitrary",)),
        name="mod",
    )(cc, w_mod, b_mod)


_C_CQ = 0
_C_CKV = 256
_C_HGRN = 512
_C_KPE = 512 + 5 * HGRN_WIDTH
_IN_COLS = _C_KPE + 2 * QK_ROPE_DIM


def _split3(g):
    hi = g.astype(BF16)
    r = g - hi.astype(F32)
    mid = r.astype(BF16)
    lo = (r - mid.astype(F32)).astype(BF16)
    return hi, mid, lo


def _chunk_masks():
    row = lax.broadcasted_iota(jnp.int32, (HGRN_BLOCK, HGRN_BLOCK), 0)
    col = lax.broadcasted_iota(jnp.int32, (HGRN_BLOCK, HGRN_BLOCK), 1)
    same_chunk = (row // CHUNK) == (col // CHUNK)
    return same_chunk & (col <= row), same_chunk & (col >= row)


def _chunk_triangles():
    return tuple(jnp.where(m, 1.0, 0.0).astype(BF16) for m in _chunk_masks())


def _in_proj_kernel(x_ref, mod_ref, gmix_ref, win_ref, gq_ref, wuq_ref, gkv_ref, wukv_ref,
                    cos_ref, sin_ref, lbf_ref, lbb_ref,
                    qt_ref, k_ref, vt_ref, hq_ref, hi_ref, hg_ref, kf_ref, bf_ref, kb_ref, bb_ref):
    x = x_ref[0]
    shift = mod_ref[0, 0:1, :]
    gain = gmix_ref[...] * (1.0 + mod_ref[0, 1:2, :])
    h = (_rms(x) * gain + shift).astype(BF16)

    cos4 = cos_ref[...]
    sin4 = sin_ref[...]

    lat = _dot(h, win_ref[:, _C_CQ:_C_HGRN])
    cq = (_rms(lat[:, :Q_LORA_RANK]) * gq_ref[...]).astype(BF16)
    ckv = (_rms(lat[:, Q_LORA_RANK:]) * gkv_ref[...]).astype(BF16)

    q = _dot(cq, wuq_ref[...])
    q_nope_t = (q[:, :512] * Q_SCALE).T
    q_rope_t = ((q[:, 512:768] * cos4 + q[:, 768:1024] * sin4) * Q_SCALE).T
    kv = _dot(ckv, wukv_ref[...])
    kp = _dot(h, win_ref[:, _C_KPE:_IN_COLS])
    k_rope = (kp[:, :QK_ROPE_DIM] * cos4[:, :QK_ROPE_DIM]
              + kp[:, QK_ROPE_DIM:] * sin4[:, :QK_ROPE_DIM]).astype(BF16)
    v_t = kv[:, 512:].T
    for hd in range(HEADS):
        qt_ref[0, hd, :QK_NOPE_DIM, :] = q_nope_t[hd * 128:(hd + 1) * 128].astype(BF16)
        qt_ref[0, hd, QK_NOPE_DIM:, :] = q_rope_t[hd * 64:(hd + 1) * 64].astype(BF16)
        k_ref[0, hd, :, :QK_NOPE_DIM] = kv[:, hd * 128:(hd + 1) * 128].astype(BF16)
        k_ref[0, hd, :, QK_NOPE_DIM:] = k_rope
        vt_ref[0, hd] = v_t[hd * 128:(hd + 1) * 128].astype(BF16)

    for j, o_ref in enumerate((hq_ref, hi_ref, hg_ref)):
        c0 = _C_HGRN + j * HGRN_WIDTH
        t = _dot(h, win_ref[:, c0:c0 + HGRN_WIDTH])
        for hd in range(HEADS):
            o_ref[0, hd] = t[:, hd * 128:(hd + 1) * 128].astype(o_ref.dtype)

    tri = _chunk_triangles()
    tm = x.shape[0]
    for d, (lb_ref, kk_ref, b_ref) in enumerate(((lbf_ref, kf_ref, bf_ref), (lbb_ref, kb_ref, bb_ref))):
        c0 = _C_HGRN + (3 + d) * HGRN_WIDTH
        t = lb_ref[...]
        e = jnp.exp(t - jnp.max(t, axis=0, keepdims=True))
        lb = e[0:1] / jnp.sum(e, axis=0, keepdims=True)
        f = lb + (1.0 - lb) * jax.nn.sigmoid(_dot(h, win_ref[:, c0:c0 + HGRN_WIDTH]))
        kk = (1.0 - f).astype(BF16)
        parts = _split3(jnp.log2(f))
        for r0 in range(0, tm, HGRN_BLOCK):
            b = sum(_dot(tri[d], p[r0:r0 + HGRN_BLOCK]) for p in parts)
            for hd in range(HEADS):
                b_ref[0, hd, r0:r0 + HGRN_BLOCK, :] = b[:, hd * 128:(hd + 1) * 128]
        for hd in range(HEADS):
            kk_ref[0, hd] = kk[:, hd * 128:(hd + 1) * 128]


def _in_proj_call(x, mod_rows, g_mix, w_in, g_q, w_uq, g_kv, w_ukv, cos4, sin4, lb_fwd, lb_bwd, tm):
    B, n, _ = x.shape
    const = lambda b, i: (0, 0)
    head_blk = lambda w: pl.BlockSpec((1, HEADS, tm, w), lambda b, i: (b, 0, i, 0))
    hshape = lambda w, dt: jax.ShapeDtypeStruct((B, HEADS, n, w), dt)
    head_blk_t = lambda w: pl.BlockSpec((1, HEADS, w, tm), lambda b, i: (b, 0, 0, i))
    hshape_t = lambda w: jax.ShapeDtypeStruct((B, HEADS, w, n), BF16)
    return pl.pallas_call(
        _in_proj_kernel,
        grid=(B, n // tm),
        in_specs=[pl.BlockSpec((1, tm, D_MODEL), lambda b, i: (b, i, 0)),
                  pl.BlockSpec((1, 8, D_MODEL), lambda b, i: (b, 0, 0)),
                  pl.BlockSpec((1, D_MODEL), const),
                  pl.BlockSpec((D_MODEL, _IN_COLS), const),
                  pl.BlockSpec((1, Q_LORA_RANK), const),
                  pl.BlockSpec((Q_LORA_RANK, 1024), const),
                  pl.BlockSpec((1, KV_LORA_RANK), const),
                  pl.BlockSpec((KV_LORA_RANK, 1024), const),
                  pl.BlockSpec((tm, 256), lambda b, i: (i, 0)),
                  pl.BlockSpec((tm, 256), lambda b, i: (i, 0)),
                  pl.BlockSpec(lb_fwd.shape, const),
                  pl.BlockSpec(lb_bwd.shape, const)],
        out_specs=[head_blk_t(QK_HEAD_DIM), head_blk(QK_HEAD_DIM), head_blk_t(V_HEAD_DIM),
                   head_blk(128), head_blk(128), head_blk(128),
                   head_blk(128), head_blk(128), head_blk(128), head_blk(128)],
        out_shape=[hshape_t(QK_HEAD_DIM), hshape(QK_HEAD_DIM, BF16), hshape_t(V_HEAD_DIM),
                   hshape(128, BF16), hshape(128, BF16), hshape(128, BF16),
                   hshape(128, BF16), hshape(128, F32), hshape(128, BF16), hshape(128, F32)],
        compiler_params=pltpu.CompilerParams(
            dimension_semantics=("parallel", "parallel"), vmem_limit_bytes=VMEM_LIMIT_BYTES),
        name="in_proj",
    )(x, mod_rows, g_mix, w_in, g_q, w_uq, g_kv, w_ukv, cos4, sin4, lb_fwd, lb_bwd)


def _attn_kernel(qt_ref, kc_ref, kl_ref, vtc_ref, vtl_ref, o_ref, s_ref, *, tk):
    qt = qt_ref[0, 0]
    n_ctx = kc_ref.shape[2]
    n_lat = kl_ref.shape[2]
    chunks = [(n_ctx, lambda: kc_ref[0, 0], lambda: vtc_ref[0, 0])]
    for j in range(n_lat // tk):
        chunks.append((tk, lambda j=j: kl_ref[0, 0, j * tk:(j + 1) * tk, :],
                       lambda j=j: vtl_ref[0, 0, :, j * tk:(j + 1) * tk]))

    def scores(j):
        rows, keys, _ = chunks[j]
        s_ref[j % 2, :rows, :] = _dot(keys(), qt)

    scores(0)
    m = l = acc = None
    for j, (rows, _, values_t) in enumerate(chunks):
        if j + 1 < len(chunks):
            scores(j + 1)
        s = s_ref[j % 2, :rows, :]
        m_chunk = jnp.max(s, axis=0, keepdims=True)
        m_new = m_chunk if m is None else jnp.maximum(m, m_chunk)
        p = jnp.exp2(s - m_new)
        l_chunk = jnp.sum(p, axis=0, keepdims=True)
        pv = _dot(values_t(), p.astype(BF16))
        if m is None:
            l, acc = l_chunk, pv
        else:
            alpha = jnp.exp2(m - m_new)
            l = alpha * l + l_chunk
            acc = alpha * acc + pv
        m = m_new
    o_ref[0] = (acc * (1.0 / l)).T.astype(o_ref.dtype)


def _attn_call(q_t, k_ctx, k_lat, vt_ctx, vt_lat, tq, tk):
    B, H, _, n = q_t.shape
    n_ctx = k_ctx.shape[2]
    return pl.pallas_call(
        functools.partial(_attn_kernel, tk=tk),
        grid=(B, H, n // tq),
        in_specs=[pl.BlockSpec((1, 1, QK_HEAD_DIM, tq), lambda b, h, i: (b, h, 0, i)),
                  pl.BlockSpec((1, 1, n_ctx, QK_HEAD_DIM), lambda b, h, i: (b, h, 0, 0)),
                  pl.BlockSpec((1, 1, n, QK_HEAD_DIM), lambda b, h, i: (b, h, 0, 0)),
                  pl.BlockSpec((1, 1, V_HEAD_DIM, n_ctx), lambda b, h, i: (b, h, 0, 0)),
                  pl.BlockSpec((1, 1, V_HEAD_DIM, n), lambda b, h, i: (b, h, 0, 0))],
        out_specs=pl.BlockSpec((1, tq, V_HEAD_DIM), lambda b, h, i: (b, i, h)),
        out_shape=jax.ShapeDtypeStruct((B, n, H * V_HEAD_DIM), BF16),
        scratch_shapes=[pltpu.VMEM((2, max(tk, n_ctx), tq), F32)],
        compiler_params=pltpu.CompilerParams(
            dimension_semantics=("parallel", "parallel", "arbitrary"),
            vmem_limit_bytes=VMEM_LIMIT_BYTES),
        name="attn",
    )(q_t, k_ctx, k_lat, vt_ctx, vt_lat)


_REF_ROWS = ((CHUNK // 2 - 1, CHUNK - 1), (CHUNK // 2, 0))
_CPB = HGRN_BLOCK // CHUNK
_FINISH_ROWS = 512


def _hgrn_kernel(vc_ref, kfc_ref, bfc_ref, kbc_ref, bbc_ref,
                 q_ref, v_ref, hg_ref, kf_ref, bf_ref, kb_ref, bb_ref, gon_ref,
                 o_ref, upd_ref, dec_ref, snap_ref, qd_ref, oin_ref):
    n_ctx = vc_ref.shape[2]
    n_lat = q_ref.shape[2]
    nc_ctx = n_ctx // CHUNK
    nc_lat = n_lat // CHUNK
    masks = _chunk_masks()

    def chunk_rows(x, r):
        return jnp.concatenate(
            [jnp.broadcast_to(x[c * CHUNK + r:c * CHUNK + r + 1], (CHUNK, x.shape[1])) for c in range(_CPB)],
            axis=0)

    def block_a(chunk0, rows, q, v, kks, bs):
        kds, decs, a, qds = [], [], None, []
        for d in range(2):
            r_ref, r_last = _REF_ROWS[d]
            kk, b = kks[d].astype(F32), bs[d]
            kds.append((kk * jnp.exp2(chunk_rows(b, r_last) - b)).astype(BF16))
            decs.append([jnp.exp2(b[c * CHUNK + r_last:c * CHUNK + r_last + 1]) for c in range(_CPB)])
            if q is not None:
                b_ref = chunk_rows(b, r_ref)
                qds.append((q * jnp.exp2(b)).astype(BF16))
                qa = (q * jnp.exp2(b - b_ref)).astype(BF16)
                ka = (kk * jnp.exp2(b_ref - b)).astype(BF16)
                a_d = jnp.where(masks[d], _dot_nt(qa, ka), 0.0)
                a = a_d if a is None else a + a_d
        kd = jnp.concatenate(kds, axis=1)
        for c in range(_CPB):
            rc = slice(c * CHUNK, (c + 1) * CHUNK)
            upd_ref[chunk0 + c] = _dot_tn(v[rc], kd[rc])
            dec_ref[chunk0 + c] = jnp.concatenate([decs[0][c], decs[1][c]], axis=1)
        if q is not None:
            qd_ref[rows, :] = jnp.concatenate(qds, axis=1)
            oin_ref[rows, :] = _dot(a.astype(BF16), v)

    for i in range(n_ctx // HGRN_BLOCK):
        r = slice(i * HGRN_BLOCK, (i + 1) * HGRN_BLOCK)
        block_a(i * _CPB, None, None, vc_ref[0, 0, r, :],
                (kfc_ref[0, 0, r, :], kbc_ref[0, 0, r, :]), (bfc_ref[0, 0, r, :], bbc_ref[0, 0, r, :]))

    def phase_a(i, _):
        r = pl.ds(pl.multiple_of(i * HGRN_BLOCK, HGRN_BLOCK), HGRN_BLOCK)
        block_a(nc_ctx + i * _CPB, r, q_ref[0, 0, r, :].astype(F32), v_ref[0, 0, r, :],
                (kf_ref[0, 0, r, :], kb_ref[0, 0, r, :]), (bf_ref[0, 0, r, :], bb_ref[0, 0, r, :]))
        return 0

    lax.fori_loop(0, n_lat // HGRN_BLOCK, phase_a, 0)

    def advance(st, cf, cb):
        dec = jnp.concatenate([dec_ref[cf][:, :HGRN_DIM], dec_ref[cb][:, HGRN_DIM:]], axis=1)
        upd = jnp.concatenate([upd_ref[cf][:, :HGRN_DIM], upd_ref[cb][:, HGRN_DIM:]], axis=1)
        return st * dec + upd

    st = jnp.zeros((HGRN_DIM, 2 * HGRN_DIM), F32)
    for i in range(nc_ctx):
        st = advance(st, i, nc_ctx - 1 - i)

    def phase_b(i, st):
        cf, cb = i, nc_lat - 1 - i
        sb = st.astype(BF16)
        snap_ref[cf, :, :HGRN_DIM] = sb[:, :HGRN_DIM]
        snap_ref[cb, :, HGRN_DIM:] = sb[:, HGRN_DIM:]
        return advance(st, nc_ctx + cf, nc_ctx + cb)

    lax.fori_loop(0, nc_lat, phase_b, st, unroll=4)

    cpf = _FINISH_ROWS // CHUNK
    def phase_c(i, _):
        r0 = pl.multiple_of(i * _FINISH_ROWS, _FINISH_ROWS)
        inter = [_dot_nt(qd_ref[pl.ds(r0 + c * CHUNK, CHUNK), :], snap_ref[i * cpf + c]) for c in range(cpf)]
        o = oin_ref[pl.ds(r0, _FINISH_ROWS), :] + jnp.concatenate(inter, axis=0)
        y = _rms(o) * gon_ref[...]
        gate = _silu(hg_ref[0, 0, pl.ds(r0, _FINISH_ROWS), :].astype(F32))
        o_ref[0, pl.ds(r0, _FINISH_ROWS), :] = (y * gate).astype(o_ref.dtype)
        return 0

    lax.fori_loop(0, n_lat // _FINISH_ROWS, phase_c, 0)


def _hgrn_call(ctx_ops, lat_ops, g_on):
    B, H, n, _ = lat_ops[0].shape
    n_ctx = ctx_ops[0].shape[2]
    n_chunks = (n + n_ctx) // CHUNK
    ctx_blk = pl.BlockSpec((1, 1, n_ctx, 128), lambda b, h: (b, h, 0, 0))
    lat_blk = pl.BlockSpec((1, 1, n, 128), lambda b, h: (b, h, 0, 0))
    return pl.pallas_call(
        _hgrn_kernel,
        grid=(B, H),
        in_specs=[ctx_blk] * len(ctx_ops) + [lat_blk] * len(lat_ops)
                 + [pl.BlockSpec((1, 128), lambda b, h: (0, 0))],
        out_specs=pl.BlockSpec((1, n, 128), lambda b, h: (b, 0, h)),
        out_shape=jax.ShapeDtypeStruct((B, n, H * 128), BF16),
        scratch_shapes=[pltpu.VMEM((n_chunks, HGRN_DIM, 2 * HGRN_DIM), F32),
                        pltpu.VMEM((n_chunks, 1, 2 * HGRN_DIM), F32),
                        pltpu.VMEM((n // CHUNK, HGRN_DIM, 2 * HGRN_DIM), BF16),
                        pltpu.VMEM((n, 2 * HGRN_DIM), BF16),
                        pltpu.VMEM((n, HGRN_DIM), F32)],
        compiler_params=pltpu.CompilerParams(
            dimension_semantics=("parallel", "parallel"), vmem_limit_bytes=VMEM_LIMIT_BYTES),
        name="hgrn",
    )(*ctx_ops, *lat_ops, g_on)


def _out_ffn_kernel(x_ref, om_ref, oh_ref, mod_ref, gffn_ref, wout_ref, wg_ref, wu_ref, wd_ref,
                    gfin_ref, o_ref):
    x = x_ref[0]
    mix = _dot(om_ref[0], wout_ref[:HEADS * V_HEAD_DIM, :]) + _dot(oh_ref[0], wout_ref[HEADS * V_HEAD_DIM:, :])
    x1 = x + mod_ref[0, 2:3, :] * mix
    gain = gffn_ref[...] * (1.0 + mod_ref[0, 4:5, :])
    h2 = (_rms(x1) * gain + mod_ref[0, 3:4, :]).astype(BF16)
    acc = jnp.zeros((x.shape[0], D_MODEL), F32)
    for j in range(D_FF // FF_CHUNK):
        c0 = j * FF_CHUNK
        g = _dot(h2, wg_ref[:, c0:c0 + FF_CHUNK])
        u = _dot(h2, wu_ref[:, c0:c0 + FF_CHUNK])
        a = (_silu(g) * u).astype(BF16)
        acc = acc + _dot(a, wd_ref[c0:c0 + FF_CHUNK, :])
    x2 = x1 + mod_ref[0, 5:6, :] * acc
    o_ref[0] = _rms(x2) * gfin_ref[...]


def _out_ffn_call(x, o_mla, o_hgrn, mod_rows, g_ffn, w_out, w_gate, w_up, w_down, g_final, tm):
    B, n, _ = x.shape
    const = lambda b, i: (0, 0)
    resident = lambda shape: pl.BlockSpec(shape, const, pipeline_mode=pl.Buffered(1))
    return pl.pallas_call(
        _out_ffn_kernel,
        grid=(B, n // tm),
        in_specs=[pl.BlockSpec((1, tm, D_MODEL), lambda b, i: (b, i, 0)),
                  pl.BlockSpec((1, tm, 512), lambda b, i: (b, i, 0)),
                  pl.BlockSpec((1, tm, 512), lambda b, i: (b, i, 0)),
                  pl.BlockSpec((1, 8, D_MODEL), lambda b, i: (b, 0, 0)),
                  pl.BlockSpec((1, D_MODEL), const),
                  resident((D_MODEL, D_MODEL)),
                  resident((D_MODEL, D_FF)),
                  resident((D_MODEL, D_FF)),
                  resident((D_FF, D_MODEL)),
                  pl.BlockSpec((1, D_MODEL), const)],
        out_specs=pl.BlockSpec((1, tm, D_MODEL), lambda b, i: (b, i, 0)),
        out_shape=jax.ShapeDtypeStruct((B, n, D_MODEL), F32),
        compiler_params=pltpu.CompilerParams(
            dimension_semantics=("parallel", "parallel"), vmem_limit_bytes=VMEM_LIMIT_BYTES),
        name="out_ffn",
    )(x, o_mla, o_hgrn, mod_rows, g_ffn, w_out, w_gate, w_up, w_down, g_final)


_HALF_SWAP = np.concatenate([np.arange(16, 32), np.arange(0, 16), np.arange(48, 64), np.arange(32, 48)])


def _prep_weights(w_in, w_uq, w_ukv):
    offs = np.cumsum((0,) + IN_SIZES)
    cq, ckv, kpe, hq, hi, hg, ff, fb = (w_in[:, offs[i]:offs[i + 1]] for i in range(8))
    w_in_p = jnp.concatenate([cq, ckv, hq, hi, hg, ff, fb, kpe, kpe[:, _HALF_SWAP]], axis=1).astype(BF16)
    uq = w_uq.reshape(Q_LORA_RANK, HEADS, QK_HEAD_DIM)
    rope = uq[:, :, QK_NOPE_DIM:]
    w_uq_p = jnp.concatenate([uq[:, :, :QK_NOPE_DIM].reshape(Q_LORA_RANK, -1),
                              rope.reshape(Q_LORA_RANK, -1),
                              rope[:, :, _HALF_SWAP].reshape(Q_LORA_RANK, -1)], axis=1).astype(BF16)
    ukv = w_ukv.reshape(KV_LORA_RANK, HEADS, QK_NOPE_DIM + V_HEAD_DIM)
    w_ukv_p = jnp.concatenate([ukv[:, :, :QK_NOPE_DIM].reshape(KV_LORA_RANK, -1),
                               ukv[:, :, QK_NOPE_DIM:].reshape(KV_LORA_RANK, -1)], axis=1).astype(BF16)
    return w_in_p, w_uq_p, w_ukv_p


def _rope_tables(n):
    rows = n // GRID_W
    row = jnp.broadcast_to(jnp.arange(rows)[:, None], (rows, GRID_W)).reshape(n)
    col = jnp.broadcast_to(jnp.arange(GRID_W)[None, :], (rows, GRID_W)).reshape(n)
    axis_dim = QK_ROPE_DIM // 2
    inv = 1.0 / (ROPE_THETA ** (jnp.arange(0, axis_dim, 2, dtype=F32) / axis_dim))
    ang_r = row.astype(F32)[:, None] * inv
    ang_c = col.astype(F32)[:, None] * inv
    cos = jnp.concatenate([jnp.cos(ang_r)] * 2 + [jnp.cos(ang_c)] * 2, axis=-1)
    sin = jnp.concatenate([-jnp.sin(ang_r), jnp.sin(ang_r), -jnp.sin(ang_c), jnp.sin(ang_c)], axis=-1)
    return jnp.tile(cos, (1, HEADS)), jnp.tile(sin, (1, HEADS))


def kernel(x, c, ctx, c_ctx, w_mod, b_mod, g_norm_mix, g_norm_ffn, w_in, g_q_norm, w_uq, g_kv_norm,
           w_ukv, lb_fwd, lb_bwd, g_hgrn_norm, w_out, w_gate, w_up, w_down, g_final):
    B, N, D = x.shape
    L = ctx.shape[1]
    layer = 0

    cc = jnp.concatenate([c, c_ctx[None, :], jnp.zeros((7, D), F32)], axis=0)
    mod = _mod_call(cc, w_mod[layer], b_mod[layer][None, :])
    pad = jnp.zeros((B, 2, D), F32)
    mod_lat = jnp.concatenate([mod[:B].reshape(B, 6, D), pad], axis=1)
    mod_ctx = jnp.concatenate([jnp.broadcast_to(mod[B].reshape(1, 6, D), (B, 6, D)), pad], axis=1)

    w_in_p, w_uq_p, w_ukv_p = _prep_weights(w_in[layer], w_uq[layer], w_ukv[layer])
    cos4, sin4 = _rope_tables(N)
    row2 = lambda v: v.reshape(1, -1)
    proj_args = (row2(g_norm_mix[layer]), w_in_p, row2(g_q_norm[layer]), w_uq_p,
                 row2(g_kv_norm[layer]), w_ukv_p)

    q_l, k_l, v_l, hq_l, hi_l, hg_l, *decay_l = _in_proj_call(
        x, mod_lat, *proj_args, cos4, sin4, lb_fwd, lb_bwd, tm=512)
    _, k_c, v_c, _, hi_c, _, *decay_c = _in_proj_call(
        ctx, mod_ctx, *proj_args, jnp.ones((L, 256), F32), jnp.zeros((L, 256), F32), lb_fwd, lb_bwd, tm=L)

    o_mla = _attn_call(q_l, k_c, k_l, v_c, v_l, tq=512, tk=1024)
    o_hgrn = _hgrn_call((hi_c, *decay_c), (hq_l, hi_l, hg_l, *decay_l), row2(g_hgrn_norm[layer]))

    return _out_ffn_call(x, o_mla, o_hgrn, mod_lat, row2(g_norm_ffn[layer]),
                         w_out[layer].astype(BF16), w_gate[layer].astype(BF16),
                         w_up[layer].astype(BF16), w_down[layer].astype(BF16),
                         row2(g_final), tm=512)
```

```python
import functools

import numpy as np
import jax
import jax.numpy as jnp
from jax import lax
from jax.experimental import pallas as pl
from jax.experimental.pallas import tpu as pltpu

D_MODEL = 1024
GRID_W = 64
EPS = 1e-6
ROPE_THETA = 10000.0
V_HEAD_DIM = 128
QK_NOPE_DIM = 128
QK_ROPE_DIM = 64
Q_LORA_RANK = 256
KV_LORA_RANK = 256
HEADS = 4
QK_HEAD_DIM = QK_NOPE_DIM + QK_ROPE_DIM
HGRN_DIM = 128
HGRN_WIDTH = HEADS * HGRN_DIM
CHUNK = 64
IN_SIZES = (Q_LORA_RANK, KV_LORA_RANK, QK_ROPE_DIM,
            HGRN_WIDTH, HGRN_WIDTH, HGRN_WIDTH, HGRN_WIDTH, HGRN_WIDTH)
D_FF = 2816
FF_CHUNK = 256
HGRN_BLOCK = 256
VMEM_LIMIT_BYTES = 56 * 1024 * 1024
Q_SCALE = float(np.log2(np.e) / np.sqrt(QK_HEAD_DIM))

BF16 = jnp.bfloat16
F32 = jnp.float32


def _dot(a, b):
    return jnp.dot(a, b, preferred_element_type=F32)


def _dot_nt(a, b):
    return lax.dot_general(a, b, (((1,), (1,)), ((), ())), preferred_element_type=F32)


def _dot_tn(a, b):
    return lax.dot_general(a, b, (((0,), (0,)), ((), ())), preferred_element_type=F32)


def _silu(x):
    return x * jax.nn.sigmoid(x)


def _rms(x):
    return x * lax.rsqrt(jnp.mean(x * x, axis=-1, keepdims=True) + EPS)


def _mod_kernel(c_ref, w_ref, b_ref, o_ref):
    a = _silu(c_ref[...]).astype(BF16)
    o_ref[...] = _dot(a, w_ref[...].astype(BF16)) + b_ref[...]


def _mod_call(cc, w_mod, b_mod):
    rows = cc.shape[0]
    cols = w_mod.shape[1]
    tn = 1024
    return pl.pallas_call(
        _mod_kernel,
        grid=(cols // tn,),
        in_specs=[pl.BlockSpec((rows, D_MODEL), lambda j: (0, 0)),
                  pl.BlockSpec((D_MODEL, tn), lambda j: (0, j)),
                  pl.BlockSpec((1, tn), lambda j: (0, j))],
        out_specs=pl.BlockSpec((rows, tn), lambda j: (0, j)),
        out_shape=jax.ShapeDtypeStruct((rows, cols), F32),
        compiler_params=pltpu.CompilerParams(dimension_semantics=("arbitrary",)),
        name="mod",
    )(cc, w_mod, b_mod)


_C_CQ = 0
_C_CKV = 256
_C_HGRN = 512
_C_KPE = 512 + 5 * HGRN_WIDTH
_IN_COLS = _C_KPE + 2 * QK_ROPE_DIM


def _chunk_cumsum(g, reverse):
    rows, w = g.shape
    x = g.reshape(rows // 8, 8, w)
    sub = lax.broadcasted_iota(jnp.int32, (1, 8, w), 1)
    for s in (1, 2, 4):
        if reverse:
            x = x + jnp.where(sub < 8 - s, pltpu.roll(x, 8 - s, axis=1), 0.0)
        else:
            x = x + jnp.where(sub >= s, pltpu.roll(x, s, axis=1), 0.0)
    groups = CHUNK // 8
    x = x.reshape(rows // CHUNK, groups, 8, w)
    edge = 0 if reverse else 7
    outs = [None] * groups
    carry = None
    for j in (range(groups - 1, -1, -1) if reverse else range(groups)):
        blk = x[:, j] if carry is None else x[:, j] + carry
        outs[j] = blk
        carry = jnp.broadcast_to(blk[:, edge:edge + 1, :], blk.shape)
    return jnp.stack(outs, axis=1).reshape(rows, w)


def _chunk_masks():
    row = lax.broadcasted_iota(jnp.int32, (HGRN_BLOCK, HGRN_BLOCK), 0)
    col = lax.broadcasted_iota(jnp.int32, (HGRN_BLOCK, HGRN_BLOCK), 1)
    same_chunk = (row // CHUNK) == (col // CHUNK)
    return same_chunk & (col <= row), same_chunk & (col >= row)


def _in_proj_kernel(x_ref, mod_ref, gmix_ref, win_ref, gq_ref, wuq_ref, gkv_ref, wukv_ref,
                    cos_ref, sin_ref, lbf_ref, lbb_ref,
                    qt_ref, k_ref, vt_ref, hq_ref, hi_ref, hg_ref, kf_ref, bf_ref, kb_ref, bb_ref):
    x = x_ref[0]
    shift = mod_ref[0, 0:1, :]
    gain = gmix_ref[...] * (1.0 + mod_ref[0, 1:2, :])
    h = (_rms(x) * gain + shift).astype(BF16)

    cos4 = cos_ref[...]
    sin4 = sin_ref[...]

    lat = _dot(h, win_ref[:, _C_CQ:_C_HGRN])
    cq = (_rms(lat[:, :Q_LORA_RANK]) * gq_ref[...]).astype(BF16)
    ckv = (_rms(lat[:, Q_LORA_RANK:]) * gkv_ref[...]).astype(BF16)

    q = _dot(cq, wuq_ref[...])
    q_nope_t = (q[:, :512] * Q_SCALE).T
    q_rope_t = ((q[:, 512:768] * cos4 + q[:, 768:1024] * sin4) * Q_SCALE).T
    kv = _dot(ckv, wukv_ref[...])
    kp = _dot(h, win_ref[:, _C_KPE:_IN_COLS])
    k_rope = (kp[:, :QK_ROPE_DIM] * cos4[:, :QK_ROPE_DIM]
              + kp[:, QK_ROPE_DIM:] * sin4[:, :QK_ROPE_DIM]).astype(BF16)
    v_t = kv[:, 512:].T
    for hd in range(HEADS):
        qt_ref[0, hd, :QK_NOPE_DIM, :] = q_nope_t[hd * 128:(hd + 1) * 128].astype(BF16)
        qt_ref[0, hd, QK_NOPE_DIM:, :] = q_rope_t[hd * 64:(hd + 1) * 64].astype(BF16)
        k_ref[0, hd, :, :QK_NOPE_DIM] = kv[:, hd * 128:(hd + 1) * 128].astype(BF16)
        k_ref[0, hd, :, QK_NOPE_DIM:] = k_rope
        vt_ref[0, hd] = v_t[hd * 128:(hd + 1) * 128].astype(BF16)

    for j, o_ref in enumerate((hq_ref, hi_ref, hg_ref)):
        c0 = _C_HGRN + j * HGRN_WIDTH
        t = _dot(h, win_ref[:, c0:c0 + HGRN_WIDTH])
        for hd in range(HEADS):
            o_ref[0, hd] = t[:, hd * 128:(hd + 1) * 128].astype(o_ref.dtype)

    for d, (lb_ref, kk_ref, b_ref) in enumerate(((lbf_ref, kf_ref, bf_ref), (lbb_ref, kb_ref, bb_ref))):
        c0 = _C_HGRN + (3 + d) * HGRN_WIDTH
        t = lb_ref[...]
        e = jnp.exp(t - jnp.max(t, axis=0, keepdims=True))
        lb = e[0:1] / jnp.sum(e, axis=0, keepdims=True)
        f = lb + (1.0 - lb) * jax.nn.sigmoid(_dot(h, win_ref[:, c0:c0 + HGRN_WIDTH]))
        kk = (1.0 - f).astype(BF16)
        b = _chunk_cumsum(jnp.log2(f), reverse=(d == 1))
        for hd in range(HEADS):
            b_ref[0, hd] = b[:, hd * 128:(hd + 1) * 128]
            kk_ref[0, hd] = kk[:, hd * 128:(hd + 1) * 128]


def _in_proj_call(x, mod_rows, g_mix, w_in, g_q, w_uq, g_kv, w_ukv, cos4, sin4, lb_fwd, lb_bwd, tm):
    B, n, _ = x.shape
    const = lambda b, i: (0, 0)
    head_blk = lambda w: pl.BlockSpec((1, HEADS, tm, w), lambda b, i: (b, 0, i, 0))
    hshape = lambda w, dt: jax.ShapeDtypeStruct((B, HEADS, n, w), dt)
    head_blk_t = lambda w: pl.BlockSpec((1, HEADS, w, tm), lambda b, i: (b, 0, 0, i))
    hshape_t = lambda w: jax.ShapeDtypeStruct((B, HEADS, w, n), BF16)
    return pl.pallas_call(
        _in_proj_kernel,
        grid=(B, n // tm),
        in_specs=[pl.BlockSpec((1, tm, D_MODEL), lambda b, i: (b, i, 0)),
                  pl.BlockSpec((1, 8, D_MODEL), lambda b, i: (b, 0, 0)),
                  pl.BlockSpec((1, D_MODEL), const),
                  pl.BlockSpec((D_MODEL, _IN_COLS), const),
                  pl.BlockSpec((1, Q_LORA_RANK), const),
                  pl.BlockSpec((Q_LORA_RANK, 1024), const),
                  pl.BlockSpec((1, KV_LORA_RANK), const),
                  pl.BlockSpec((KV_LORA_RANK, 1024), const),
                  pl.BlockSpec((tm, 256), lambda b, i: (i, 0)),
                  pl.BlockSpec((tm, 256), lambda b, i: (i, 0)),
                  pl.BlockSpec(lb_fwd.shape, const),
                  pl.BlockSpec(lb_bwd.shape, const)],
        out_specs=[head_blk_t(QK_HEAD_DIM), head_blk(QK_HEAD_DIM), head_blk_t(V_HEAD_DIM),
                   head_blk(128), head_blk(128), head_blk(128),
                   head_blk(128), head_blk(128), head_blk(128), head_blk(128)],
        out_shape=[hshape_t(QK_HEAD_DIM), hshape(QK_HEAD_DIM, BF16), hshape_t(V_HEAD_DIM),
                   hshape(128, BF16), hshape(128, BF16), hshape(128, BF16),
                   hshape(128, BF16), hshape(128, F32), hshape(128, BF16), hshape(128, F32)],
        compiler_params=pltpu.CompilerParams(
            dimension_semantics=("parallel", "parallel"), vmem_limit_bytes=VMEM_LIMIT_BYTES),
        name="in_proj",
    )(x, mod_rows, g_mix, w_in, g_q, w_uq, g_kv, w_ukv, cos4, sin4, lb_fwd, lb_bwd)


def _attn_kernel(qt_ref, kc_ref, kl_ref, vtc_ref, vtl_ref, o_ref, s_ref, *, tk):
    qt = qt_ref[0, 0]
    n_ctx = kc_ref.shape[2]
    n_lat = kl_ref.shape[2]
    chunks = [(n_ctx, lambda: kc_ref[0, 0], lambda: vtc_ref[0, 0])]
    for j in range(n_lat // tk):
        chunks.append((tk, lambda j=j: kl_ref[0, 0, j * tk:(j + 1) * tk, :],
                       lambda j=j: vtl_ref[0, 0, :, j * tk:(j + 1) * tk]))

    def scores(j):
        rows, keys, _ = chunks[j]
        s_ref[j % 2, :rows, :] = _dot(keys(), qt)

    scores(0)
    m = l = acc = None
    for j, (rows, _, values_t) in enumerate(chunks):
        if j + 1 < len(chunks):
            scores(j + 1)
        s = s_ref[j % 2, :rows, :]
        m_chunk = jnp.max(s, axis=0, keepdims=True)
        m_new = m_chunk if m is None else jnp.maximum(m, m_chunk)
        p = jnp.exp2(s - m_new)
        l_chunk = jnp.sum(p, axis=0, keepdims=True)
        pv = _dot(values_t(), p.astype(BF16))
        if m is None:
            l, acc = l_chunk, pv
        else:
            alpha = jnp.exp2(m - m_new)
            l = alpha * l + l_chunk
            acc = alpha * acc + pv
        m = m_new
    o_ref[0] = (acc * (1.0 / l)).T.astype(o_ref.dtype)


def _attn_call(q_t, k_ctx, k_lat, vt_ctx, vt_lat, tq, tk):
    B, H, _, n = q_t.shape
    n_ctx = k_ctx.shape[2]
    return pl.pallas_call(
        functools.partial(_attn_kernel, tk=tk),
        grid=(B, H, n // tq),
        in_specs=[pl.BlockSpec((1, 1, QK_HEAD_DIM, tq), lambda b, h, i: (b, h, 0, i)),
                  pl.BlockSpec((1, 1, n_ctx, QK_HEAD_DIM), lambda b, h, i: (b, h, 0, 0)),
                  pl.BlockSpec((1, 1, n, QK_HEAD_DIM), lambda b, h, i: (b, h, 0, 0)),
                  pl.BlockSpec((1, 1, V_HEAD_DIM, n_ctx), lambda b, h, i: (b, h, 0, 0)),
                  pl.BlockSpec((1, 1, V_HEAD_DIM, n), lambda b, h, i: (b, h, 0, 0))],
        out_specs=pl.BlockSpec((1, tq, V_HEAD_DIM), lambda b, h, i: (b, i, h)),
        out_shape=jax.ShapeDtypeStruct((B, n, H * V_HEAD_DIM), BF16),
        scratch_shapes=[pltpu.VMEM((2, max(tk, n_ctx), tq), F32)],
        compiler_params=pltpu.CompilerParams(
            dimension_semantics=("parallel", "parallel", "arbitrary"),
            vmem_limit_bytes=VMEM_LIMIT_BYTES),
        name="attn",
    )(q_t, k_ctx, k_lat, vt_ctx, vt_lat)


_REF_ROWS = ((CHUNK // 2 - 1, CHUNK - 1), (CHUNK // 2, 0))
_CPB = HGRN_BLOCK // CHUNK
_FINISH_ROWS = 512


def _hgrn_kernel(vc_ref, kfc_ref, bfc_ref, kbc_ref, bbc_ref,
                 q_ref, v_ref, hg_ref, kf_ref, bf_ref, kb_ref, bb_ref, gon_ref,
                 o_ref, upd_ref, dec_ref, snap_ref, qd_ref, oin_ref):
    n_ctx = vc_ref.shape[2]
    n_lat = q_ref.shape[2]
    nc_ctx = n_ctx // CHUNK
    nc_lat = n_lat // CHUNK
    masks = _chunk_masks()

    def chunk_rows(x, r):
        return jnp.concatenate(
            [jnp.broadcast_to(x[c * CHUNK + r:c * CHUNK + r + 1], (CHUNK, x.shape[1])) for c in range(_CPB)],
            axis=0)

    def block_a(chunk0, rows, q, v, kks, bs):
        kds, decs, a, qds = [], [], None, []
        for d in range(2):
            r_ref, r_last = _REF_ROWS[d]
            kk, b = kks[d].astype(F32), bs[d]
            kds.append((kk * jnp.exp2(chunk_rows(b, r_last) - b)).astype(BF16))
            decs.append([jnp.exp2(b[c * CHUNK + r_last:c * CHUNK + r_last + 1]) for c in range(_CPB)])
            if q is not None:
                b_ref = chunk_rows(b, r_ref)
                qds.append((q * jnp.exp2(b)).astype(BF16))
                qa = (q * jnp.exp2(b - b_ref)).astype(BF16)
                ka = (kk * jnp.exp2(b_ref - b)).astype(BF16)
                a_d = jnp.where(masks[d], _dot_nt(qa, ka), 0.0)
                a = a_d if a is None else a + a_d
        kd = jnp.concatenate(kds, axis=1)
        for c in range(_CPB):
            rc = slice(c * CHUNK, (c + 1) * CHUNK)
            upd_ref[chunk0 + c] = _dot_tn(v[rc], kd[rc])
            dec_ref[chunk0 + c] = jnp.concatenate([decs[0][c], decs[1][c]], axis=1)
        if q is not None:
            qd_ref[rows, :] = jnp.concatenate(qds, axis=1)
            oin_ref[rows, :] = _dot(a.astype(BF16), v)

    for i in range(n_ctx // HGRN_BLOCK):
        r = slice(i * HGRN_BLOCK, (i + 1) * HGRN_BLOCK)
        block_a(i * _CPB, None, None, vc_ref[0, 0, r, :],
                (kfc_ref[0, 0, r, :], kbc_ref[0, 0, r, :]), (bfc_ref[0, 0, r, :], bbc_ref[0, 0, r, :]))

    def phase_a(i, _):
        r = pl.ds(pl.multiple_of(i * HGRN_BLOCK, HGRN_BLOCK), HGRN_BLOCK)
        block_a(nc_ctx + i * _CPB, r, q_ref[0, 0, r, :].astype(F32), v_ref[0, 0, r, :],
                (kf_ref[0, 0, r, :], kb_ref[0, 0, r, :]), (bf_ref[0, 0, r, :], bb_ref[0, 0, r, :]))
        return 0

    lax.fori_loop(0, n_lat // HGRN_BLOCK, phase_a, 0, unroll=4)

    def advance(st, cf, cb):
        dec = jnp.concatenate([dec_ref[cf][:, :HGRN_DIM], dec_ref[cb][:, HGRN_DIM:]], axis=1)
        upd = jnp.concatenate([upd_ref[cf][:, :HGRN_DIM], upd_ref[cb][:, HGRN_DIM:]], axis=1)
        return st * dec + upd

    st = jnp.zeros((HGRN_DIM, 2 * HGRN_DIM), F32)
    for i in range(nc_ctx):
        st = advance(st, i, nc_ctx - 1 - i)

    def phase_b(i, st):
        cf, cb = i, nc_lat - 1 - i
        sb = st.astype(BF16)
        snap_ref[cf, :, :HGRN_DIM] = sb[:, :HGRN_DIM]
        snap_ref[cb, :, HGRN_DIM:] = sb[:, HGRN_DIM:]
        return advance(st, nc_ctx + cf, nc_ctx + cb)

    lax.fori_loop(0, nc_lat, phase_b, st, unroll=4)

    cpf = _FINISH_ROWS // CHUNK
    def phase_c(i, _):
        r0 = pl.multiple_of(i * _FINISH_ROWS, _FINISH_ROWS)
        inter = [_dot_nt(qd_ref[pl.ds(r0 + c * CHUNK, CHUNK), :], snap_ref[i * cpf + c]) for c in range(cpf)]
        o = oin_ref[pl.ds(r0, _FINISH_ROWS), :] + jnp.concatenate(inter, axis=0)
        y = _rms(o) * gon_ref[...]
        gate = _silu(hg_ref[0, 0, pl.ds(r0, _FINISH_ROWS), :].astype(F32))
        o_ref[0, pl.ds(r0, _FINISH_ROWS), :] = (y * gate).astype(o_ref.dtype)
        return 0

    lax.fori_loop(0, n_lat // _FINISH_ROWS, phase_c, 0)


def _hgrn_call(ctx_ops, lat_ops, g_on):
    B, H, n, _ = lat_ops[0].shape
    n_ctx = ctx_ops[0].shape[2]
    n_chunks = (n + n_ctx) // CHUNK
    ctx_blk = pl.BlockSpec((1, 1, n_ctx, 128), lambda b, h: (b, h, 0, 0))
    lat_blk = pl.BlockSpec((1, 1, n, 128), lambda b, h: (b, h, 0, 0))
    return pl.pallas_call(
        _hgrn_kernel,
        grid=(B, H),
        in_specs=[ctx_blk] * len(ctx_ops) + [lat_blk] * len(lat_ops)
                 + [pl.BlockSpec((1, 128), lambda b, h: (0, 0))],
        out_specs=pl.BlockSpec((1, n, 128), lambda b, h: (b, 0, h)),
        out_shape=jax.ShapeDtypeStruct((B, n, H * 128), BF16),
        scratch_shapes=[pltpu.VMEM((n_chunks, HGRN_DIM, 2 * HGRN_DIM), F32),
                        pltpu.VMEM((n_chunks, 1, 2 * HGRN_DIM), F32),
                        pltpu.VMEM((n // CHUNK, HGRN_DIM, 2 * HGRN_DIM), BF16),
                        pltpu.VMEM((n, 2 * HGRN_DIM), BF16),
                        pltpu.VMEM((n, HGRN_DIM), F32)],
        compiler_params=pltpu.CompilerParams(
            dimension_semantics=("parallel", "parallel"), vmem_limit_bytes=VMEM_LIMIT_BYTES),
        name="hgrn",
    )(*ctx_ops, *lat_ops, g_on)


def _out_ffn_kernel(x_ref, om_ref, oh_ref, mod_ref, gffn_ref, wout_ref, wg_ref, wu_ref, wd_ref,
                    gfin_ref, o_ref):
    x = x_ref[0]
    mix = _dot(om_ref[0], wout_ref[:HEADS * V_HEAD_DIM, :]) + _dot(oh_ref[0], wout_ref[HEADS * V_HEAD_DIM:, :])
    x1 = x + mod_ref[0, 2:3, :] * mix
    gain = gffn_ref[...] * (1.0 + mod_ref[0, 4:5, :])
    h2 = (_rms(x1) * gain + mod_ref[0, 3:4, :]).astype(BF16)
    acc = jnp.zeros((x.shape[0], D_MODEL), F32)
    for j in range(D_FF // FF_CHUNK):
        c0 = j * FF_CHUNK
        g = _dot(h2, wg_ref[:, c0:c0 + FF_CHUNK])
        u = _dot(h2, wu_ref[:, c0:c0 + FF_CHUNK])
        a = (_silu(g) * u).astype(BF16)
        acc = acc + _dot(a, wd_ref[c0:c0 + FF_CHUNK, :])
    x2 = x1 + mod_ref[0, 5:6, :] * acc
    o_ref[0] = _rms(x2) * gfin_ref[...]


def _out_ffn_call(x, o_mla, o_hgrn, mod_rows, g_ffn, w_out, w_gate, w_up, w_down, g_final, tm):
    B, n, _ = x.shape
    const = lambda b, i: (0, 0)
    resident = lambda shape: pl.BlockSpec(shape, const, pipeline_mode=pl.Buffered(1))
    return pl.pallas_call(
        _out_ffn_kernel,
        grid=(B, n // tm),
        in_specs=[pl.BlockSpec((1, tm, D_MODEL), lambda b, i: (b, i, 0)),
                  pl.BlockSpec((1, tm, 512), lambda b, i: (b, i, 0)),
                  pl.BlockSpec((1, tm, 512), lambda b, i: (b, i, 0)),
                  pl.BlockSpec((1, 8, D_MODEL), lambda b, i: (b, 0, 0)),
                  pl.BlockSpec((1, D_MODEL), const),
                  resident((D_MODEL, D_MODEL)),
                  resident((D_MODEL, D_FF)),
                  resident((D_MODEL, D_FF)),
                  resident((D_FF, D_MODEL)),
                  pl.BlockSpec((1, D_MODEL), const)],
        out_specs=pl.BlockSpec((1, tm, D_MODEL), lambda b, i: (b, i, 0)),
        out_shape=jax.ShapeDtypeStruct((B, n, D_MODEL), F32),
        compiler_params=pltpu.CompilerParams(
            dimension_semantics=("parallel", "parallel"), vmem_limit_bytes=VMEM_LIMIT_BYTES),
        name="out_ffn",
    )(x, o_mla, o_hgrn, mod_rows, g_ffn, w_out, w_gate, w_up, w_down, g_final)


_HALF_SWAP = np.concatenate([np.arange(16, 32), np.arange(0, 16), np.arange(48, 64), np.arange(32, 48)])


def _prep_weights(w_in, w_uq, w_ukv):
    offs = np.cumsum((0,) + IN_SIZES)
    cq, ckv, kpe, hq, hi, hg, ff, fb = (w_in[:, offs[i]:offs[i + 1]] for i in range(8))
    w_in_p = jnp.concatenate([cq, ckv, hq, hi, hg, ff, fb, kpe, kpe[:, _HALF_SWAP]], axis=1).astype(BF16)
    uq = w_uq.reshape(Q_LORA_RANK, HEADS, QK_HEAD_DIM)
    rope = uq[:, :, QK_NOPE_DIM:]
    w_uq_p = jnp.concatenate([uq[:, :, :QK_NOPE_DIM].reshape(Q_LORA_RANK, -1),
                              rope.reshape(Q_LORA_RANK, -1),
                              rope[:, :, _HALF_SWAP].reshape(Q_LORA_RANK, -1)], axis=1).astype(BF16)
    ukv = w_ukv.reshape(KV_LORA_RANK, HEADS, QK_NOPE_DIM + V_HEAD_DIM)
    w_ukv_p = jnp.concatenate([ukv[:, :, :QK_NOPE_DIM].reshape(KV_LORA_RANK, -1),
                               ukv[:, :, QK_NOPE_DIM:].reshape(KV_LORA_RANK, -1)], axis=1).astype(BF16)
    return w_in_p, w_uq_p, w_ukv_p


def _rope_tables(n):
    rows = n // GRID_W
    row = jnp.broadcast_to(jnp.arange(rows)[:, None], (rows, GRID_W)).reshape(n)
    col = jnp.broadcast_to(jnp.arange(GRID_W)[None, :], (rows, GRID_W)).reshape(n)
    axis_dim = QK_ROPE_DIM // 2
    inv = 1.0 / (ROPE_THETA ** (jnp.arange(0, axis_dim, 2, dtype=F32) / axis_dim))
    ang_r = row.astype(F32)[:, None] * inv
    ang_c = col.astype(F32)[:, None] * inv
    cos = jnp.concatenate([jnp.cos(ang_r)] * 2 + [jnp.cos(ang_c)] * 2, axis=-1)
    sin = jnp.concatenate([-jnp.sin(ang_r), jnp.sin(ang_r), -jnp.sin(ang_c), jnp.sin(ang_c)], axis=-1)
    return jnp.tile(cos, (1, HEADS)), jnp.tile(sin, (1, HEADS))


def kernel(x, c, ctx, c_ctx, w_mod, b_mod, g_norm_mix, g_norm_ffn, w_in, g_q_norm, w_uq, g_kv_norm,
           w_ukv, lb_fwd, lb_bwd, g_hgrn_norm, w_out, w_gate, w_up, w_down, g_final):
    B, N, D = x.shape
    L = ctx.shape[1]
    layer = 0

    cc = jnp.concatenate([c, c_ctx[None, :], jnp.zeros((7, D), F32)], axis=0)
    mod = _mod_call(cc, w_mod[layer], b_mod[layer][None, :])
    pad = jnp.zeros((B, 2, D), F32)
    mod_lat = jnp.concatenate([mod[:B].reshape(B, 6, D), pad], axis=1)
    mod_ctx = jnp.concatenate([jnp.broadcast_to(mod[B].reshape(1, 6, D), (B, 6, D)), pad], axis=1)

    w_in_p, w_uq_p, w_ukv_p = _prep_weights(w_in[layer], w_uq[layer], w_ukv[layer])
    cos4, sin4 = _rope_tables(N)
    row2 = lambda v: v.reshape(1, -1)
    proj_args = (row2(g_norm_mix[layer]), w_in_p, row2(g_q_norm[layer]), w_uq_p,
                 row2(g_kv_norm[layer]), w_ukv_p)

    q_l, k_l, v_l, hq_l, hi_l, hg_l, *decay_l = _in_proj_call(
        x, mod_lat, *proj_args, cos4, sin4, lb_fwd, lb_bwd, tm=512)
    _, k_c, v_c, _, hi_c, _, *decay_c = _in_proj_call(
        ctx, mod_ctx, *proj_args, jnp.ones((L, 256), F32), jnp.zeros((L, 256), F32), lb_fwd, lb_bwd, tm=L)

    o_mla = _attn_call(q_l, k_c, k_l, v_c, v_l, tq=512, tk=1024)
    o_hgrn = _hgrn_call((hi_c, *decay_c), (hq_l, hi_l, hg_l, *decay_l), row2(g_hgrn_norm[layer]))

    return _out_ffn_call(x, o_mla, o_hgrn, mod_lat, row2(g_norm_ffn[layer]),
                         w_out[layer].astype(BF16), w_gate[layer].astype(BF16),
                         w_up[layer].astype(BF16), w_down[layer].astype(BF16),
                         row2(g_final), tm=512)
```

```python
import functools

import numpy as np
import jax
import jax.numpy as jnp
from jax import lax
from jax.experimental import pallas as pl
from jax.experimental.pallas import tpu as pltpu

D_MODEL = 1024
GRID_W = 64
EPS = 1e-6
ROPE_THETA = 10000.0
V_HEAD_DIM = 128
QK_NOPE_DIM = 128
QK_ROPE_DIM = 64
Q_LORA_RANK = 256
KV_LORA_RANK = 256
HEADS = 4
QK_HEAD_DIM = QK_NOPE_DIM + QK_ROPE_DIM
HGRN_DIM = 128
HGRN_WIDTH = HEADS * HGRN_DIM
CHUNK = 64
IN_SIZES = (Q_LORA_RANK, KV_LORA_RANK, QK_ROPE_DIM,
            HGRN_WIDTH, HGRN_WIDTH, HGRN_WIDTH, HGRN_WIDTH, HGRN_WIDTH)
D_FF = 2816
FF_CHUNK = 256
HGRN_BLOCK = 256
VMEM_LIMIT_BYTES = 56 * 1024 * 1024
Q_SCALE = float(np.log2(np.e) / np.sqrt(QK_HEAD_DIM))

BF16 = jnp.bfloat16
F32 = jnp.float32


def _dot(a, b):
    return jnp.dot(a, b, preferred_element_type=F32)


def _dot_nt(a, b):
    return lax.dot_general(a, b, (((1,), (1,)), ((), ())), preferred_element_type=F32)


def _dot_tn(a, b):
    return lax.dot_general(a, b, (((0,), (0,)), ((), ())), preferred_element_type=F32)


def _silu(x):
    return x * jax.nn.sigmoid(x)


def _rms(x):
    return x * lax.rsqrt(jnp.mean(x * x, axis=-1, keepdims=True) + EPS)


def _mod_kernel(c_ref, w_ref, b_ref, o_ref):
    a = _silu(c_ref[...]).astype(BF16)
    o_ref[...] = _dot(a, w_ref[...].astype(BF16)) + b_ref[...]


def _mod_call(cc, w_mod, b_mod):
    rows = cc.shape[0]
    cols = w_mod.shape[1]
    tn = 1024
    return pl.pallas_call(
        _mod_kernel,
        grid=(cols // tn,),
        in_specs=[pl.BlockSpec((rows, D_MODEL), lambda j: (0, 0)),
                  pl.BlockSpec((D_MODEL, tn), lambda j: (0, j)),
                  pl.BlockSpec((1, tn), lambda j: (0, j))],
        out_specs=pl.BlockSpec((rows, tn), lambda j: (0, j)),
        out_shape=jax.ShapeDtypeStruct((rows, cols), F32),
        compiler_params=pltpu.CompilerParams(dimension_semantics=("arbitrary",)),
        name="mod",
    )(cc, w_mod, b_mod)


_C_CQ = 0
_C_CKV = 256
_C_HGRN = 512
_C_KPE = 512 + 5 * HGRN_WIDTH
_IN_COLS = _C_KPE + 2 * QK_ROPE_DIM


def _chunk_cumsum(g, reverse):
    rows, w = g.shape
    x = g.reshape(rows // 8, 8, w)
    sub = lax.broadcasted_iota(jnp.int32, (1, 8, w), 1)
    for s in (1, 2, 4):
        if reverse:
            x = x + jnp.where(sub < 8 - s, pltpu.roll(x, 8 - s, axis=1), 0.0)
        else:
            x = x + jnp.where(sub >= s, pltpu.roll(x, s, axis=1), 0.0)
    groups = CHUNK // 8
    x = x.reshape(rows // CHUNK, groups, 8, w)
    edge = 0 if reverse else 7
    outs = [None] * groups
    carry = None
    for j in (range(groups - 1, -1, -1) if reverse else range(groups)):
        blk = x[:, j] if carry is None else x[:, j] + carry
        outs[j] = blk
        carry = jnp.broadcast_to(blk[:, edge:edge + 1, :], blk.shape)
    return jnp.stack(outs, axis=1).reshape(rows, w)


def _chunk_masks():
    row = lax.broadcasted_iota(jnp.int32, (HGRN_BLOCK, HGRN_BLOCK), 0)
    col = lax.broadcasted_iota(jnp.int32, (HGRN_BLOCK, HGRN_BLOCK), 1)
    same_chunk = (row // CHUNK) == (col // CHUNK)
    return same_chunk & (col <= row), same_chunk & (col >= row)


def _in_proj_kernel(x_ref, mod_ref, gmix_ref, win_ref, gq_ref, wuq_ref, gkv_ref, wukv_ref,
                    cos_ref, sin_ref, lbf_ref, lbb_ref,
                    qt_ref, k_ref, vt_ref, hq_ref, hi_ref, hg_ref, kf_ref, bf_ref, kb_ref, bb_ref):
    x = x_ref[0]
    shift = mod_ref[0, 0:1, :]
    gain = gmix_ref[...] * (1.0 + mod_ref[0, 1:2, :])
    h = (_rms(x) * gain + shift).astype(BF16)

    cos4 = cos_ref[...]
    sin4 = sin_ref[...]

    lat = _dot(h, win_ref[:, _C_CQ:_C_HGRN])
    cq = (_rms(lat[:, :Q_LORA_RANK]) * gq_ref[...]).astype(BF16)
    ckv = (_rms(lat[:, Q_LORA_RANK:]) * gkv_ref[...]).astype(BF16)

    q = _dot(cq, wuq_ref[...])
    q_nope_t = (q[:, :512] * Q_SCALE).T
    q_rope_t = ((q[:, 512:768] * cos4 + q[:, 768:1024] * sin4) * Q_SCALE).T
    kv = _dot(ckv, wukv_ref[...])
    kp = _dot(h, win_ref[:, _C_KPE:_IN_COLS])
    k_rope = (kp[:, :QK_ROPE_DIM] * cos4[:, :QK_ROPE_DIM]
              + kp[:, QK_ROPE_DIM:] * sin4[:, :QK_ROPE_DIM]).astype(BF16)
    v_t = kv[:, 512:].T
    for hd in range(HEADS):
        qt_ref[0, hd, :QK_NOPE_DIM, :] = q_nope_t[hd * 128:(hd + 1) * 128].astype(BF16)
        qt_ref[0, hd, QK_NOPE_DIM:, :] = q_rope_t[hd * 64:(hd + 1) * 64].astype(BF16)
        k_ref[0, hd, :, :QK_NOPE_DIM] = kv[:, hd * 128:(hd + 1) * 128].astype(BF16)
        k_ref[0, hd, :, QK_NOPE_DIM:] = k_rope
        vt_ref[0, hd] = v_t[hd * 128:(hd + 1) * 128].astype(BF16)

    for j, o_ref in enumerate((hq_ref, hi_ref, hg_ref)):
        c0 = _C_HGRN + j * HGRN_WIDTH
        t = _dot(h, win_ref[:, c0:c0 + HGRN_WIDTH])
        for hd in range(HEADS):
            o_ref[0, hd] = t[:, hd * 128:(hd + 1) * 128].astype(o_ref.dtype)

    for d, (lb_ref, kk_ref, b_ref) in enumerate(((lbf_ref, kf_ref, bf_ref), (lbb_ref, kb_ref, bb_ref))):
        c0 = _C_HGRN + (3 + d) * HGRN_WIDTH
        t = lb_ref[...]
        e = jnp.exp(t - jnp.max(t, axis=0, keepdims=True))
        lb = e[0:1] / jnp.sum(e, axis=0, keepdims=True)
        f = lb + (1.0 - lb) * jax.nn.sigmoid(_dot(h, win_ref[:, c0:c0 + HGRN_WIDTH]))
        kk = (1.0 - f).astype(BF16)
        b = _chunk_cumsum(jnp.log2(f), reverse=(d == 1))
        for hd in range(HEADS):
            b_ref[0, hd] = b[:, hd * 128:(hd + 1) * 128]
            kk_ref[0, hd] = kk[:, hd * 128:(hd + 1) * 128]


def _in_proj_call(x, mod_rows, g_mix, w_in, g_q, w_uq, g_kv, w_ukv, cos4, sin4, lb_fwd, lb_bwd, tm):
    B, n, _ = x.shape
    const = lambda b, i: (0, 0)
    head_blk = lambda w: pl.BlockSpec((1, HEADS, tm, w), lambda b, i: (b, 0, i, 0))
    hshape = lambda w, dt: jax.ShapeDtypeStruct((B, HEADS, n, w), dt)
    head_blk_t = lambda w: pl.BlockSpec((1, HEADS, w, tm), lambda b, i: (b, 0, 0, i))
    hshape_t = lambda w: jax.ShapeDtypeStruct((B, HEADS, w, n), BF16)
    return pl.pallas_call(
        _in_proj_kernel,
        grid=(B, n // tm),
        in_specs=[pl.BlockSpec((1, tm, D_MODEL), lambda b, i: (b, i, 0)),
                  pl.BlockSpec((1, 8, D_MODEL), lambda b, i: (b, 0, 0)),
                  pl.BlockSpec((1, D_MODEL), const),
                  pl.BlockSpec((D_MODEL, _IN_COLS), const),
                  pl.BlockSpec((1, Q_LORA_RANK), const),
                  pl.BlockSpec((Q_LORA_RANK, 1024), const),
                  pl.BlockSpec((1, KV_LORA_RANK), const),
                  pl.BlockSpec((KV_LORA_RANK, 1024), const),
                  pl.BlockSpec((tm, 256), lambda b, i: (i, 0)),
                  pl.BlockSpec((tm, 256), lambda b, i: (i, 0)),
                  pl.BlockSpec(lb_fwd.shape, const),
                  pl.BlockSpec(lb_bwd.shape, const)],
        out_specs=[head_blk_t(QK_HEAD_DIM), head_blk(QK_HEAD_DIM), head_blk_t(V_HEAD_DIM),
                   head_blk(128), head_blk(128), head_blk(128),
                   head_blk(128), head_blk(128), head_blk(128), head_blk(128)],
        out_shape=[hshape_t(QK_HEAD_DIM), hshape(QK_HEAD_DIM, BF16), hshape_t(V_HEAD_DIM),
                   hshape(128, BF16), hshape(128, BF16), hshape(128, BF16),
                   hshape(128, BF16), hshape(128, F32), hshape(128, BF16), hshape(128, F32)],
        compiler_params=pltpu.CompilerParams(
            dimension_semantics=("parallel", "parallel"), vmem_limit_bytes=VMEM_LIMIT_BYTES),
        name="in_proj",
    )(x, mod_rows, g_mix, w_in, g_q, w_uq, g_kv, w_ukv, cos4, sin4, lb_fwd, lb_bwd)


def _attn_kernel(qt_ref, kc_ref, kl_ref, vtc_ref, vtl_ref, o_ref, s_ref, *, tq, tk):
    n_ctx = kc_ref.shape[2]
    n_lat = kl_ref.shape[2]
    chunks = [(n_ctx, lambda: kc_ref[0, 0], lambda: vtc_ref[0, 0])]
    for j in range(n_lat // tk):
        chunks.append((tk, lambda j=j: kl_ref[0, 0, j * tk:(j + 1) * tk, :],
                       lambda j=j: vtl_ref[0, 0, :, j * tk:(j + 1) * tk]))

    def q_tile(i, _):
        q0 = pl.multiple_of(i * tq, tq)
        qt = qt_ref[0, 0, :, pl.ds(q0, tq)]

        def scores(j):
            rows, keys, _ = chunks[j]
            s_ref[j % 2, :rows, :] = _dot(keys(), qt)

        scores(0)
        m = l = acc = None
        for j, (rows, _, values_t) in enumerate(chunks):
            if j + 1 < len(chunks):
                scores(j + 1)
            s = s_ref[j % 2, :rows, :]
            m_chunk = jnp.max(s, axis=0, keepdims=True)
            m_new = m_chunk if m is None else jnp.maximum(m, m_chunk)
            p = jnp.exp2(s - m_new)
            l_chunk = jnp.sum(p, axis=0, keepdims=True)
            pv = _dot(values_t(), p.astype(BF16))
            if m is None:
                l, acc = l_chunk, pv
            else:
                alpha = jnp.exp2(m - m_new)
                l = alpha * l + l_chunk
                acc = alpha * acc + pv
            m = m_new
        o_ref[0, pl.ds(q0, tq), :] = (acc * (1.0 / l)).T.astype(o_ref.dtype)
        return 0

    lax.fori_loop(0, n_lat // tq, q_tile, 0)


def _attn_call(q_t, k_ctx, k_lat, vt_ctx, vt_lat, tq, tk):
    B, H, _, n = q_t.shape
    n_ctx = k_ctx.shape[2]
    return pl.pallas_call(
        functools.partial(_attn_kernel, tq=tq, tk=tk),
        grid=(B, H),
        in_specs=[pl.BlockSpec((1, 1, QK_HEAD_DIM, n), lambda b, h: (b, h, 0, 0)),
                  pl.BlockSpec((1, 1, n_ctx, QK_HEAD_DIM), lambda b, h: (b, h, 0, 0)),
                  pl.BlockSpec((1, 1, n, QK_HEAD_DIM), lambda b, h: (b, h, 0, 0)),
                  pl.BlockSpec((1, 1, V_HEAD_DIM, n_ctx), lambda b, h: (b, h, 0, 0)),
                  pl.BlockSpec((1, 1, V_HEAD_DIM, n), lambda b, h: (b, h, 0, 0))],
        out_specs=pl.BlockSpec((1, n, V_HEAD_DIM), lambda b, h: (b, 0, h)),
        out_shape=jax.ShapeDtypeStruct((B, n, H * V_HEAD_DIM), BF16),
        scratch_shapes=[pltpu.VMEM((2, max(tk, n_ctx), tq), F32)],
        compiler_params=pltpu.CompilerParams(
            dimension_semantics=("parallel", "parallel"), vmem_limit_bytes=VMEM_LIMIT_BYTES),
        name="attn",
    )(q_t, k_ctx, k_lat, vt_ctx, vt_lat)


_REF_ROWS = ((CHUNK // 2 - 1, CHUNK - 1), (CHUNK // 2, 0))
_CPB = HGRN_BLOCK // CHUNK
_FINISH_ROWS = 512


def _hgrn_kernel(vc_ref, kfc_ref, bfc_ref, kbc_ref, bbc_ref,
                 q_ref, v_ref, hg_ref, kf_ref, bf_ref, kb_ref, bb_ref, gon_ref,
                 o_ref, upd_ref, dec_ref, snap_ref, qd_ref, oin_ref):
    n_ctx = vc_ref.shape[2]
    n_lat = q_ref.shape[2]
    nc_ctx = n_ctx // CHUNK
    nc_lat = n_lat // CHUNK
    masks = _chunk_masks()

    def chunk_rows(x, r):
        return jnp.concatenate(
            [jnp.broadcast_to(x[c * CHUNK + r:c * CHUNK + r + 1], (CHUNK, x.shape[1])) for c in range(_CPB)],
            axis=0)

    def block_a(chunk0, rows, q, v, kks, bs):
        kds, decs, a, qds = [], [], None, []
        for d in range(2):
            r_ref, r_last = _REF_ROWS[d]
            kk, b = kks[d].astype(F32), bs[d]
            kds.append((kk * jnp.exp2(chunk_rows(b, r_last) - b)).astype(BF16))
            decs.append([jnp.exp2(b[c * CHUNK + r_last:c * CHUNK + r_last + 1]) for c in range(_CPB)])
            if q is not None:
                b_ref = chunk_rows(b, r_ref)
                qds.append((q * jnp.exp2(b)).astype(BF16))
                qa = (q * jnp.exp2(b - b_ref)).astype(BF16)
                ka = (kk * jnp.exp2(b_ref - b)).astype(BF16)
                a_d = jnp.where(masks[d], _dot_nt(qa, ka), 0.0)
                a = a_d if a is None else a + a_d
        kd = jnp.concatenate(kds, axis=1)
        for c in range(_CPB):
            rc = slice(c * CHUNK, (c + 1) * CHUNK)
            upd_ref[chunk0 + c] = _dot_tn(v[rc], kd[rc])
            dec_ref[chunk0 + c] = jnp.concatenate([decs[0][c], decs[1][c]], axis=1)
        if q is not None:
            qd_ref[rows, :] = jnp.concatenate(qds, axis=1)
            oin_ref[rows, :] = _dot(a.astype(BF16), v)

    for i in range(n_ctx // HGRN_BLOCK):
        r = slice(i * HGRN_BLOCK, (i + 1) * HGRN_BLOCK)
        block_a(i * _CPB, None, None, vc_ref[0, 0, r, :],
                (kfc_ref[0, 0, r, :], kbc_ref[0, 0, r, :]), (bfc_ref[0, 0, r, :], bbc_ref[0, 0, r, :]))

    def phase_a(i, _):
        r = pl.ds(pl.multiple_of(i * HGRN_BLOCK, HGRN_BLOCK), HGRN_BLOCK)
        block_a(nc_ctx + i * _CPB, r, q_ref[0, 0, r, :].astype(F32), v_ref[0, 0, r, :],
                (kf_ref[0, 0, r, :], kb_ref[0, 0, r, :]), (bf_ref[0, 0, r, :], bb_ref[0, 0, r, :]))
        return 0

    lax.fori_loop(0, n_lat // HGRN_BLOCK, phase_a, 0, unroll=4)

    def advance(st, cf, cb):
        dec = jnp.concatenate([dec_ref[cf][:, :HGRN_DIM], dec_ref[cb][:, HGRN_DIM:]], axis=1)
        upd = jnp.concatenate([upd_ref[cf][:, :HGRN_DIM], upd_ref[cb][:, HGRN_DIM:]], axis=1)
        return st * dec + upd

    st = jnp.zeros((HGRN_DIM, 2 * HGRN_DIM), F32)
    for i in range(nc_ctx):
        st = advance(st, i, nc_ctx - 1 - i)

    def phase_b(i, st):
        cf, cb = i, nc_lat - 1 - i
        sb = st.astype(BF16)
        snap_ref[cf, :, :HGRN_DIM] = sb[:, :HGRN_DIM]
        snap_ref[cb, :, HGRN_DIM:] = sb[:, HGRN_DIM:]
        return advance(st, nc_ctx + cf, nc_ctx + cb)

    lax.fori_loop(0, nc_lat, phase_b, st, unroll=4)

    cpf = _FINISH_ROWS // CHUNK
    def phase_c(i, _):
        r0 = pl.multiple_of(i * _FINISH_ROWS, _FINISH_ROWS)
        inter = [_dot_nt(qd_ref[pl.ds(r0 + c * CHUNK, CHUNK), :], snap_ref[i * cpf + c]) for c in range(cpf)]
        o = oin_ref[pl.ds(r0, _FINISH_ROWS), :] + jnp.concatenate(inter, axis=0)
        y = _rms(o) * gon_ref[...]
        gate = _silu(hg_ref[0, 0, pl.ds(r0, _FINISH_ROWS), :].astype(F32))
        o_ref[0, pl.ds(r0, _FINISH_ROWS), :] = (y * gate).astype(o_ref.dtype)
        return 0

    lax.fori_loop(0, n_lat // _FINISH_ROWS, phase_c, 0)


def _hgrn_call(ctx_ops, lat_ops, g_on):
    B, H, n, _ = lat_ops[0].shape
    n_ctx = ctx_ops[0].shape[2]
    n_chunks = (n + n_ctx) // CHUNK
    ctx_blk = pl.BlockSpec((1, 1, n_ctx, 128), lambda b, h: (b, h, 0, 0))
    lat_blk = pl.BlockSpec((1, 1, n, 128), lambda b, h: (b, h, 0, 0))
    return pl.pallas_call(
        _hgrn_kernel,
        grid=(B, H),
        in_specs=[ctx_blk] * len(ctx_ops) + [lat_blk] * len(lat_ops)
                 + [pl.BlockSpec((1, 128), lambda b, h: (0, 0))],
        out_specs=pl.BlockSpec((1, n, 128), lambda b, h: (b, 0, h)),
        out_shape=jax.ShapeDtypeStruct((B, n, H * 128), BF16),
        scratch_shapes=[pltpu.VMEM((n_chunks, HGRN_DIM, 2 * HGRN_DIM), F32),
                        pltpu.VMEM((n_chunks, 1, 2 * HGRN_DIM), F32),
                        pltpu.VMEM((n // CHUNK, HGRN_DIM, 2 * HGRN_DIM), BF16),
                        pltpu.VMEM((n, 2 * HGRN_DIM), BF16),
                        pltpu.VMEM((n, HGRN_DIM), F32)],
        compiler_params=pltpu.CompilerParams(
            dimension_semantics=("parallel", "parallel"), vmem_limit_bytes=VMEM_LIMIT_BYTES),
        name="hgrn",
    )(*ctx_ops, *lat_ops, g_on)


def _out_ffn_kernel(x_ref, om_ref, oh_ref, mod_ref, gffn_ref, wout_ref, wg_ref, wu_ref, wd_ref,
                    gfin_ref, o_ref):
    x = x_ref[0]
    mix = _dot(om_ref[0], wout_ref[:HEADS * V_HEAD_DIM, :]) + _dot(oh_ref[0], wout_ref[HEADS * V_HEAD_DIM:, :])
    x1 = x + mod_ref[0, 2:3, :] * mix
    gain = gffn_ref[...] * (1.0 + mod_ref[0, 4:5, :])
    h2 = (_rms(x1) * gain + mod_ref[0, 3:4, :]).astype(BF16)
    acc = jnp.zeros((x.shape[0], D_MODEL), F32)
    for j in range(D_FF // FF_CHUNK):
        c0 = j * FF_CHUNK
        g = _dot(h2, wg_ref[:, c0:c0 + FF_CHUNK])
        u = _dot(h2, wu_ref[:, c0:c0 + FF_CHUNK])
        a = (_silu(g) * u).astype(BF16)
        acc = acc + _dot(a, wd_ref[c0:c0 + FF_CHUNK, :])
    x2 = x1 + mod_ref[0, 5:6, :] * acc
    o_ref[0] = _rms(x2) * gfin_ref[...]


def _out_ffn_call(x, o_mla, o_hgrn, mod_rows, g_ffn, w_out, w_gate, w_up, w_down, g_final, tm):
    B, n, _ = x.shape
    const = lambda b, i: (0, 0)
    resident = lambda shape: pl.BlockSpec(shape, const, pipeline_mode=pl.Buffered(1))
    return pl.pallas_call(
        _out_ffn_kernel,
        grid=(B, n // tm),
        in_specs=[pl.BlockSpec((1, tm, D_MODEL), lambda b, i: (b, i, 0)),
                  pl.BlockSpec((1, tm, 512), lambda b, i: (b, i, 0)),
                  pl.BlockSpec((1, tm, 512), lambda b, i: (b, i, 0)),
                  pl.BlockSpec((1, 8, D_MODEL), lambda b, i: (b, 0, 0)),
                  pl.BlockSpec((1, D_MODEL), const),
                  resident((D_MODEL, D_MODEL)),
                  resident((D_MODEL, D_FF)),
                  resident((D_MODEL, D_FF)),
                  resident((D_FF, D_MODEL)),
                  pl.BlockSpec((1, D_MODEL), const)],
        out_specs=pl.BlockSpec((1, tm, D_MODEL), lambda b, i: (b, i, 0)),
        out_shape=jax.ShapeDtypeStruct((B, n, D_MODEL), F32),
        compiler_params=pltpu.CompilerParams(
            dimension_semantics=("parallel", "parallel"), vmem_limit_bytes=VMEM_LIMIT_BYTES),
        name="out_ffn",
    )(x, o_mla, o_hgrn, mod_rows, g_ffn, w_out, w_gate, w_up, w_down, g_final)


_HALF_SWAP = np.concatenate([np.arange(16, 32), np.arange(0, 16), np.arange(48, 64), np.arange(32, 48)])


def _prep_weights(w_in, w_uq, w_ukv):
    offs = np.cumsum((0,) + IN_SIZES)
    cq, ckv, kpe, hq, hi, hg, ff, fb = (w_in[:, offs[i]:offs[i + 1]] for i in range(8))
    w_in_p = jnp.concatenate([cq, ckv, hq, hi, hg, ff, fb, kpe, kpe[:, _HALF_SWAP]], axis=1).astype(BF16)
    uq = w_uq.reshape(Q_LORA_RANK, HEADS, QK_HEAD_DIM)
    rope = uq[:, :, QK_NOPE_DIM:]
    w_uq_p = jnp.concatenate([uq[:, :, :QK_NOPE_DIM].reshape(Q_LORA_RANK, -1),
                              rope.reshape(Q_LORA_RANK, -1),
                              rope[:, :, _HALF_SWAP].reshape(Q_LORA_RANK, -1)], axis=1).astype(BF16)
    ukv = w_ukv.reshape(KV_LORA_RANK, HEADS, QK_NOPE_DIM + V_HEAD_DIM)
    w_ukv_p = jnp.concatenate([ukv[:, :, :QK_NOPE_DIM].reshape(KV_LORA_RANK, -1),
                               ukv[:, :, QK_NOPE_DIM:].reshape(KV_LORA_RANK, -1)], axis=1).astype(BF16)
    return w_in_p, w_uq_p, w_ukv_p


def _rope_tables(n):
    rows = n // GRID_W
    row = jnp.broadcast_to(jnp.arange(rows)[:, None], (rows, GRID_W)).reshape(n)
    col = jnp.broadcast_to(jnp.arange(GRID_W)[None, :], (rows, GRID_W)).reshape(n)
    axis_dim = QK_ROPE_DIM // 2
    inv = 1.0 / (ROPE_THETA ** (jnp.arange(0, axis_dim, 2, dtype=F32) / axis_dim))
    ang_r = row.astype(F32)[:, None] * inv
    ang_c = col.astype(F32)[:, None] * inv
    cos = jnp.concatenate([jnp.cos(ang_r)] * 2 + [jnp.cos(ang_c)] * 2, axis=-1)
    sin = jnp.concatenate([-jnp.sin(ang_r), jnp.sin(ang_r), -jnp.sin(ang_c), jnp.sin(ang_c)], axis=-1)
    return jnp.tile(cos, (1, HEADS)), jnp.tile(sin, (1, HEADS))


def kernel(x, c, ctx, c_ctx, w_mod, b_mod, g_norm_mix, g_norm_ffn, w_in, g_q_norm, w_uq, g_kv_norm,
           w_ukv, lb_fwd, lb_bwd, g_hgrn_norm, w_out, w_gate, w_up, w_down, g_final):
    B, N, D = x.shape
    L = ctx.shape[1]
    layer = 0

    cc = jnp.concatenate([c, c_ctx[None, :], jnp.zeros((7, D), F32)], axis=0)
    mod = _mod_call(cc, w_mod[layer], b_mod[layer][None, :])
    pad = jnp.zeros((B, 2, D), F32)
    mod_lat = jnp.concatenate([mod[:B].reshape(B, 6, D), pad], axis=1)
    mod_ctx = jnp.concatenate([jnp.broadcast_to(mod[B].reshape(1, 6, D), (B, 6, D)), pad], axis=1)

    w_in_p, w_uq_p, w_ukv_p = _prep_weights(w_in[layer], w_uq[layer], w_ukv[layer])
    cos4, sin4 = _rope_tables(N)
    row2 = lambda v: v.reshape(1, -1)
    proj_args = (row2(g_norm_mix[layer]), w_in_p, row2(g_q_norm[layer]), w_uq_p,
                 row2(g_kv_norm[layer]), w_ukv_p)

    q_l, k_l, v_l, hq_l, hi_l, hg_l, *decay_l = _in_proj_call(
        x, mod_lat, *proj_args, cos4, sin4, lb_fwd, lb_bwd, tm=512)
    _, k_c, v_c, _, hi_c, _, *decay_c = _in_proj_call(
        ctx, mod_ctx, *proj_args, jnp.ones((L, 256), F32), jnp.zeros((L, 256), F32), lb_fwd, lb_bwd, tm=L)

    o_mla = _attn_call(q_l, k_c, k_l, v_c, v_l, tq=512, tk=1024)
    o_hgrn = _hgrn_call((hi_c, *decay_c), (hq_l, hi_l, hg_l, *decay_l), row2(g_hgrn_norm[layer]))

    return _out_ffn_call(x, o_mla, o_hgrn, mod_lat, row2(g_norm_ffn[layer]),
                         w_out[layer].astype(BF16), w_gate[layer].astype(BF16),
                         w_up[layer].astype(BF16), w_down[layer].astype(BF16),
                         row2(g_final), tm=512)
```

```python
import functools

import numpy as np
import jax
import jax.numpy as jnp
from jax import lax
from jax.experimental import pallas as pl
from jax.experimental.pallas import tpu as pltpu

D_MODEL = 1024
GRID_W = 64
EPS = 1e-6
ROPE_THETA = 10000.0
V_HEAD_DIM = 128
QK_NOPE_DIM = 128
QK_ROPE_DIM = 64
Q_LORA_RANK = 256
KV_LORA_RANK = 256
HEADS = 4
QK_HEAD_DIM = QK_NOPE_DIM + QK_ROPE_DIM
HGRN_DIM = 128
HGRN_WIDTH = HEADS * HGRN_DIM
CHUNK = 64
IN_SIZES = (Q_LORA_RANK, KV_LORA_RANK, QK_ROPE_DIM,
            HGRN_WIDTH, HGRN_WIDTH, HGRN_WIDTH, HGRN_WIDTH, HGRN_WIDTH)
D_FF = 2816
FF_CHUNK = 256
HGRN_BLOCK = 256
ATTN_SUB = 256
VMEM_LIMIT_BYTES = 56 * 1024 * 1024
Q_SCALE = float(np.log2(np.e) / np.sqrt(QK_HEAD_DIM))

BF16 = jnp.bfloat16
F32 = jnp.float32


def _dot(a, b):
    return jnp.dot(a, b, preferred_element_type=F32)


def _dot_nt(a, b):
    return lax.dot_general(a, b, (((1,), (1,)), ((), ())), preferred_element_type=F32)


def _dot_tn(a, b):
    return lax.dot_general(a, b, (((0,), (0,)), ((), ())), preferred_element_type=F32)


def _silu(x):
    return x * jax.nn.sigmoid(x)


def _rms(x):
    return x * lax.rsqrt(jnp.mean(x * x, axis=-1, keepdims=True) + EPS)


def _mod_kernel(c_ref, w_ref, b_ref, o_ref):
    a = _silu(c_ref[...]).astype(BF16)
    o_ref[...] = _dot(a, w_ref[...].astype(BF16)) + b_ref[...]


def _mod_call(cc, w_mod, b_mod):
    rows = cc.shape[0]
    cols = w_mod.shape[1]
    tn = 1024
    return pl.pallas_call(
        _mod_kernel,
        grid=(cols // tn,),
        in_specs=[pl.BlockSpec((rows, D_MODEL), lambda j: (0, 0)),
                  pl.BlockSpec((D_MODEL, tn), lambda j: (0, j)),
                  pl.BlockSpec((1, tn), lambda j: (0, j))],
        out_specs=pl.BlockSpec((rows, tn), lambda j: (0, j)),
        out_shape=jax.ShapeDtypeStruct((rows, cols), F32),
        compiler_params=pltpu.CompilerParams(dimension_semantics=("arbitrary",)),
        name="mod",
    )(cc, w_mod, b_mod)


_C_CQ = 0
_C_CKV = 256
_C_HGRN = 512
_C_KPE = 512 + 5 * HGRN_WIDTH
_IN_COLS = _C_KPE + 2 * QK_ROPE_DIM


def _chunk_cumsum(g, reverse):
    rows, w = g.shape
    x = g.reshape(rows // 8, 8, w)
    sub = lax.broadcasted_iota(jnp.int32, (1, 8, w), 1)
    for s in (1, 2, 4):
        if reverse:
            x = x + jnp.where(sub < 8 - s, pltpu.roll(x, 8 - s, axis=1), 0.0)
        else:
            x = x + jnp.where(sub >= s, pltpu.roll(x, s, axis=1), 0.0)
    groups = CHUNK // 8
    x = x.reshape(rows // CHUNK, groups, 8, w)
    edge = 0 if reverse else 7
    outs = [None] * groups
    carry = None
    for j in (range(groups - 1, -1, -1) if reverse else range(groups)):
        blk = x[:, j] if carry is None else x[:, j] + carry
        outs[j] = blk
        carry = jnp.broadcast_to(blk[:, edge:edge + 1, :], blk.shape)
    return jnp.stack(outs, axis=1).reshape(rows, w)


def _chunk_masks():
    row = lax.broadcasted_iota(jnp.int32, (HGRN_BLOCK, HGRN_BLOCK), 0)
    col = lax.broadcasted_iota(jnp.int32, (HGRN_BLOCK, HGRN_BLOCK), 1)
    same_chunk = (row // CHUNK) == (col // CHUNK)
    return same_chunk & (col <= row), same_chunk & (col >= row)


def _in_proj_kernel(x_ref, mod_ref, gmix_ref, win_ref, gq_ref, wuq_ref, gkv_ref, wukv_ref,
                    cos_ref, sin_ref, lbf_ref, lbb_ref,
                    qt_ref, k_ref, vt_ref, hq_ref, hi_ref, hg_ref, kf_ref, bf_ref, kb_ref, bb_ref):
    x = x_ref[0]
    shift = mod_ref[0, 0:1, :]
    gain = gmix_ref[...] * (1.0 + mod_ref[0, 1:2, :])
    h = (_rms(x) * gain + shift).astype(BF16)

    cos4 = cos_ref[...]
    sin4 = sin_ref[...]

    lat = _dot(h, win_ref[:, _C_CQ:_C_HGRN])
    cq = (_rms(lat[:, :Q_LORA_RANK]) * gq_ref[...]).astype(BF16)
    ckv = (_rms(lat[:, Q_LORA_RANK:]) * gkv_ref[...]).astype(BF16)

    q = _dot(cq, wuq_ref[...])
    q_nope_t = (q[:, :512] * Q_SCALE).T
    q_rope_t = ((q[:, 512:768] * cos4 + q[:, 768:1024] * sin4) * Q_SCALE).T
    kv = _dot(ckv, wukv_ref[...])
    kp = _dot(h, win_ref[:, _C_KPE:_IN_COLS])
    k_rope = (kp[:, :QK_ROPE_DIM] * cos4[:, :QK_ROPE_DIM]
              + kp[:, QK_ROPE_DIM:] * sin4[:, :QK_ROPE_DIM]).astype(BF16)
    v_t = kv[:, 512:].T
    for hd in range(HEADS):
        qt_ref[0, hd, :QK_NOPE_DIM, :] = q_nope_t[hd * 128:(hd + 1) * 128].astype(BF16)
        qt_ref[0, hd, QK_NOPE_DIM:, :] = q_rope_t[hd * 64:(hd + 1) * 64].astype(BF16)
        k_ref[0, hd, :, :QK_NOPE_DIM] = kv[:, hd * 128:(hd + 1) * 128].astype(BF16)
        k_ref[0, hd, :, QK_NOPE_DIM:] = k_rope
        vt_ref[0, hd] = v_t[hd * 128:(hd + 1) * 128].astype(BF16)

    for j, o_ref in enumerate((hq_ref, hi_ref, hg_ref)):
        c0 = _C_HGRN + j * HGRN_WIDTH
        t = _dot(h, win_ref[:, c0:c0 + HGRN_WIDTH])
        for hd in range(HEADS):
            o_ref[0, hd] = t[:, hd * 128:(hd + 1) * 128].astype(o_ref.dtype)

    for d, (lb_ref, kk_ref, b_ref) in enumerate(((lbf_ref, kf_ref, bf_ref), (lbb_ref, kb_ref, bb_ref))):
        c0 = _C_HGRN + (3 + d) * HGRN_WIDTH
        t = lb_ref[...]
        e = jnp.exp(t - jnp.max(t, axis=0, keepdims=True))
        lb = e[0:1] / jnp.sum(e, axis=0, keepdims=True)
        f = lb + (1.0 - lb) * jax.nn.sigmoid(_dot(h, win_ref[:, c0:c0 + HGRN_WIDTH]))
        kk = (1.0 - f).astype(BF16)
        b = _chunk_cumsum(jnp.log2(f), reverse=(d == 1))
        for hd in range(HEADS):
            b_ref[0, hd] = b[:, hd * 128:(hd + 1) * 128]
            kk_ref[0, hd] = kk[:, hd * 128:(hd + 1) * 128]


def _in_proj_call(x, mod_rows, g_mix, w_in, g_q, w_uq, g_kv, w_ukv, cos4, sin4, lb_fwd, lb_bwd, tm):
    B, n, _ = x.shape
    const = lambda b, i: (0, 0)
    head_blk = lambda w: pl.BlockSpec((1, HEADS, tm, w), lambda b, i: (b, 0, i, 0))
    hshape = lambda w, dt: jax.ShapeDtypeStruct((B, HEADS, n, w), dt)
    head_blk_t = lambda w: pl.BlockSpec((1, HEADS, w, tm), lambda b, i: (b, 0, 0, i))
    hshape_t = lambda w: jax.ShapeDtypeStruct((B, HEADS, w, n), BF16)
    return pl.pallas_call(
        _in_proj_kernel,
        grid=(B, n // tm),
        in_specs=[pl.BlockSpec((1, tm, D_MODEL), lambda b, i: (b, i, 0)),
                  pl.BlockSpec((1, 8, D_MODEL), lambda b, i: (b, 0, 0)),
                  pl.BlockSpec((1, D_MODEL), const),
                  pl.BlockSpec((D_MODEL, _IN_COLS), const),
                  pl.BlockSpec((1, Q_LORA_RANK), const),
                  pl.BlockSpec((Q_LORA_RANK, 1024), const),
                  pl.BlockSpec((1, KV_LORA_RANK), const),
                  pl.BlockSpec((KV_LORA_RANK, 1024), const),
                  pl.BlockSpec((tm, 256), lambda b, i: (i, 0)),
                  pl.BlockSpec((tm, 256), lambda b, i: (i, 0)),
                  pl.BlockSpec(lb_fwd.shape, const),
                  pl.BlockSpec(lb_bwd.shape, const)],
        out_specs=[head_blk_t(QK_HEAD_DIM), head_blk(QK_HEAD_DIM), head_blk_t(V_HEAD_DIM),
                   head_blk(128), head_blk(128), head_blk(128),
                   head_blk(128), head_blk(128), head_blk(128), head_blk(128)],
        out_shape=[hshape_t(QK_HEAD_DIM), hshape(QK_HEAD_DIM, BF16), hshape_t(V_HEAD_DIM),
                   hshape(128, BF16), hshape(128, BF16), hshape(128, BF16),
                   hshape(128, BF16), hshape(128, F32), hshape(128, BF16), hshape(128, F32)],
        compiler_params=pltpu.CompilerParams(
            dimension_semantics=("parallel", "parallel"), vmem_limit_bytes=VMEM_LIMIT_BYTES),
        name="in_proj",
    )(x, mod_rows, g_mix, w_in, g_q, w_uq, g_kv, w_ukv, cos4, sin4, lb_fwd, lb_bwd)


def _attn_kernel(qt_ref, kc_ref, kl_ref, vtc_ref, vtl_ref, o_ref, s_ref, *, tq, tk):
    n_ctx = kc_ref.shape[2]
    n_lat = kl_ref.shape[2]
    chunks = [(n_ctx, lambda: kc_ref[0, 0], lambda: vtc_ref[0, 0])]
    for j in range(n_lat // tk):
        chunks.append((tk, lambda j=j: kl_ref[0, 0, j * tk:(j + 1) * tk, :],
                       lambda j=j: vtl_ref[0, 0, :, j * tk:(j + 1) * tk]))

    def scores(j, qt, slot):
        rows, keys, _ = chunks[j]
        k = keys()
        for r in range(0, rows, ATTN_SUB):
            s_ref[slot, r:r + ATTN_SUB, :] = _dot(k[r:r + ATTN_SUB], qt)

    def load_q(t):
        return qt_ref[0, 0, :, pl.ds(pl.multiple_of(t * tq, tq), tq)]

    def q_tile(t, t_next, slot0):
        qt = load_q(t)
        m = l = acc = None
        for j, (rows, _, values_t) in enumerate(chunks):
            slot = (slot0 + j) % 2
            if j + 1 < len(chunks):
                scores(j + 1, qt, 1 - slot)
            else:
                scores(0, load_q(t_next), 1 - slot)
            m_chunk = jnp.max(s_ref[slot, :rows, :], axis=0, keepdims=True)
            m_new = m_chunk if m is None else jnp.maximum(m, m_chunk)
            vt = values_t()
            l_chunk = pv = None
            for r in range(0, rows, ATTN_SUB):
                p = jnp.exp2(s_ref[slot, r:r + ATTN_SUB, :] - m_new)
                l_sub = jnp.sum(p, axis=0, keepdims=True)
                pv_sub = _dot(vt[:, r:r + ATTN_SUB], p.astype(BF16))
                l_chunk = l_sub if l_chunk is None else l_chunk + l_sub
                pv = pv_sub if pv is None else pv + pv_sub
            if m is None:
                l, acc = l_chunk, pv
            else:
                alpha = jnp.exp2(m - m_new)
                l = alpha * l + l_chunk
                acc = alpha * acc + pv
            m = m_new
        o_ref[0, pl.ds(pl.multiple_of(t * tq, tq), tq), :] = (acc * (1.0 / l)).T.astype(o_ref.dtype)

    n_tiles = n_lat // tq
    scores(0, load_q(0), 0)

    def tile_pair(i, _):
        t = 2 * i
        q_tile(t, t + 1, 0)
        q_tile(t + 1, jnp.minimum(t + 2, n_tiles - 1), len(chunks) % 2)
        return 0

    lax.fori_loop(0, n_tiles // 2, tile_pair, 0)


def _attn_call(q_t, k_ctx, k_lat, vt_ctx, vt_lat, tq, tk):
    B, H, _, n = q_t.shape
    n_ctx = k_ctx.shape[2]
    return pl.pallas_call(
        functools.partial(_attn_kernel, tq=tq, tk=tk),
        grid=(B, H),
        in_specs=[pl.BlockSpec((1, 1, QK_HEAD_DIM, n), lambda b, h: (b, h, 0, 0)),
                  pl.BlockSpec((1, 1, n_ctx, QK_HEAD_DIM), lambda b, h: (b, h, 0, 0)),
                  pl.BlockSpec((1, 1, n, QK_HEAD_DIM), lambda b, h: (b, h, 0, 0)),
                  pl.BlockSpec((1, 1, V_HEAD_DIM, n_ctx), lambda b, h: (b, h, 0, 0)),
                  pl.BlockSpec((1, 1, V_HEAD_DIM, n), lambda b, h: (b, h, 0, 0))],
        out_specs=pl.BlockSpec((1, n, V_HEAD_DIM), lambda b, h: (b, 0, h)),
        out_shape=jax.ShapeDtypeStruct((B, n, H * V_HEAD_DIM), BF16),
        scratch_shapes=[pltpu.VMEM((2, max(tk, n_ctx), tq), F32)],
        compiler_params=pltpu.CompilerParams(
            dimension_semantics=("parallel", "parallel"), vmem_limit_bytes=VMEM_LIMIT_BYTES),
        name="attn",
    )(q_t, k_ctx, k_lat, vt_ctx, vt_lat)


_REF_ROWS = ((CHUNK // 2 - 1, CHUNK - 1), (CHUNK // 2, 0))
_CPB = HGRN_BLOCK // CHUNK
_FINISH_ROWS = 512


def _hgrn_kernel(vc_ref, kfc_ref, bfc_ref, kbc_ref, bbc_ref,
                 q_ref, v_ref, hg_ref, kf_ref, bf_ref, kb_ref, bb_ref, gon_ref,
                 o_ref, upd_ref, dec_ref, snap_ref, qd_ref, oin_ref):
    n_ctx = vc_ref.shape[2]
    n_lat = q_ref.shape[2]
    nc_ctx = n_ctx // CHUNK
    nc_lat = n_lat // CHUNK
    masks = _chunk_masks()

    def chunk_rows(x, r):
        return jnp.concatenate(
            [jnp.broadcast_to(x[c * CHUNK + r:c * CHUNK + r + 1], (CHUNK, x.shape[1])) for c in range(_CPB)],
            axis=0)

    def block_a(chunk0, rows, q, v, kks, bs):
        kds, decs, a, qds = [], [], None, []
        for d in range(2):
            r_ref, r_last = _REF_ROWS[d]
            kk, b = kks[d].astype(F32), bs[d]
            kds.append((kk * jnp.exp2(chunk_rows(b, r_last) - b)).astype(BF16))
            decs.append([jnp.exp2(b[c * CHUNK + r_last:c * CHUNK + r_last + 1]) for c in range(_CPB)])
            if q is not None:
                b_ref = chunk_rows(b, r_ref)
                qds.append((q * jnp.exp2(b)).astype(BF16))
                qa = (q * jnp.exp2(b - b_ref)).astype(BF16)
                ka = (kk * jnp.exp2(b_ref - b)).astype(BF16)
                a_d = jnp.where(masks[d], _dot_nt(qa, ka), 0.0)
                a = a_d if a is None else a + a_d
        kd = jnp.concatenate(kds, axis=1)
        for c in range(_CPB):
            rc = slice(c * CHUNK, (c + 1) * CHUNK)
            upd_ref[chunk0 + c] = _dot_tn(v[rc], kd[rc])
            dec_ref[chunk0 + c] = jnp.concatenate([decs[0][c], decs[1][c]], axis=1)
        if q is not None:
            qd_ref[rows, :] = jnp.concatenate(qds, axis=1)
            oin_ref[rows, :] = _dot(a.astype(BF16), v)

    for i in range(n_ctx // HGRN_BLOCK):
        r = slice(i * HGRN_BLOCK, (i + 1) * HGRN_BLOCK)
        block_a(i * _CPB, None, None, vc_ref[0, 0, r, :],
                (kfc_ref[0, 0, r, :], kbc_ref[0, 0, r, :]), (bfc_ref[0, 0, r, :], bbc_ref[0, 0, r, :]))

    def phase_a(i, _):
        r = pl.ds(pl.multiple_of(i * HGRN_BLOCK, HGRN_BLOCK), HGRN_BLOCK)
        block_a(nc_ctx + i * _CPB, r, q_ref[0, 0, r, :].astype(F32), v_ref[0, 0, r, :],
                (kf_ref[0, 0, r, :], kb_ref[0, 0, r, :]), (bf_ref[0, 0, r, :], bb_ref[0, 0, r, :]))
        return 0

    lax.fori_loop(0, n_lat // HGRN_BLOCK, phase_a, 0, unroll=4)

    def advance(st, cf, cb):
        dec = jnp.concatenate([dec_ref[cf][:, :HGRN_DIM], dec_ref[cb][:, HGRN_DIM:]], axis=1)
        upd = jnp.concatenate([upd_ref[cf][:, :HGRN_DIM], upd_ref[cb][:, HGRN_DIM:]], axis=1)
        return st * dec + upd

    st = jnp.zeros((HGRN_DIM, 2 * HGRN_DIM), F32)
    for i in range(nc_ctx):
        st = advance(st, i, nc_ctx - 1 - i)

    def phase_b(i, st):
        cf, cb = i, nc_lat - 1 - i
        sb = st.astype(BF16)
        snap_ref[cf, :, :HGRN_DIM] = sb[:, :HGRN_DIM]
        snap_ref[cb, :, HGRN_DIM:] = sb[:, HGRN_DIM:]
        return advance(st, nc_ctx + cf, nc_ctx + cb)

    lax.fori_loop(0, nc_lat, phase_b, st, unroll=4)

    cpf = _FINISH_ROWS // CHUNK
    def phase_c(i, _):
        r0 = pl.multiple_of(i * _FINISH_ROWS, _FINISH_ROWS)
        inter = [_dot_nt(qd_ref[pl.ds(r0 + c * CHUNK, CHUNK), :], snap_ref[i * cpf + c]) for c in range(cpf)]
        o = oin_ref[pl.ds(r0, _FINISH_ROWS), :] + jnp.concatenate(inter, axis=0)
        y = _rms(o) * gon_ref[...]
        gate = _silu(hg_ref[0, 0, pl.ds(r0, _FINISH_ROWS), :].astype(F32))
        o_ref[0, pl.ds(r0, _FINISH_ROWS), :] = (y * gate).astype(o_ref.dtype)
        return 0

    lax.fori_loop(0, n_lat // _FINISH_ROWS, phase_c, 0)


def _hgrn_call(ctx_ops, lat_ops, g_on):
    B, H, n, _ = lat_ops[0].shape
    n_ctx = ctx_ops[0].shape[2]
    n_chunks = (n + n_ctx) // CHUNK
    ctx_blk = pl.BlockSpec((1, 1, n_ctx, 128), lambda b, h: (b, h, 0, 0))
    lat_blk = pl.BlockSpec((1, 1, n, 128), lambda b, h: (b, h, 0, 0))
    return pl.pallas_call(
        _hgrn_kernel,
        grid=(B, H),
        in_specs=[ctx_blk] * len(ctx_ops) + [lat_blk] * len(lat_ops)
                 + [pl.BlockSpec((1, 128), lambda b, h: (0, 0))],
        out_specs=pl.BlockSpec((1, n, 128), lambda b, h: (b, 0, h)),
        out_shape=jax.ShapeDtypeStruct((B, n, H * 128), BF16),
        scratch_shapes=[pltpu.VMEM((n_chunks, HGRN_DIM, 2 * HGRN_DIM), F32),
                        pltpu.VMEM((n_chunks, 1, 2 * HGRN_DIM), F32),
                        pltpu.VMEM((n // CHUNK, HGRN_DIM, 2 * HGRN_DIM), BF16),
                        pltpu.VMEM((n, 2 * HGRN_DIM), BF16),
                        pltpu.VMEM((n, HGRN_DIM), F32)],
        compiler_params=pltpu.CompilerParams(
            dimension_semantics=("parallel", "parallel"), vmem_limit_bytes=VMEM_LIMIT_BYTES),
        name="hgrn",
    )(*ctx_ops, *lat_ops, g_on)


def _out_ffn_kernel(x_ref, om_ref, oh_ref, mod_ref, gffn_ref, wout_ref, wg_ref, wu_ref, wd_ref,
                    gfin_ref, o_ref):
    x = x_ref[0]
    mix = _dot(om_ref[0], wout_ref[:HEADS * V_HEAD_DIM, :]) + _dot(oh_ref[0], wout_ref[HEADS * V_HEAD_DIM:, :])
    x1 = x + mod_ref[0, 2:3, :] * mix
    gain = gffn_ref[...] * (1.0 + mod_ref[0, 4:5, :])
    h2 = (_rms(x1) * gain + mod_ref[0, 3:4, :]).astype(BF16)
    acc = jnp.zeros((x.shape[0], D_MODEL), F32)
    for j in range(D_FF // FF_CHUNK):
        c0 = j * FF_CHUNK
        g = _dot(h2, wg_ref[:, c0:c0 + FF_CHUNK])
        u = _dot(h2, wu_ref[:, c0:c0 + FF_CHUNK])
        a = (_silu(g) * u).astype(BF16)
        acc = acc + _dot(a, wd_ref[c0:c0 + FF_CHUNK, :])
    x2 = x1 + mod_ref[0, 5:6, :] * acc
    o_ref[0] = _rms(x2) * gfin_ref[...]


def _out_ffn_call(x, o_mla, o_hgrn, mod_rows, g_ffn, w_out, w_gate, w_up, w_down, g_final, tm):
    B, n, _ = x.shape
    const = lambda b, i: (0, 0)
    resident = lambda shape: pl.BlockSpec(shape, const, pipeline_mode=pl.Buffered(1))
    return pl.pallas_call(
        _out_ffn_kernel,
        grid=(B, n // tm),
        in_specs=[pl.BlockSpec((1, tm, D_MODEL), lambda b, i: (b, i, 0)),
                  pl.BlockSpec((1, tm, 512), lambda b, i: (b, i, 0)),
                  pl.BlockSpec((1, tm, 512), lambda b, i: (b, i, 0)),
                  pl.BlockSpec((1, 8, D_MODEL), lambda b, i: (b, 0, 0)),
                  pl.BlockSpec((1, D_MODEL), const),
                  resident((D_MODEL, D_MODEL)),
                  resident((D_MODEL, D_FF)),
                  resident((D_MODEL, D_FF)),
                  resident((D_FF, D_MODEL)),
                  pl.BlockSpec((1, D_MODEL), const)],
        out_specs=pl.BlockSpec((1, tm, D_MODEL), lambda b, i: (b, i, 0)),
        out_shape=jax.ShapeDtypeStruct((B, n, D_MODEL), F32),
        compiler_params=pltpu.CompilerParams(
            dimension_semantics=("parallel", "parallel"), vmem_limit_bytes=VMEM_LIMIT_BYTES),
        name="out_ffn",
    )(x, o_mla, o_hgrn, mod_rows, g_ffn, w_out, w_gate, w_up, w_down, g_final)


_HALF_SWAP = np.concatenate([np.arange(16, 32), np.arange(0, 16), np.arange(48, 64), np.arange(32, 48)])


def _prep_weights(w_in, w_uq, w_ukv):
    offs = np.cumsum((0,) + IN_SIZES)
    cq, ckv, kpe, hq, hi, hg, ff, fb = (w_in[:, offs[i]:offs[i + 1]] for i in range(8))
    w_in_p = jnp.concatenate([cq, ckv, hq, hi, hg, ff, fb, kpe, kpe[:, _HALF_SWAP]], axis=1).astype(BF16)
    uq = w_uq.reshape(Q_LORA_RANK, HEADS, QK_HEAD_DIM)
    rope = uq[:, :, QK_NOPE_DIM:]
    w_uq_p = jnp.concatenate([uq[:, :, :QK_NOPE_DIM].reshape(Q_LORA_RANK, -1),
                              rope.reshape(Q_LORA_RANK, -1),
                              rope[:, :, _HALF_SWAP].reshape(Q_LORA_RANK, -1)], axis=1).astype(BF16)
    ukv = w_ukv.reshape(KV_LORA_RANK, HEADS, QK_NOPE_DIM + V_HEAD_DIM)
    w_ukv_p = jnp.concatenate([ukv[:, :, :QK_NOPE_DIM].reshape(KV_LORA_RANK, -1),
                               ukv[:, :, QK_NOPE_DIM:].reshape(KV_LORA_RANK, -1)], axis=1).astype(BF16)
    return w_in_p, w_uq_p, w_ukv_p


def _rope_tables(n):
    rows = n // GRID_W
    row = jnp.broadcast_to(jnp.arange(rows)[:, None], (rows, GRID_W)).reshape(n)
    col = jnp.broadcast_to(jnp.arange(GRID_W)[None, :], (rows, GRID_W)).reshape(n)
    axis_dim = QK_ROPE_DIM // 2
    inv = 1.0 / (ROPE_THETA ** (jnp.arange(0, axis_dim, 2, dtype=F32) / axis_dim))
    ang_r = row.astype(F32)[:, None] * inv
    ang_c = col.astype(F32)[:, None] * inv
    cos = jnp.concatenate([jnp.cos(ang_r)] * 2 + [jnp.cos(ang_c)] * 2, axis=-1)
    sin = jnp.concatenate([-jnp.sin(ang_r), jnp.sin(ang_r), -jnp.sin(ang_c), jnp.sin(ang_c)], axis=-1)
    return jnp.tile(cos, (1, HEADS)), jnp.tile(sin, (1, HEADS))


def kernel(x, c, ctx, c_ctx, w_mod, b_mod, g_norm_mix, g_norm_ffn, w_in, g_q_norm, w_uq, g_kv_norm,
           w_ukv, lb_fwd, lb_bwd, g_hgrn_norm, w_out, w_gate, w_up, w_down, g_final):
    B, N, D = x.shape
    L = ctx.shape[1]
    layer = 0

    cc = jnp.concatenate([c, c_ctx[None, :], jnp.zeros((7, D), F32)], axis=0)
    mod = _mod_call(cc, w_mod[layer], b_mod[layer][None, :])
    pad = jnp.zeros((B, 2, D), F32)
    mod_lat = jnp.concatenate([mod[:B].reshape(B, 6, D), pad], axis=1)
    mod_ctx = jnp.concatenate([jnp.broadcast_to(mod[B].reshape(1, 6, D), (B, 6, D)), pad], axis=1)

    w_in_p, w_uq_p, w_ukv_p = _prep_weights(w_in[layer], w_uq[layer], w_ukv[layer])
    cos4, sin4 = _rope_tables(N)
    row2 = lambda v: v.reshape(1, -1)
    proj_args = (row2(g_norm_mix[layer]), w_in_p, row2(g_q_norm[layer]), w_uq_p,
                 row2(g_kv_norm[layer]), w_ukv_p)

    q_l, k_l, v_l, hq_l, hi_l, hg_l, *decay_l = _in_proj_call(
        x, mod_lat, *proj_args, cos4, sin4, lb_fwd, lb_bwd, tm=512)
    _, k_c, v_c, _, hi_c, _, *decay_c = _in_proj_call(
        ctx, mod_ctx, *proj_args, jnp.ones((L, 256), F32), jnp.zeros((L, 256), F32), lb_fwd, lb_bwd, tm=L)

    o_mla = _attn_call(q_l, k_c, k_l, v_c, v_l, tq=512, tk=1024)
    o_hgrn = _hgrn_call((hi_c, *decay_c), (hq_l, hi_l, hg_l, *decay_l), row2(g_hgrn_norm[layer]))

    return _out_ffn_call(x, o_mla, o_hgrn, mod_lat, row2(g_norm_ffn[layer]),
                         w_out[layer].astype(BF16), w_gate[layer].astype(BF16),
                         w_up[layer].astype(BF16), w_down[layer].astype(BF16),
                         row2(g_final), tm=512)
```

```python
import functools

import numpy as np
import jax
import jax.numpy as jnp
from jax import lax
from jax.experimental import pallas as pl
from jax.experimental.pallas import tpu as pltpu

D_MODEL = 1024
GRID_W = 64
EPS = 1e-6
ROPE_THETA = 10000.0
V_HEAD_DIM = 128
QK_NOPE_DIM = 128
QK_ROPE_DIM = 64
Q_LORA_RANK = 256
KV_LORA_RANK = 256
HEADS = 4
QK_HEAD_DIM = QK_NOPE_DIM + QK_ROPE_DIM
HGRN_DIM = 128
HGRN_WIDTH = HEADS * HGRN_DIM
CHUNK = 64
IN_SIZES = (Q_LORA_RANK, KV_LORA_RANK, QK_ROPE_DIM,
            HGRN_WIDTH, HGRN_WIDTH, HGRN_WIDTH, HGRN_WIDTH, HGRN_WIDTH)
D_FF = 2816
FF_CHUNK = 256
HGRN_BLOCK = 256
ATTN_SUB = 256
VMEM_LIMIT_BYTES = 56 * 1024 * 1024
Q_SCALE = float(np.log2(np.e) / np.sqrt(QK_HEAD_DIM))

BF16 = jnp.bfloat16
F32 = jnp.float32


def _dot(a, b):
    return jnp.dot(a, b, preferred_element_type=F32)


def _dot_nt(a, b):
    return lax.dot_general(a, b, (((1,), (1,)), ((), ())), preferred_element_type=F32)


def _dot_tn(a, b):
    return lax.dot_general(a, b, (((0,), (0,)), ((), ())), preferred_element_type=F32)


def _silu(x):
    return x * jax.nn.sigmoid(x)


def _rms(x):
    return x * lax.rsqrt(jnp.mean(x * x, axis=-1, keepdims=True) + EPS)


def _mod_kernel(c_ref, w_ref, b_ref, o_ref):
    a = _silu(c_ref[...]).astype(BF16)
    o_ref[...] = _dot(a, w_ref[...].astype(BF16)) + b_ref[...]


def _mod_call(cc, w_mod, b_mod):
    rows = cc.shape[0]
    cols = w_mod.shape[1]
    tn = 1024
    return pl.pallas_call(
        _mod_kernel,
        grid=(cols // tn,),
        in_specs=[pl.BlockSpec((rows, D_MODEL), lambda j: (0, 0)),
                  pl.BlockSpec((D_MODEL, tn), lambda j: (0, j)),
                  pl.BlockSpec((1, tn), lambda j: (0, j))],
        out_specs=pl.BlockSpec((rows, tn), lambda j: (0, j)),
        out_shape=jax.ShapeDtypeStruct((rows, cols), F32),
        compiler_params=pltpu.CompilerParams(dimension_semantics=("arbitrary",)),
        name="mod",
    )(cc, w_mod, b_mod)


_C_CQ = 0
_C_CKV = 256
_C_HGRN = 512
_C_KPE = 512 + 5 * HGRN_WIDTH
_IN_COLS = _C_KPE + 2 * QK_ROPE_DIM


def _chunk_cumsum(g, reverse):
    rows, w = g.shape
    x = g.reshape(rows // 8, 8, w)
    sub = lax.broadcasted_iota(jnp.int32, (1, 8, w), 1)
    for s in (1, 2, 4):
        if reverse:
            x = x + jnp.where(sub < 8 - s, pltpu.roll(x, 8 - s, axis=1), 0.0)
        else:
            x = x + jnp.where(sub >= s, pltpu.roll(x, s, axis=1), 0.0)
    groups = CHUNK // 8
    x = x.reshape(rows // CHUNK, groups, 8, w)
    edge = 0 if reverse else 7
    outs = [None] * groups
    carry = None
    for j in (range(groups - 1, -1, -1) if reverse else range(groups)):
        blk = x[:, j] if carry is None else x[:, j] + carry
        outs[j] = blk
        carry = jnp.broadcast_to(blk[:, edge:edge + 1, :], blk.shape)
    return jnp.stack(outs, axis=1).reshape(rows, w)


def _chunk_masks():
    row = lax.broadcasted_iota(jnp.int32, (HGRN_BLOCK, HGRN_BLOCK), 0)
    col = lax.broadcasted_iota(jnp.int32, (HGRN_BLOCK, HGRN_BLOCK), 1)
    same_chunk = (row // CHUNK) == (col // CHUNK)
    return same_chunk & (col <= row), same_chunk & (col >= row)


def _in_proj_kernel(x_ref, mod_ref, gmix_ref, win_ref, gq_ref, wuq_ref, gkv_ref, wukv_ref,
                    cos_ref, sin_ref, lbf_ref, lbb_ref,
                    qt_ref, k_ref, vt_ref, hq_ref, hi_ref, hg_ref, kf_ref, bf_ref, kb_ref, bb_ref):
    x = x_ref[0]
    shift = mod_ref[0, 0:1, :]
    gain = gmix_ref[...] * (1.0 + mod_ref[0, 1:2, :])
    h = (_rms(x) * gain + shift).astype(BF16)

    cos4 = cos_ref[...]
    sin4 = sin_ref[...]

    lat = _dot(h, win_ref[:, _C_CQ:_C_HGRN])
    cq = (_rms(lat[:, :Q_LORA_RANK]) * gq_ref[...]).astype(BF16)
    ckv = (_rms(lat[:, Q_LORA_RANK:]) * gkv_ref[...]).astype(BF16)

    q = _dot(cq, wuq_ref[...])
    q_nope_t = (q[:, :512] * Q_SCALE).T
    q_rope_t = ((q[:, 512:768] * cos4 + q[:, 768:1024] * sin4) * Q_SCALE).T
    kv = _dot(ckv, wukv_ref[...])
    kp = _dot(h, win_ref[:, _C_KPE:_IN_COLS])
    k_rope = (kp[:, :QK_ROPE_DIM] * cos4[:, :QK_ROPE_DIM]
              + kp[:, QK_ROPE_DIM:] * sin4[:, :QK_ROPE_DIM]).astype(BF16)
    v_t = kv[:, 512:].T
    for hd in range(HEADS):
        qt_ref[0, hd, :QK_NOPE_DIM, :] = q_nope_t[hd * 128:(hd + 1) * 128].astype(BF16)
        qt_ref[0, hd, QK_NOPE_DIM:, :] = q_rope_t[hd * 64:(hd + 1) * 64].astype(BF16)
        k_ref[0, hd, :, :QK_NOPE_DIM] = kv[:, hd * 128:(hd + 1) * 128].astype(BF16)
        k_ref[0, hd, :, QK_NOPE_DIM:] = k_rope
        vt_ref[0, hd] = v_t[hd * 128:(hd + 1) * 128].astype(BF16)

    for j, o_ref in enumerate((hq_ref, hi_ref, hg_ref)):
        c0 = _C_HGRN + j * HGRN_WIDTH
        t = _dot(h, win_ref[:, c0:c0 + HGRN_WIDTH])
        for hd in range(HEADS):
            o_ref[0, hd] = t[:, hd * 128:(hd + 1) * 128].astype(o_ref.dtype)

    for d, (lb_ref, kk_ref, b_ref) in enumerate(((lbf_ref, kf_ref, bf_ref), (lbb_ref, kb_ref, bb_ref))):
        c0 = _C_HGRN + (3 + d) * HGRN_WIDTH
        t = lb_ref[...]
        e = jnp.exp(t - jnp.max(t, axis=0, keepdims=True))
        lb = e[0:1] / jnp.sum(e, axis=0, keepdims=True)
        f = lb + (1.0 - lb) * jax.nn.sigmoid(_dot(h, win_ref[:, c0:c0 + HGRN_WIDTH]))
        kk = (1.0 - f).astype(BF16)
        b = _chunk_cumsum(jnp.log2(f), reverse=(d == 1))
        for hd in range(HEADS):
            b_ref[0, hd] = b[:, hd * 128:(hd + 1) * 128]
            kk_ref[0, hd] = kk[:, hd * 128:(hd + 1) * 128]


def _in_proj_call(x, mod_rows, g_mix, w_in, g_q, w_uq, g_kv, w_ukv, cos4, sin4, lb_fwd, lb_bwd, tm):
    B, n, _ = x.shape
    const = lambda b, i: (0, 0)
    head_blk = lambda w: pl.BlockSpec((1, HEADS, tm, w), lambda b, i: (b, 0, i, 0))
    hshape = lambda w, dt: jax.ShapeDtypeStruct((B, HEADS, n, w), dt)
    head_blk_t = lambda w: pl.BlockSpec((1, HEADS, w, tm), lambda b, i: (b, 0, 0, i))
    hshape_t = lambda w: jax.ShapeDtypeStruct((B, HEADS, w, n), BF16)
    return pl.pallas_call(
        _in_proj_kernel,
        grid=(B, n // tm),
        in_specs=[pl.BlockSpec((1, tm, D_MODEL), lambda b, i: (b, i, 0)),
                  pl.BlockSpec((1, 8, D_MODEL), lambda b, i: (b, 0, 0)),
                  pl.BlockSpec((1, D_MODEL), const),
                  pl.BlockSpec((D_MODEL, _IN_COLS), const),
                  pl.BlockSpec((1, Q_LORA_RANK), const),
                  pl.BlockSpec((Q_LORA_RANK, 1024), const),
                  pl.BlockSpec((1, KV_LORA_RANK), const),
                  pl.BlockSpec((KV_LORA_RANK, 1024), const),
                  pl.BlockSpec((tm, 256), lambda b, i: (i, 0)),
                  pl.BlockSpec((tm, 256), lambda b, i: (i, 0)),
                  pl.BlockSpec(lb_fwd.shape, const),
                  pl.BlockSpec(lb_bwd.shape, const)],
        out_specs=[head_blk_t(QK_HEAD_DIM), head_blk(QK_HEAD_DIM), head_blk_t(V_HEAD_DIM),
                   head_blk(128), head_blk(128), head_blk(128),
                   head_blk(128), head_blk(128), head_blk(128), head_blk(128)],
        out_shape=[hshape_t(QK_HEAD_DIM), hshape(QK_HEAD_DIM, BF16), hshape_t(V_HEAD_DIM),
                   hshape(128, BF16), hshape(128, BF16), hshape(128, BF16),
                   hshape(128, BF16), hshape(128, F32), hshape(128, BF16), hshape(128, F32)],
        compiler_params=pltpu.CompilerParams(
            dimension_semantics=("parallel", "parallel"), vmem_limit_bytes=VMEM_LIMIT_BYTES),
        name="in_proj",
    )(x, mod_rows, g_mix, w_in, g_q, w_uq, g_kv, w_ukv, cos4, sin4, lb_fwd, lb_bwd)


def _attn_kernel(qt_ref, kc_ref, kl_ref, vtc_ref, vtl_ref, o_ref, s_ref, *, tq, tk):
    n_ctx = kc_ref.shape[2]
    n_lat = kl_ref.shape[2]
    chunks = [(n_ctx, lambda: kc_ref[0, 0], lambda: vtc_ref[0, 0])]
    for j in range(n_lat // tk):
        chunks.append((tk, lambda j=j: kl_ref[0, 0, j * tk:(j + 1) * tk, :],
                       lambda j=j: vtl_ref[0, 0, :, j * tk:(j + 1) * tk]))

    def scores(j, qt, slot):
        rows, keys, _ = chunks[j]
        k = keys()
        for r in range(0, rows, ATTN_SUB):
            s_ref[slot, r:r + ATTN_SUB, :] = _dot(k[r:r + ATTN_SUB], qt)

    def load_q(t):
        return qt_ref[0, 0, :, pl.ds(pl.multiple_of(t * tq, tq), tq)]

    def q_tile(t, t_next, slot0):
        qt = load_q(t)
        m = l = acc = None
        for j, (rows, _, values_t) in enumerate(chunks):
            slot = (slot0 + j) % 2
            if j + 1 < len(chunks):
                scores(j + 1, qt, 1 - slot)
            else:
                scores(0, load_q(t_next), 1 - slot)
            m_chunk = jnp.max(s_ref[slot, :rows, :], axis=0, keepdims=True)
            m_new = m_chunk if m is None else jnp.maximum(m, m_chunk)
            vt = values_t()
            l_chunk = pv = None
            for r in range(0, rows, ATTN_SUB):
                p = jnp.exp2(s_ref[slot, r:r + ATTN_SUB, :] - m_new)
                l_sub = jnp.sum(p, axis=0, keepdims=True)
                pv_sub = _dot(vt[:, r:r + ATTN_SUB], p.astype(BF16))
                l_chunk = l_sub if l_chunk is None else l_chunk + l_sub
                pv = pv_sub if pv is None else pv + pv_sub
            if m is None:
                l, acc = l_chunk, pv
            else:
                alpha = jnp.exp2(m - m_new)
                l = alpha * l + l_chunk
                acc = alpha * acc + pv
            m = m_new
        o_ref[0, pl.ds(pl.multiple_of(t * tq, tq), tq), :] = (acc * (1.0 / l)).T.astype(o_ref.dtype)

    n_tiles = n_lat // tq
    scores(0, load_q(0), 0)

    def tile_pair(i, _):
        t = 2 * i
        q_tile(t, t + 1, 0)
        q_tile(t + 1, jnp.minimum(t + 2, n_tiles - 1), len(chunks) % 2)
        return 0

    lax.fori_loop(0, n_tiles // 2, tile_pair, 0)


def _attn_call(q_t, k_ctx, k_lat, vt_ctx, vt_lat, tq, tk):
    B, H, _, n = q_t.shape
    n_ctx = k_ctx.shape[2]
    return pl.pallas_call(
        functools.partial(_attn_kernel, tq=tq, tk=tk),
        grid=(B, H),
        in_specs=[pl.BlockSpec((1, 1, QK_HEAD_DIM, n), lambda b, h: (b, h, 0, 0)),
                  pl.BlockSpec((1, 1, n_ctx, QK_HEAD_DIM), lambda b, h: (b, h, 0, 0)),
                  pl.BlockSpec((1, 1, n, QK_HEAD_DIM), lambda b, h: (b, h, 0, 0)),
                  pl.BlockSpec((1, 1, V_HEAD_DIM, n_ctx), lambda b, h: (b, h, 0, 0)),
                  pl.BlockSpec((1, 1, V_HEAD_DIM, n), lambda b, h: (b, h, 0, 0))],
        out_specs=pl.BlockSpec((1, n, V_HEAD_DIM), lambda b, h: (b, 0, h)),
        out_shape=jax.ShapeDtypeStruct((B, n, H * V_HEAD_DIM), BF16),
        scratch_shapes=[pltpu.VMEM((2, max(tk, n_ctx), tq), F32)],
        compiler_params=pltpu.CompilerParams(
            dimension_semantics=("parallel", "parallel"), vmem_limit_bytes=VMEM_LIMIT_BYTES),
        name="attn",
    )(q_t, k_ctx, k_lat, vt_ctx, vt_lat)


_REF_ROWS = ((CHUNK // 2 - 1, CHUNK - 1), (CHUNK // 2, 0))
_CPB = HGRN_BLOCK // CHUNK
_FINISH_ROWS = 512


def _hgrn_kernel(vc_ref, kfc_ref, bfc_ref, kbc_ref, bbc_ref,
                 q_ref, v_ref, hg_ref, kf_ref, bf_ref, kb_ref, bb_ref, gon_ref,
                 o_ref, upd_ref, dec_ref, snap_ref, qd_ref, oin_ref):
    n_ctx = vc_ref.shape[2]
    n_lat = q_ref.shape[2]
    nc_ctx = n_ctx // CHUNK
    nc_lat = n_lat // CHUNK
    masks = _chunk_masks()

    def chunk_rows(x, r):
        return jnp.concatenate(
            [jnp.broadcast_to(x[c * CHUNK + r:c * CHUNK + r + 1], (CHUNK, x.shape[1])) for c in range(_CPB)],
            axis=0)

    def block_a(chunk0, rows, q, v, kks, bs):
        kds, decs, a, qds = [], [], None, []
        for d in range(2):
            r_ref, r_last = _REF_ROWS[d]
            kk, b = kks[d].astype(F32), bs[d]
            kds.append((kk * jnp.exp2(chunk_rows(b, r_last) - b)).astype(BF16))
            decs.append([jnp.exp2(b[c * CHUNK + r_last:c * CHUNK + r_last + 1]) for c in range(_CPB)])
            if q is not None:
                b_ref = chunk_rows(b, r_ref)
                qds.append((q * jnp.exp2(b)).astype(BF16))
                qa = (q * jnp.exp2(b - b_ref)).astype(BF16)
                ka = (kk * jnp.exp2(b_ref - b)).astype(BF16)
                a_d = jnp.where(masks[d], _dot_nt(qa, ka), 0.0)
                a = a_d if a is None else a + a_d
        kd = jnp.concatenate(kds, axis=1)
        for c in range(_CPB):
            rc = slice(c * CHUNK, (c + 1) * CHUNK)
            upd_ref[chunk0 + c] = _dot_tn(v[rc], kd[rc])
            dec_ref[chunk0 + c] = jnp.concatenate([decs[0][c], decs[1][c]], axis=1)
        if q is not None:
            qd_ref[rows, :] = jnp.concatenate(qds, axis=1)
            oin_ref[rows, :] = _dot(a.astype(BF16), v)

    for i in range(n_ctx // HGRN_BLOCK):
        r = slice(i * HGRN_BLOCK, (i + 1) * HGRN_BLOCK)
        block_a(i * _CPB, None, None, vc_ref[0, 0, r, :],
                (kfc_ref[0, 0, r, :], kbc_ref[0, 0, r, :]), (bfc_ref[0, 0, r, :], bbc_ref[0, 0, r, :]))

    def phase_a(i, _):
        r = pl.ds(pl.multiple_of(i * HGRN_BLOCK, HGRN_BLOCK), HGRN_BLOCK)
        block_a(nc_ctx + i * _CPB, r, q_ref[0, 0, r, :].astype(F32), v_ref[0, 0, r, :],
                (kf_ref[0, 0, r, :], kb_ref[0, 0, r, :]), (bf_ref[0, 0, r, :], bb_ref[0, 0, r, :]))
        return 0

    lax.fori_loop(0, n_lat // HGRN_BLOCK, phase_a, 0, unroll=8)

    def advance(st, cf, cb):
        dec = jnp.concatenate([dec_ref[cf][:, :HGRN_DIM], dec_ref[cb][:, HGRN_DIM:]], axis=1)
        upd = jnp.concatenate([upd_ref[cf][:, :HGRN_DIM], upd_ref[cb][:, HGRN_DIM:]], axis=1)
        return st * dec + upd

    st = jnp.zeros((HGRN_DIM, 2 * HGRN_DIM), F32)
    for i in range(nc_ctx):
        st = advance(st, i, nc_ctx - 1 - i)

    def phase_b(i, st):
        cf, cb = i, nc_lat - 1 - i
        sb = st.astype(BF16)
        snap_ref[cf, :, :HGRN_DIM] = sb[:, :HGRN_DIM]
        snap_ref[cb, :, HGRN_DIM:] = sb[:, HGRN_DIM:]
        return advance(st, nc_ctx + cf, nc_ctx + cb)

    lax.fori_loop(0, nc_lat, phase_b, st, unroll=4)

    cpf = _FINISH_ROWS // CHUNK
    def phase_c(i, _):
        r0 = pl.multiple_of(i * _FINISH_ROWS, _FINISH_ROWS)
        inter = [_dot_nt(qd_ref[pl.ds(r0 + c * CHUNK, CHUNK), :], snap_ref[i * cpf + c]) for c in range(cpf)]
        o = oin_ref[pl.ds(r0, _FINISH_ROWS), :] + jnp.concatenate(inter, axis=0)
        y = _rms(o) * gon_ref[...]
        gate = _silu(hg_ref[0, 0, pl.ds(r0, _FINISH_ROWS), :].astype(F32))
        o_ref[0, pl.ds(r0, _FINISH_ROWS), :] = (y * gate).astype(o_ref.dtype)
        return 0

    lax.fori_loop(0, n_lat // _FINISH_ROWS, phase_c, 0, unroll=4)


def _hgrn_call(ctx_ops, lat_ops, g_on):
    B, H, n, _ = lat_ops[0].shape
    n_ctx = ctx_ops[0].shape[2]
    n_chunks = (n + n_ctx) // CHUNK
    ctx_blk = pl.BlockSpec((1, 1, n_ctx, 128), lambda b, h: (b, h, 0, 0))
    lat_blk = pl.BlockSpec((1, 1, n, 128), lambda b, h: (b, h, 0, 0))
    return pl.pallas_call(
        _hgrn_kernel,
        grid=(B, H),
        in_specs=[ctx_blk] * len(ctx_ops) + [lat_blk] * len(lat_ops)
                 + [pl.BlockSpec((1, 128), lambda b, h: (0, 0))],
        out_specs=pl.BlockSpec((1, n, 128), lambda b, h: (b, 0, h)),
        out_shape=jax.ShapeDtypeStruct((B, n, H * 128), BF16),
        scratch_shapes=[pltpu.VMEM((n_chunks, HGRN_DIM, 2 * HGRN_DIM), F32),
                        pltpu.VMEM((n_chunks, 1, 2 * HGRN_DIM), F32),
                        pltpu.VMEM((n // CHUNK, HGRN_DIM, 2 * HGRN_DIM), BF16),
                        pltpu.VMEM((n, 2 * HGRN_DIM), BF16),
                        pltpu.VMEM((n, HGRN_DIM), F32)],
        compiler_params=pltpu.CompilerParams(
            dimension_semantics=("parallel", "parallel"), vmem_limit_bytes=VMEM_LIMIT_BYTES),
        name="hgrn",
    )(*ctx_ops, *lat_ops, g_on)


def _out_ffn_kernel(x_ref, om_ref, oh_ref, mod_ref, gffn_ref, wout_ref, wg_ref, wu_ref, wd_ref,
                    gfin_ref, o_ref):
    x = x_ref[0]
    mix = _dot(om_ref[0], wout_ref[:HEADS * V_HEAD_DIM, :]) + _dot(oh_ref[0], wout_ref[HEADS * V_HEAD_DIM:, :])
    x1 = x + mod_ref[0, 2:3, :] * mix
    gain = gffn_ref[...] * (1.0 + mod_ref[0, 4:5, :])
    h2 = (_rms(x1) * gain + mod_ref[0, 3:4, :]).astype(BF16)
    acc = jnp.zeros((x.shape[0], D_MODEL), F32)
    for j in range(D_FF // FF_CHUNK):
        c0 = j * FF_CHUNK
        g = _dot(h2, wg_ref[:, c0:c0 + FF_CHUNK])
        u = _dot(h2, wu_ref[:, c0:c0 + FF_CHUNK])
        a = (_silu(g) * u).astype(BF16)
        acc = acc + _dot(a, wd_ref[c0:c0 + FF_CHUNK, :])
    x2 = x1 + mod_ref[0, 5:6, :] * acc
    o_ref[0] = _rms(x2) * gfin_ref[...]


def _out_ffn_call(x, o_mla, o_hgrn, mod_rows, g_ffn, w_out, w_gate, w_up, w_down, g_final, tm):
    B, n, _ = x.shape
    const = lambda b, i: (0, 0)
    resident = lambda shape: pl.BlockSpec(shape, const, pipeline_mode=pl.Buffered(1))
    return pl.pallas_call(
        _out_ffn_kernel,
        grid=(B, n // tm),
        in_specs=[pl.BlockSpec((1, tm, D_MODEL), lambda b, i: (b, i, 0)),
                  pl.BlockSpec((1, tm, 512), lambda b, i: (b, i, 0)),
                  pl.BlockSpec((1, tm, 512), lambda b, i: (b, i, 0)),
                  pl.BlockSpec((1, 8, D_MODEL), lambda b, i: (b, 0, 0)),
                  pl.BlockSpec((1, D_MODEL), const),
                  resident((D_MODEL, D_MODEL)),
                  resident((D_MODEL, D_FF)),
                  resident((D_MODEL, D_FF)),
                  resident((D_FF, D_MODEL)),
                  pl.BlockSpec((1, D_MODEL), const)],
        out_specs=pl.BlockSpec((1, tm, D_MODEL), lambda b, i: (b, i, 0)),
        out_shape=jax.ShapeDtypeStruct((B, n, D_MODEL), F32),
        compiler_params=pltpu.CompilerParams(
            dimension_semantics=("parallel", "parallel"), vmem_limit_bytes=VMEM_LIMIT_BYTES),
        name="out_ffn",
    )(x, o_mla, o_hgrn, mod_rows, g_ffn, w_out, w_gate, w_up, w_down, g_final)


_HALF_SWAP = np.concatenate([np.arange(16, 32), np.arange(0, 16), np.arange(48, 64), np.arange(32, 48)])


def _prep_weights(w_in, w_uq, w_ukv):
    offs = np.cumsum((0,) + IN_SIZES)
    cq, ckv, kpe, hq, hi, hg, ff, fb = (w_in[:, offs[i]:offs[i + 1]] for i in range(8))
    w_in_p = jnp.concatenate([cq, ckv, hq, hi, hg, ff, fb, kpe, kpe[:, _HALF_SWAP]], axis=1).astype(BF16)
    uq = w_uq.reshape(Q_LORA_RANK, HEADS, QK_HEAD_DIM)
    rope = uq[:, :, QK_NOPE_DIM:]
    w_uq_p = jnp.concatenate([uq[:, :, :QK_NOPE_DIM].reshape(Q_LORA_RANK, -1),
                              rope.reshape(Q_LORA_RANK, -1),
                              rope[:, :, _HALF_SWAP].reshape(Q_LORA_RANK, -1)], axis=1).astype(BF16)
    ukv = w_ukv.reshape(KV_LORA_RANK, HEADS, QK_NOPE_DIM + V_HEAD_DIM)
    w_ukv_p = jnp.concatenate([ukv[:, :, :QK_NOPE_DIM].reshape(KV_LORA_RANK, -1),
                               ukv[:, :, QK_NOPE_DIM:].reshape(KV_LORA_RANK, -1)], axis=1).astype(BF16)
    return w_in_p, w_uq_p, w_ukv_p


def _rope_tables(n):
    rows = n // GRID_W
    row = jnp.broadcast_to(jnp.arange(rows)[:, None], (rows, GRID_W)).reshape(n)
    col = jnp.broadcast_to(jnp.arange(GRID_W)[None, :], (rows, GRID_W)).reshape(n)
    axis_dim = QK_ROPE_DIM // 2
    inv = 1.0 / (ROPE_THETA ** (jnp.arange(0, axis_dim, 2, dtype=F32) / axis_dim))
    ang_r = row.astype(F32)[:, None] * inv
    ang_c = col.astype(F32)[:, None] * inv
    cos = jnp.concatenate([jnp.cos(ang_r)] * 2 + [jnp.cos(ang_c)] * 2, axis=-1)
    sin = jnp.concatenate([-jnp.sin(ang_r), jnp.sin(ang_r), -jnp.sin(ang_c), jnp.sin(ang_c)], axis=-1)
    return jnp.tile(cos, (1, HEADS)), jnp.tile(sin, (1, HEADS))


def kernel(x, c, ctx, c_ctx, w_mod, b_mod, g_norm_mix, g_norm_ffn, w_in, g_q_norm, w_uq, g_kv_norm,
           w_ukv, lb_fwd, lb_bwd, g_hgrn_norm, w_out, w_gate, w_up, w_down, g_final):
    B, N, D = x.shape
    L = ctx.shape[1]
    layer = 0

    cc = jnp.concatenate([c, c_ctx[None, :], jnp.zeros((7, D), F32)], axis=0)
    mod = _mod_call(cc, w_mod[layer], b_mod[layer][None, :])
    pad = jnp.zeros((B, 2, D), F32)
    mod_lat = jnp.concatenate([mod[:B].reshape(B, 6, D), pad], axis=1)
    mod_ctx = jnp.concatenate([jnp.broadcast_to(mod[B].reshape(1, 6, D), (B, 6, D)), pad], axis=1)

    w_in_p, w_uq_p, w_ukv_p = _prep_weights(w_in[layer], w_uq[layer], w_ukv[layer])
    cos4, sin4 = _rope_tables(N)
    row2 = lambda v: v.reshape(1, -1)
    proj_args = (row2(g_norm_mix[layer]), w_in_p, row2(g_q_norm[layer]), w_uq_p,
                 row2(g_kv_norm[layer]), w_ukv_p)

    q_l, k_l, v_l, hq_l, hi_l, hg_l, *decay_l = _in_proj_call(
        x, mod_lat, *proj_args, cos4, sin4, lb_fwd, lb_bwd, tm=512)
    _, k_c, v_c, _, hi_c, _, *decay_c = _in_proj_call(
        ctx, mod_ctx, *proj_args, jnp.ones((L, 256), F32), jnp.zeros((L, 256), F32), lb_fwd, lb_bwd, tm=L)

    o_mla = _attn_call(q_l, k_c, k_l, v_c, v_l, tq=512, tk=2048)
    o_hgrn = _hgrn_call((hi_c, *decay_c), (hq_l, hi_l, hg_l, *decay_l), row2(g_hgrn_norm[layer]))

    return _out_ffn_call(x, o_mla, o_hgrn, mod_lat, row2(g_norm_ffn[layer]),
                         w_out[layer].astype(BF16), w_gate[layer].astype(BF16),
                         w_up[layer].astype(BF16), w_down[layer].astype(BF16),
                         row2(g_final), tm=512)
```

```python
import functools

import numpy as np
import jax
import jax.numpy as jnp
from jax import lax
from jax.experimental import pallas as pl
from jax.experimental.pallas import tpu as pltpu

D_MODEL = 1024
GRID_W = 64
EPS = 1e-6
ROPE_THETA = 10000.0
V_HEAD_DIM = 128
QK_NOPE_DIM = 128
QK_ROPE_DIM = 64
Q_LORA_RANK = 256
KV_LORA_RANK = 256
HEADS = 4
QK_HEAD_DIM = QK_NOPE_DIM + QK_ROPE_DIM
HGRN_DIM = 128
HGRN_WIDTH = HEADS * HGRN_DIM
CHUNK = 64
IN_SIZES = (Q_LORA_RANK, KV_LORA_RANK, QK_ROPE_DIM,
            HGRN_WIDTH, HGRN_WIDTH, HGRN_WIDTH, HGRN_WIDTH, HGRN_WIDTH)
D_FF = 2816
FF_CHUNK = 256
HGRN_BLOCK = 256
ATTN_SUB = 256
PROJ_SUB = 256
VMEM_LIMIT_BYTES = 56 * 1024 * 1024
Q_SCALE = float(np.log2(np.e) / np.sqrt(QK_HEAD_DIM))

BF16 = jnp.bfloat16
F32 = jnp.float32


def _dot(a, b):
    return jnp.dot(a, b, preferred_element_type=F32)


def _dot_nt(a, b):
    return lax.dot_general(a, b, (((1,), (1,)), ((), ())), preferred_element_type=F32)


def _dot_tn(a, b):
    return lax.dot_general(a, b, (((0,), (0,)), ((), ())), preferred_element_type=F32)


def _silu(x):
    return x * jax.nn.sigmoid(x)


def _rms(x):
    return x * lax.rsqrt(jnp.mean(x * x, axis=-1, keepdims=True) + EPS)


def _mod_kernel(c_ref, w_ref, b_ref, o_ref):
    a = _silu(c_ref[...]).astype(BF16)
    o_ref[...] = _dot(a, w_ref[...].astype(BF16)) + b_ref[...]


def _mod_call(cc, w_mod, b_mod):
    rows = cc.shape[0]
    cols = w_mod.shape[1]
    tn = 1024
    return pl.pallas_call(
        _mod_kernel,
        grid=(cols // tn,),
        in_specs=[pl.BlockSpec((rows, D_MODEL), lambda j: (0, 0)),
                  pl.BlockSpec((D_MODEL, tn), lambda j: (0, j)),
                  pl.BlockSpec((1, tn), lambda j: (0, j))],
        out_specs=pl.BlockSpec((rows, tn), lambda j: (0, j)),
        out_shape=jax.ShapeDtypeStruct((rows, cols), F32),
        compiler_params=pltpu.CompilerParams(dimension_semantics=("arbitrary",)),
        name="mod",
    )(cc, w_mod, b_mod)


_C_CQ = 0
_C_CKV = 256
_C_HGRN = 512
_C_KPE = 512 + 5 * HGRN_WIDTH
_IN_COLS = _C_KPE + 2 * QK_ROPE_DIM


def _chunk_cumsum(g, reverse):
    rows, w = g.shape
    x = g.reshape(rows // 8, 8, w)
    sub = lax.broadcasted_iota(jnp.int32, (1, 8, w), 1)
    for s in (1, 2, 4):
        if reverse:
            x = x + jnp.where(sub < 8 - s, pltpu.roll(x, 8 - s, axis=1), 0.0)
        else:
            x = x + jnp.where(sub >= s, pltpu.roll(x, s, axis=1), 0.0)
    groups = CHUNK // 8
    x = x.reshape(rows // CHUNK, groups, 8, w)
    edge = 0 if reverse else 7
    outs = [None] * groups
    carry = None
    for j in (range(groups - 1, -1, -1) if reverse else range(groups)):
        blk = x[:, j] if carry is None else x[:, j] + carry
        outs[j] = blk
        carry = jnp.broadcast_to(blk[:, edge:edge + 1, :], blk.shape)
    return jnp.stack(outs, axis=1).reshape(rows, w)


def _chunk_masks():
    row = lax.broadcasted_iota(jnp.int32, (HGRN_BLOCK, HGRN_BLOCK), 0)
    col = lax.broadcasted_iota(jnp.int32, (HGRN_BLOCK, HGRN_BLOCK), 1)
    same_chunk = (row // CHUNK) == (col // CHUNK)
    return same_chunk & (col <= row), same_chunk & (col >= row)


def _in_proj_kernel(x_ref, mod_ref, gmix_ref, win_ref, gq_ref, wuq_ref, gkv_ref, wukv_ref,
                    cos_ref, sin_ref, lbf_ref, lbb_ref,
                    qt_ref, k_ref, vt_ref, hq_ref, hi_ref, hg_ref, kf_ref, bf_ref, kb_ref, bb_ref):
    shift = mod_ref[0, 0:1, :]
    gain = gmix_ref[...] * (1.0 + mod_ref[0, 1:2, :])
    tm = x_ref.shape[1]
    sub = min(tm, PROJ_SUB)

    def lower_bound(lb_ref):
        t = lb_ref[...]
        e = jnp.exp(t - jnp.max(t, axis=0, keepdims=True))
        return e[0:1] / jnp.sum(e, axis=0, keepdims=True)

    lbs = (lower_bound(lbf_ref), lower_bound(lbb_ref))

    def project(h):
        cols = lambda c0, w: _dot(h, win_ref[:, c0:c0 + w])
        return dict(lat=cols(_C_CQ, 512), kp=cols(_C_KPE, 2 * QK_ROPE_DIM),
                    hgrn=[cols(_C_HGRN + j * HGRN_WIDTH, HGRN_WIDTH) for j in range(5)])

    def finish(rows, raw):
        cos4 = cos_ref[rows, :]
        sin4 = sin_ref[rows, :]
        lat, kp = raw["lat"], raw["kp"]
        cq = (_rms(lat[:, :Q_LORA_RANK]) * gq_ref[...]).astype(BF16)
        ckv = (_rms(lat[:, Q_LORA_RANK:]) * gkv_ref[...]).astype(BF16)

        q = _dot(cq, wuq_ref[...])
        q_nope_t = (q[:, :512] * Q_SCALE).T
        q_rope_t = ((q[:, 512:768] * cos4 + q[:, 768:1024] * sin4) * Q_SCALE).T
        kv = _dot(ckv, wukv_ref[...])
        k_rope = (kp[:, :QK_ROPE_DIM] * cos4[:, :QK_ROPE_DIM]
                  + kp[:, QK_ROPE_DIM:] * sin4[:, :QK_ROPE_DIM]).astype(BF16)
        v_t = kv[:, 512:].T
        for hd in range(HEADS):
            qt_ref[0, hd, :QK_NOPE_DIM, rows] = q_nope_t[hd * 128:(hd + 1) * 128].astype(BF16)
            qt_ref[0, hd, QK_NOPE_DIM:, rows] = q_rope_t[hd * 64:(hd + 1) * 64].astype(BF16)
            k_ref[0, hd, rows, :QK_NOPE_DIM] = kv[:, hd * 128:(hd + 1) * 128].astype(BF16)
            k_ref[0, hd, rows, QK_NOPE_DIM:] = k_rope
            vt_ref[0, hd, :, rows] = v_t[hd * 128:(hd + 1) * 128].astype(BF16)

        for t, o_ref in zip(raw["hgrn"][:3], (hq_ref, hi_ref, hg_ref)):
            for hd in range(HEADS):
                o_ref[0, hd, rows, :] = t[:, hd * 128:(hd + 1) * 128].astype(o_ref.dtype)

        for d, (kk_ref, b_ref) in enumerate(((kf_ref, bf_ref), (kb_ref, bb_ref))):
            f = lbs[d] + (1.0 - lbs[d]) * jax.nn.sigmoid(raw["hgrn"][3 + d])
            kk = (1.0 - f).astype(BF16)
            b = _chunk_cumsum(jnp.log2(f), reverse=(d == 1))
            for hd in range(HEADS):
                b_ref[0, hd, rows, :] = b[:, hd * 128:(hd + 1) * 128]
                kk_ref[0, hd, rows, :] = kk[:, hd * 128:(hd + 1) * 128]

    blocks = [slice(r, r + sub) for r in range(0, tm, sub)]
    hs = [(_rms(x_ref[0, rows, :]) * gain + shift).astype(BF16) for rows in blocks]
    raws = [project(h) for h in hs]
    for rows, raw in zip(blocks, raws):
        finish(rows, raw)


def _in_proj_call(x, mod_rows, g_mix, w_in, g_q, w_uq, g_kv, w_ukv, cos4, sin4, lb_fwd, lb_bwd, tm):
    B, n, _ = x.shape
    const = lambda b, i: (0, 0)
    head_blk = lambda w: pl.BlockSpec((1, HEADS, tm, w), lambda b, i: (b, 0, i, 0))
    hshape = lambda w, dt: jax.ShapeDtypeStruct((B, HEADS, n, w), dt)
    head_blk_t = lambda w: pl.BlockSpec((1, HEADS, w, tm), lambda b, i: (b, 0, 0, i))
    hshape_t = lambda w: jax.ShapeDtypeStruct((B, HEADS, w, n), BF16)
    return pl.pallas_call(
        _in_proj_kernel,
        grid=(B, n // tm),
        in_specs=[pl.BlockSpec((1, tm, D_MODEL), lambda b, i: (b, i, 0)),
                  pl.BlockSpec((1, 8, D_MODEL), lambda b, i: (b, 0, 0)),
                  pl.BlockSpec((1, D_MODEL), const),
                  pl.BlockSpec((D_MODEL, _IN_COLS), const),
                  pl.BlockSpec((1, Q_LORA_RANK), const),
                  pl.BlockSpec((Q_LORA_RANK, 1024), const),
                  pl.BlockSpec((1, KV_LORA_RANK), const),
                  pl.BlockSpec((KV_LORA_RANK, 1024), const),
                  pl.BlockSpec((tm, 256), lambda b, i: (i, 0)),
                  pl.BlockSpec((tm, 256), lambda b, i: (i, 0)),
                  pl.BlockSpec(lb_fwd.shape, const),
                  pl.BlockSpec(lb_bwd.shape, const)],
        out_specs=[head_blk_t(QK_HEAD_DIM), head_blk(QK_HEAD_DIM), head_blk_t(V_HEAD_DIM),
                   head_blk(128), head_blk(128), head_blk(128),
                   head_blk(128), head_blk(128), head_blk(128), head_blk(128)],
        out_shape=[hshape_t(QK_HEAD_DIM), hshape(QK_HEAD_DIM, BF16), hshape_t(V_HEAD_DIM),
                   hshape(128, BF16), hshape(128, BF16), hshape(128, BF16),
                   hshape(128, BF16), hshape(128, F32), hshape(128, BF16), hshape(128, F32)],
        compiler_params=pltpu.CompilerParams(
            dimension_semantics=("parallel", "parallel"), vmem_limit_bytes=VMEM_LIMIT_BYTES),
        name="in_proj",
    )(x, mod_rows, g_mix, w_in, g_q, w_uq, g_kv, w_ukv, cos4, sin4, lb_fwd, lb_bwd)


def _attn_kernel(qt_ref, kc_ref, kl_ref, vtc_ref, vtl_ref, o_ref, s_ref, *, tq, tk):
    n_ctx = kc_ref.shape[2]
    n_lat = kl_ref.shape[2]
    chunks = [(n_ctx, lambda: kc_ref[0, 0], lambda: vtc_ref[0, 0])]
    for j in range(n_lat // tk):
        chunks.append((tk, lambda j=j: kl_ref[0, 0, j * tk:(j + 1) * tk, :],
                       lambda j=j: vtl_ref[0, 0, :, j * tk:(j + 1) * tk]))

    def scores(j, qt, slot):
        rows, keys, _ = chunks[j]
        k = keys()
        for r in range(0, rows, ATTN_SUB):
            s_ref[slot, r:r + ATTN_SUB, :] = _dot(k[r:r + ATTN_SUB], qt)

    def load_q(t):
        return qt_ref[0, 0, :, pl.ds(pl.multiple_of(t * tq, tq), tq)]

    def q_tile(t, t_next, slot0):
        qt = load_q(t)
        m = l = acc = None
        for j, (rows, _, values_t) in enumerate(chunks):
            slot = (slot0 + j) % 2
            if j + 1 < len(chunks):
                scores(j + 1, qt, 1 - slot)
            else:
                scores(0, load_q(t_next), 1 - slot)
            m_chunk = jnp.max(s_ref[slot, :rows, :], axis=0, keepdims=True)
            m_new = m_chunk if m is None else jnp.maximum(m, m_chunk)
            vt = values_t()
            l_chunk = pv = None
            for r in range(0, rows, ATTN_SUB):
                p = jnp.exp2(s_ref[slot, r:r + ATTN_SUB, :] - m_new)
                l_sub = jnp.sum(p, axis=0, keepdims=True)
                pv_sub = _dot(vt[:, r:r + ATTN_SUB], p.astype(BF16))
                l_chunk = l_sub if l_chunk is None else l_chunk + l_sub
                pv = pv_sub if pv is None else pv + pv_sub
            if m is None:
                l, acc = l_chunk, pv
            else:
                alpha = jnp.exp2(m - m_new)
                l = alpha * l + l_chunk
                acc = alpha * acc + pv
            m = m_new
        o_ref[0, pl.ds(pl.multiple_of(t * tq, tq), tq), :] = (acc * (1.0 / l)).T.astype(o_ref.dtype)

    n_tiles = n_lat // tq
    scores(0, load_q(0), 0)

    def tile_pair(i, _):
        t = 2 * i
        q_tile(t, t + 1, 0)
        q_tile(t + 1, jnp.minimum(t + 2, n_tiles - 1), len(chunks) % 2)
        return 0

    lax.fori_loop(0, n_tiles // 2, tile_pair, 0)


def _attn_call(q_t, k_ctx, k_lat, vt_ctx, vt_lat, tq, tk):
    B, H, _, n = q_t.shape
    n_ctx = k_ctx.shape[2]
    return pl.pallas_call(
        functools.partial(_attn_kernel, tq=tq, tk=tk),
        grid=(B, H),
        in_specs=[pl.BlockSpec((1, 1, QK_HEAD_DIM, n), lambda b, h: (b, h, 0, 0)),
                  pl.BlockSpec((1, 1, n_ctx, QK_HEAD_DIM), lambda b, h: (b, h, 0, 0)),
                  pl.BlockSpec((1, 1, n, QK_HEAD_DIM), lambda b, h: (b, h, 0, 0)),
                  pl.BlockSpec((1, 1, V_HEAD_DIM, n_ctx), lambda b, h: (b, h, 0, 0)),
                  pl.BlockSpec((1, 1, V_HEAD_DIM, n), lambda b, h: (b, h, 0, 0))],
        out_specs=pl.BlockSpec((1, n, V_HEAD_DIM), lambda b, h: (b, 0, h)),
        out_shape=jax.ShapeDtypeStruct((B, n, H * V_HEAD_DIM), BF16),
        scratch_shapes=[pltpu.VMEM((2, max(tk, n_ctx), tq), F32)],
        compiler_params=pltpu.CompilerParams(
            dimension_semantics=("parallel", "parallel"), vmem_limit_bytes=VMEM_LIMIT_BYTES),
        name="attn",
    )(q_t, k_ctx, k_lat, vt_ctx, vt_lat)


_REF_ROWS = ((CHUNK // 2 - 1, CHUNK - 1), (CHUNK // 2, 0))
_CPB = HGRN_BLOCK // CHUNK
_FINISH_ROWS = 512


def _hgrn_kernel(vc_ref, kfc_ref, bfc_ref, kbc_ref, bbc_ref,
                 q_ref, v_ref, hg_ref, kf_ref, bf_ref, kb_ref, bb_ref, gon_ref,
                 o_ref, upd_ref, dec_ref, snap_ref, qd_ref, oin_ref):
    n_ctx = vc_ref.shape[2]
    n_lat = q_ref.shape[2]
    nc_ctx = n_ctx // CHUNK
    nc_lat = n_lat // CHUNK
    masks = _chunk_masks()

    def chunk_rows(x, r):
        return jnp.concatenate(
            [jnp.broadcast_to(x[c * CHUNK + r:c * CHUNK + r + 1], (CHUNK, x.shape[1])) for c in range(_CPB)],
            axis=0)

    def block_a(chunk0, rows, q, v, kks, bs):
        kds, decs, a, qds = [], [], None, []
        for d in range(2):
            r_ref, r_last = _REF_ROWS[d]
            kk, b = kks[d].astype(F32), bs[d]
            kds.append((kk * jnp.exp2(chunk_rows(b, r_last) - b)).astype(BF16))
            decs.append([jnp.exp2(b[c * CHUNK + r_last:c * CHUNK + r_last + 1]) for c in range(_CPB)])
            if q is not None:
                b_ref = chunk_rows(b, r_ref)
                qds.append((q * jnp.exp2(b)).astype(BF16))
                qa = (q * jnp.exp2(b - b_ref)).astype(BF16)
                ka = (kk * jnp.exp2(b_ref - b)).astype(BF16)
                a_d = jnp.where(masks[d], _dot_nt(qa, ka), 0.0)
                a = a_d if a is None else a + a_d
        kd = jnp.concatenate(kds, axis=1)
        for c in range(_CPB):
            rc = slice(c * CHUNK, (c + 1) * CHUNK)
            upd_ref[chunk0 + c] = _dot_tn(v[rc], kd[rc])
            dec_ref[chunk0 + c] = jnp.concatenate([decs[0][c], decs[1][c]], axis=1)
        if q is not None:
            qd_ref[rows, :] = jnp.concatenate(qds, axis=1)
            oin_ref[rows, :] = _dot(a.astype(BF16), v)

    for i in range(n_ctx // HGRN_BLOCK):
        r = slice(i * HGRN_BLOCK, (i + 1) * HGRN_BLOCK)
        block_a(i * _CPB, None, None, vc_ref[0, 0, r, :],
                (kfc_ref[0, 0, r, :], kbc_ref[0, 0, r, :]), (bfc_ref[0, 0, r, :], bbc_ref[0, 0, r, :]))

    def phase_a(i, _):
        r = pl.ds(pl.multiple_of(i * HGRN_BLOCK, HGRN_BLOCK), HGRN_BLOCK)
        block_a(nc_ctx + i * _CPB, r, q_ref[0, 0, r, :].astype(F32), v_ref[0, 0, r, :],
                (kf_ref[0, 0, r, :], kb_ref[0, 0, r, :]), (bf_ref[0, 0, r, :], bb_ref[0, 0, r, :]))
        return 0

    lax.fori_loop(0, n_lat // HGRN_BLOCK, phase_a, 0, unroll=8)

    def advance(st, cf, cb):
        dec = jnp.concatenate([dec_ref[cf][:, :HGRN_DIM], dec_ref[cb][:, HGRN_DIM:]], axis=1)
        upd = jnp.concatenate([upd_ref[cf][:, :HGRN_DIM], upd_ref[cb][:, HGRN_DIM:]], axis=1)
        return st * dec + upd

    st = jnp.zeros((HGRN_DIM, 2 * HGRN_DIM), F32)
    for i in range(nc_ctx):
        st = advance(st, i, nc_ctx - 1 - i)

    def phase_b(i, st):
        cf, cb = i, nc_lat - 1 - i
        sb = st.astype(BF16)
        snap_ref[cf, :, :HGRN_DIM] = sb[:, :HGRN_DIM]
        snap_ref[cb, :, HGRN_DIM:] = sb[:, HGRN_DIM:]
        return advance(st, nc_ctx + cf, nc_ctx + cb)

    lax.fori_loop(0, nc_lat, phase_b, st, unroll=4)

    cpf = _FINISH_ROWS // CHUNK
    def phase_c(i, _):
        r0 = pl.multiple_of(i * _FINISH_ROWS, _FINISH_ROWS)
        inter = [_dot_nt(qd_ref[pl.ds(r0 + c * CHUNK, CHUNK), :], snap_ref[i * cpf + c]) for c in range(cpf)]
        o = oin_ref[pl.ds(r0, _FINISH_ROWS), :] + jnp.concatenate(inter, axis=0)
        y = _rms(o) * gon_ref[...]
        gate = _silu(hg_ref[0, 0, pl.ds(r0, _FINISH_ROWS), :].astype(F32))
        o_ref[0, pl.ds(r0, _FINISH_ROWS), :] = (y * gate).astype(o_ref.dtype)
        return 0

    lax.fori_loop(0, n_lat // _FINISH_ROWS, phase_c, 0, unroll=4)


def _hgrn_call(ctx_ops, lat_ops, g_on):
    B, H, n, _ = lat_ops[0].shape
    n_ctx = ctx_ops[0].shape[2]
    n_chunks = (n + n_ctx) // CHUNK
    ctx_blk = pl.BlockSpec((1, 1, n_ctx, 128), lambda b, h: (b, h, 0, 0))
    lat_blk = pl.BlockSpec((1, 1, n, 128), lambda b, h: (b, h, 0, 0))
    return pl.pallas_call(
        _hgrn_kernel,
        grid=(B, H),
        in_specs=[ctx_blk] * len(ctx_ops) + [lat_blk] * len(lat_ops)
                 + [pl.BlockSpec((1, 128), lambda b, h: (0, 0))],
        out_specs=pl.BlockSpec((1, n, 128), lambda b, h: (b, 0, h)),
        out_shape=jax.ShapeDtypeStruct((B, n, H * 128), BF16),
        scratch_shapes=[pltpu.VMEM((n_chunks, HGRN_DIM, 2 * HGRN_DIM), F32),
                        pltpu.VMEM((n_chunks, 1, 2 * HGRN_DIM), F32),
                        pltpu.VMEM((n // CHUNK, HGRN_DIM, 2 * HGRN_DIM), BF16),
                        pltpu.VMEM((n, 2 * HGRN_DIM), BF16),
                        pltpu.VMEM((n, HGRN_DIM), F32)],
        compiler_params=pltpu.CompilerParams(
            dimension_semantics=("parallel", "parallel"), vmem_limit_bytes=VMEM_LIMIT_BYTES),
        name="hgrn",
    )(*ctx_ops, *lat_ops, g_on)


def _out_ffn_kernel(x_ref, om_ref, oh_ref, mod_ref, gffn_ref, wout_ref, wg_ref, wu_ref, wd_ref,
                    gfin_ref, o_ref):
    x = x_ref[0]
    mix = _dot(om_ref[0], wout_ref[:HEADS * V_HEAD_DIM, :]) + _dot(oh_ref[0], wout_ref[HEADS * V_HEAD_DIM:, :])
    x1 = x + mod_ref[0, 2:3, :] * mix
    gain = gffn_ref[...] * (1.0 + mod_ref[0, 4:5, :])
    h2 = (_rms(x1) * gain + mod_ref[0, 3:4, :]).astype(BF16)
    acc = jnp.zeros((x.shape[0], D_MODEL), F32)
    for j in range(D_FF // FF_CHUNK):
        c0 = j * FF_CHUNK
        g = _dot(h2, wg_ref[:, c0:c0 + FF_CHUNK])
        u = _dot(h2, wu_ref[:, c0:c0 + FF_CHUNK])
        a = (_silu(g) * u).astype(BF16)
        acc = acc + _dot(a, wd_ref[c0:c0 + FF_CHUNK, :])
    x2 = x1 + mod_ref[0, 5:6, :] * acc
    o_ref[0] = _rms(x2) * gfin_ref[...]


def _out_ffn_call(x, o_mla, o_hgrn, mod_rows, g_ffn, w_out, w_gate, w_up, w_down, g_final, tm):
    B, n, _ = x.shape
    const = lambda b, i: (0, 0)
    resident = lambda shape: pl.BlockSpec(shape, const, pipeline_mode=pl.Buffered(1))
    return pl.pallas_call(
        _out_ffn_kernel,
        grid=(B, n // tm),
        in_specs=[pl.BlockSpec((1, tm, D_MODEL), lambda b, i: (b, i, 0)),
                  pl.BlockSpec((1, tm, 512), lambda b, i: (b, i, 0)),
                  pl.BlockSpec((1, tm, 512), lambda b, i: (b, i, 0)),
                  pl.BlockSpec((1, 8, D_MODEL), lambda b, i: (b, 0, 0)),
                  pl.BlockSpec((1, D_MODEL), const),
                  resident((D_MODEL, D_MODEL)),
                  resident((D_MODEL, D_FF)),
                  resident((D_MODEL, D_FF)),
                  resident((D_FF, D_MODEL)),
                  pl.BlockSpec((1, D_MODEL), const)],
        out_specs=pl.BlockSpec((1, tm, D_MODEL), lambda b, i: (b, i, 0)),
        out_shape=jax.ShapeDtypeStruct((B, n, D_MODEL), F32),
        compiler_params=pltpu.CompilerParams(
            dimension_semantics=("parallel", "parallel"), vmem_limit_bytes=VMEM_LIMIT_BYTES),
        name="out_ffn",
    )(x, o_mla, o_hgrn, mod_rows, g_ffn, w_out, w_gate, w_up, w_down, g_final)


_HALF_SWAP = np.concatenate([np.arange(16, 32), np.arange(0, 16), np.arange(48, 64), np.arange(32, 48)])


def _prep_weights(w_in, w_uq, w_ukv):
    offs = np.cumsum((0,) + IN_SIZES)
    cq, ckv, kpe, hq, hi, hg, ff, fb = (w_in[:, offs[i]:offs[i + 1]] for i in range(8))
    w_in_p = jnp.concatenate([cq, ckv, hq, hi, hg, ff, fb, kpe, kpe[:, _HALF_SWAP]], axis=1).astype(BF16)
    uq = w_uq.reshape(Q_LORA_RANK, HEADS, QK_HEAD_DIM)
    rope = uq[:, :, QK_NOPE_DIM:]
    w_uq_p = jnp.concatenate([uq[:, :, :QK_NOPE_DIM].reshape(Q_LORA_RANK, -1),
                              rope.reshape(Q_LORA_RANK, -1),
                              rope[:, :, _HALF_SWAP].reshape(Q_LORA_RANK, -1)], axis=1).astype(BF16)
    ukv = w_ukv.reshape(KV_LORA_RANK, HEADS, QK_NOPE_DIM + V_HEAD_DIM)
    w_ukv_p = jnp.concatenate([ukv[:, :, :QK_NOPE_DIM].reshape(KV_LORA_RANK, -1),
                               ukv[:, :, QK_NOPE_DIM:].reshape(KV_LORA_RANK, -1)], axis=1).astype(BF16)
    return w_in_p, w_uq_p, w_ukv_p


def _rope_tables(n):
    rows = n // GRID_W
    row = jnp.broadcast_to(jnp.arange(rows)[:, None], (rows, GRID_W)).reshape(n)
    col = jnp.broadcast_to(jnp.arange(GRID_W)[None, :], (rows, GRID_W)).reshape(n)
    axis_dim = QK_ROPE_DIM // 2
    inv = 1.0 / (ROPE_THETA ** (jnp.arange(0, axis_dim, 2, dtype=F32) / axis_dim))
    ang_r = row.astype(F32)[:, None] * inv
    ang_c = col.astype(F32)[:, None] * inv
    cos = jnp.concatenate([jnp.cos(ang_r)] * 2 + [jnp.cos(ang_c)] * 2, axis=-1)
    sin = jnp.concatenate([-jnp.sin(ang_r), jnp.sin(ang_r), -jnp.sin(ang_c), jnp.sin(ang_c)], axis=-1)
    return jnp.tile(cos, (1, HEADS)), jnp.tile(sin, (1, HEADS))


def kernel(x, c, ctx, c_ctx, w_mod, b_mod, g_norm_mix, g_norm_ffn, w_in, g_q_norm, w_uq, g_kv_norm,
           w_ukv, lb_fwd, lb_bwd, g_hgrn_norm, w_out, w_gate, w_up, w_down, g_final):
    B, N, D = x.shape
    L = ctx.shape[1]
    layer = 0

    cc = jnp.concatenate([c, c_ctx[None, :], jnp.zeros((7, D), F32)], axis=0)
    mod = _mod_call(cc, w_mod[layer], b_mod[layer][None, :])
    pad = jnp.zeros((B, 2, D), F32)
    mod_lat = jnp.concatenate([mod[:B].reshape(B, 6, D), pad], axis=1)
    mod_ctx = jnp.concatenate([jnp.broadcast_to(mod[B].reshape(1, 6, D), (B, 6, D)), pad], axis=1)

    w_in_p, w_uq_p, w_ukv_p = _prep_weights(w_in[layer], w_uq[layer], w_ukv[layer])
    cos4, sin4 = _rope_tables(N)
    row2 = lambda v: v.reshape(1, -1)
    proj_args = (row2(g_norm_mix[layer]), w_in_p, row2(g_q_norm[layer]), w_uq_p,
                 row2(g_kv_norm[layer]), w_ukv_p)

    q_l, k_l, v_l, hq_l, hi_l, hg_l, *decay_l = _in_proj_call(
        x, mod_lat, *proj_args, cos4, sin4, lb_fwd, lb_bwd, tm=512)
    _, k_c, v_c, _, hi_c, _, *decay_c = _in_proj_call(
        ctx, mod_ctx, *proj_args, jnp.ones((L, 256), F32), jnp.zeros((L, 256), F32), lb_fwd, lb_bwd, tm=L)

    o_mla = _attn_call(q_l, k_c, k_l, v_c, v_l, tq=512, tk=512)
    o_hgrn = _hgrn_call((hi_c, *decay_c), (hq_l, hi_l, hg_l, *decay_l), row2(g_hgrn_norm[layer]))

    return _out_ffn_call(x, o_mla, o_hgrn, mod_lat, row2(g_norm_ffn[layer]),
                         w_out[layer].astype(BF16), w_gate[layer].astype(BF16),
                         w_up[layer].astype(BF16), w_down[layer].astype(BF16),
                         row2(g_final), tm=512)
```

```python
import functools

import numpy as np
import jax
import jax.numpy as jnp
from jax import lax
from jax.experimental import pallas as pl
from jax.experimental.pallas import tpu as pltpu

D_MODEL = 1024
GRID_W = 64
EPS = 1e-6
ROPE_THETA = 10000.0
V_HEAD_DIM = 128
QK_NOPE_DIM = 128
QK_ROPE_DIM = 64
Q_LORA_RANK = 256
KV_LORA_RANK = 256
HEADS = 4
QK_HEAD_DIM = QK_NOPE_DIM + QK_ROPE_DIM
HGRN_DIM = 128
HGRN_WIDTH = HEADS * HGRN_DIM
CHUNK = 64
IN_SIZES = (Q_LORA_RANK, KV_LORA_RANK, QK_ROPE_DIM,
            HGRN_WIDTH, HGRN_WIDTH, HGRN_WIDTH, HGRN_WIDTH, HGRN_WIDTH)
D_FF = 2816
FF_CHUNK = 256
HGRN_BLOCK = 256
PROJ_ROWS = 512
ATTN_Q_TILE = 512
ATTN_K_TILE = 512
FFN_ROWS = 512
CAST_BLOCKS = 16
VMEM_LIMIT_BYTES = 56 * 1024 * 1024
Q_SCALE = float(np.log2(np.e) / np.sqrt(QK_HEAD_DIM))

BF16 = jnp.bfloat16
F32 = jnp.float32


def _dot(a, b):
    return jnp.dot(a, b, preferred_element_type=F32)


def _dot_nt(a, b):
    return lax.dot_general(a, b, (((1,), (1,)), ((), ())), preferred_element_type=F32)


def _dot_tn(a, b):
    return lax.dot_general(a, b, (((0,), (0,)), ((), ())), preferred_element_type=F32)


def _silu(x):
    return x * jax.nn.sigmoid(x)


def _rms(x):
    return x * lax.rsqrt(jnp.mean(x * x, axis=-1, keepdims=True) + EPS)


def _mod_kernel(c_ref, w_ref, b_ref, win_ref, o_ref, wlat_ref, whg_ref, wkp_ref):
    a = _silu(c_ref[...]).astype(BF16)
    o_ref[...] = _dot(a, w_ref[...].astype(BF16)) + b_ref[...]

    w = win_ref[...]
    offs = np.cumsum((0,) + IN_SIZES)
    wlat_ref[...] = w[:, :offs[2]].astype(BF16)
    kpe = w[:, offs[2]:offs[3]]
    quarter = QK_ROPE_DIM // 4
    swapped = [kpe[:, j * quarter:(j + 1) * quarter] for j in (1, 0, 3, 2)]
    wkp_ref[...] = jnp.concatenate([kpe] + swapped, axis=1).astype(BF16)
    whg_ref[...] = w[:, offs[3]:].astype(BF16)


def _mod_call(cc, w_mod, b_mod, w_in):
    rows = cc.shape[0]
    cols = w_mod.shape[1]
    steps = 8
    tn = cols // steps
    tr = w_in.shape[0] // steps
    n_lat, n_hg, n_kp = Q_LORA_RANK + KV_LORA_RANK, 5 * HGRN_WIDTH, 2 * QK_ROPE_DIM
    row_blk = lambda w: pl.BlockSpec((tr, w), lambda j: (j, 0))
    return pl.pallas_call(
        _mod_kernel,
        grid=(steps,),
        in_specs=[pl.BlockSpec((rows, D_MODEL), lambda j: (0, 0)),
                  pl.BlockSpec((D_MODEL, tn), lambda j: (0, j)),
                  pl.BlockSpec((1, tn), lambda j: (0, j)),
                  row_blk(w_in.shape[1])],
        out_specs=[pl.BlockSpec((rows, tn), lambda j: (0, j)), row_blk(n_lat), row_blk(n_hg), row_blk(n_kp)],
        out_shape=[jax.ShapeDtypeStruct((rows, cols), F32)]
                  + [jax.ShapeDtypeStruct((w_in.shape[0], w), BF16) for w in (n_lat, n_hg, n_kp)],
        compiler_params=pltpu.CompilerParams(dimension_semantics=("arbitrary",)),
        name="mod",
    )(cc, w_mod, b_mod, w_in)


def _chunk_cumsum(g, reverse):
    rows, w = g.shape
    x = g.reshape(rows // 8, 8, w)
    sub = lax.broadcasted_iota(jnp.int32, (1, 8, w), 1)
    for s in (1, 2, 4):
        if reverse:
            x = x + jnp.where(sub < 8 - s, pltpu.roll(x, 8 - s, axis=1), 0.0)
        else:
            x = x + jnp.where(sub >= s, pltpu.roll(x, s, axis=1), 0.0)
    groups = CHUNK // 8
    x = x.reshape(rows // CHUNK, groups, 8, w)
    edge = 0 if reverse else 7
    outs = [None] * groups
    carry = None
    for j in (range(groups - 1, -1, -1) if reverse else range(groups)):
        blk = x[:, j] if carry is None else x[:, j] + carry
        outs[j] = blk
        carry = jnp.broadcast_to(blk[:, edge:edge + 1, :], blk.shape)
    return jnp.stack(outs, axis=1).reshape(rows, w)


def _chunk_masks():
    row = lax.broadcasted_iota(jnp.int32, (HGRN_BLOCK, HGRN_BLOCK), 0)
    col = lax.broadcasted_iota(jnp.int32, (HGRN_BLOCK, HGRN_BLOCK), 1)
    same_chunk = (row // CHUNK) == (col // CHUNK)
    return same_chunk & (col <= row), same_chunk & (col >= row)


def _in_proj_kernel(x_ref, mod_ref, gmix_ref, wlat_ref, whg_ref, wkp_ref, gq_ref, wuq_ref, gkv_ref, wukv_ref,
                    cos_ref, sin_ref, lbf_ref, lbb_ref,
                    qt_ref, k_ref, vt_ref, hq_ref, hi_ref, hg_ref, kf_ref, bf_ref, kb_ref, bb_ref):
    x = x_ref[0]
    shift = mod_ref[0, 0:1, :]
    gain = gmix_ref[...] * (1.0 + mod_ref[0, 1:2, :])
    h = (_rms(x) * gain + shift).astype(BF16)

    cos4 = cos_ref[...]
    sin4 = sin_ref[...]

    lat = _dot(h, wlat_ref[...])
    cq = (_rms(lat[:, :Q_LORA_RANK]) * gq_ref[...]).astype(BF16)
    ckv = (_rms(lat[:, Q_LORA_RANK:]) * gkv_ref[...]).astype(BF16)

    q = _dot(cq, wuq_ref[...])
    q_nope_t = (q[:, :512] * Q_SCALE).T
    q_rope_t = ((q[:, 512:768] * cos4 + q[:, 768:1024] * sin4) * Q_SCALE).T
    kv = _dot(ckv, wukv_ref[...])
    kp = _dot(h, wkp_ref[...])
    k_rope = (kp[:, :QK_ROPE_DIM] * cos4[:, :QK_ROPE_DIM]
              + kp[:, QK_ROPE_DIM:] * sin4[:, :QK_ROPE_DIM]).astype(BF16)
    v_t = kv[:, 512:].T
    for hd in range(HEADS):
        qt_ref[0, hd, :QK_NOPE_DIM, :] = q_nope_t[hd * 128:(hd + 1) * 128].astype(BF16)
        qt_ref[0, hd, QK_NOPE_DIM:, :] = q_rope_t[hd * 64:(hd + 1) * 64].astype(BF16)
        k_ref[0, hd, :, :QK_NOPE_DIM] = kv[:, hd * 128:(hd + 1) * 128].astype(BF16)
        k_ref[0, hd, :, QK_NOPE_DIM:] = k_rope
        vt_ref[0, hd] = v_t[hd * 128:(hd + 1) * 128].astype(BF16)

    for d, (lb_ref, kk_ref, b_ref) in enumerate(((lbf_ref, kf_ref, bf_ref), (lbb_ref, kb_ref, bb_ref))):
        c0 = (3 + d) * HGRN_WIDTH
        t = lb_ref[...]
        e = jnp.exp(t - jnp.max(t, axis=0, keepdims=True))
        lb = e[0:1] / jnp.sum(e, axis=0, keepdims=True)
        f = lb + (1.0 - lb) * jax.nn.sigmoid(_dot(h, whg_ref[:, c0:c0 + HGRN_WIDTH]))
        kk = (1.0 - f).astype(BF16)
        b = _chunk_cumsum(jnp.log2(f), reverse=(d == 1))
        for hd in range(HEADS):
            b_ref[0, hd] = b[:, hd * 128:(hd + 1) * 128]
            kk_ref[0, hd] = kk[:, hd * 128:(hd + 1) * 128]

    for j, o_ref in enumerate((hq_ref, hi_ref, hg_ref)):
        t = _dot(h, whg_ref[:, j * HGRN_WIDTH:(j + 1) * HGRN_WIDTH])
        for hd in range(HEADS):
            o_ref[0, hd] = t[:, hd * 128:(hd + 1) * 128].astype(o_ref.dtype)


def _in_proj_call(x, mod_rows, g_mix, w_lat, w_hgrn, w_kpe, g_q, w_uq, g_kv, w_ukv, cos4, sin4,
                  lb_fwd, lb_bwd, tm):
    B, n, _ = x.shape
    const = lambda b, i: (0, 0)
    head_blk = lambda w: pl.BlockSpec((1, HEADS, tm, w), lambda b, i: (b, 0, i, 0))
    hshape = lambda w, dt: jax.ShapeDtypeStruct((B, HEADS, n, w), dt)
    head_blk_t = lambda w: pl.BlockSpec((1, HEADS, w, tm), lambda b, i: (b, 0, 0, i))
    hshape_t = lambda w: jax.ShapeDtypeStruct((B, HEADS, w, n), BF16)
    return pl.pallas_call(
        _in_proj_kernel,
        grid=(B, n // tm),
        in_specs=[pl.BlockSpec((1, tm, D_MODEL), lambda b, i: (b, i, 0)),
                  pl.BlockSpec((1, 8, D_MODEL), lambda b, i: (b, 0, 0)),
                  pl.BlockSpec((1, D_MODEL), const),
                  pl.BlockSpec(w_lat.shape, const),
                  pl.BlockSpec(w_hgrn.shape, const),
                  pl.BlockSpec(w_kpe.shape, const),
                  pl.BlockSpec((1, Q_LORA_RANK), const),
                  pl.BlockSpec((Q_LORA_RANK, 1024), const),
                  pl.BlockSpec((1, KV_LORA_RANK), const),
                  pl.BlockSpec((KV_LORA_RANK, 1024), const),
                  pl.BlockSpec((tm, 256), lambda b, i: (i, 0)),
                  pl.BlockSpec((tm, 256), lambda b, i: (i, 0)),
                  pl.BlockSpec(lb_fwd.shape, const),
                  pl.BlockSpec(lb_bwd.shape, const)],
        out_specs=[head_blk_t(QK_HEAD_DIM), head_blk(QK_HEAD_DIM), head_blk_t(V_HEAD_DIM),
                   head_blk(128), head_blk(128), head_blk(128),
                   head_blk(128), head_blk(128), head_blk(128), head_blk(128)],
        out_shape=[hshape_t(QK_HEAD_DIM), hshape(QK_HEAD_DIM, BF16), hshape_t(V_HEAD_DIM),
                   hshape(128, BF16), hshape(128, BF16), hshape(128, BF16),
                   hshape(128, BF16), hshape(128, F32), hshape(128, BF16), hshape(128, F32)],
        compiler_params=pltpu.CompilerParams(
            dimension_semantics=("parallel", "parallel"), vmem_limit_bytes=VMEM_LIMIT_BYTES),
        name="in_proj",
    )(x, mod_rows, g_mix, w_lat, w_hgrn, w_kpe, g_q, w_uq, g_kv, w_ukv, cos4, sin4, lb_fwd, lb_bwd)


def _attn_kernel(qt_ref, kc_ref, kl_ref, vtc_ref, vtl_ref, *rest, tq, tk):
    n_cast = (len(rest) - 2) // 2
    o_ref, s_ref = rest[n_cast], rest[-1]
    for w_ref, w16_ref in zip(rest[:n_cast], rest[n_cast + 1:-1]):
        w16_ref[...] = w_ref[...].astype(BF16)

    n_ctx = kc_ref.shape[2]
    n_lat = kl_ref.shape[2]
    chunks = [(n_ctx, lambda: kc_ref[0, 0], lambda: vtc_ref[0, 0])]
    for j in range(n_lat // tk):
        chunks.append((tk, lambda j=j: kl_ref[0, 0, j * tk:(j + 1) * tk, :],
                       lambda j=j: vtl_ref[0, 0, :, j * tk:(j + 1) * tk]))

    def scores(j, qt, slot):
        rows, keys, _ = chunks[j]
        s_ref[slot, :rows, :] = _dot(keys(), qt)

    def load_q(t):
        return qt_ref[0, 0, :, pl.ds(pl.multiple_of(t * tq, tq), tq)]

    def q_tile(t, t_next, slot0):
        qt = load_q(t)
        m = l = acc = None
        for j, (rows, _, values_t) in enumerate(chunks):
            slot = (slot0 + j) % 2
            if j + 1 < len(chunks):
                scores(j + 1, qt, 1 - slot)
            else:
                scores(0, load_q(t_next), 1 - slot)
            s = s_ref[slot, :rows, :]
            m_chunk = jnp.max(s, axis=0, keepdims=True)
            m_new = m_chunk if m is None else jnp.maximum(m, m_chunk)
            p = jnp.exp2(s - m_new)
            l_chunk = jnp.sum(p, axis=0, keepdims=True)
            pv = _dot(values_t(), p.astype(BF16))
            if m is None:
                l, acc = l_chunk, pv
            else:
                alpha = jnp.exp2(m - m_new)
                l = alpha * l + l_chunk
                acc = alpha * acc + pv
            m = m_new
        o_ref[0, pl.ds(pl.multiple_of(t * tq, tq), tq), :] = (acc * (1.0 / l)).T.astype(o_ref.dtype)

    n_tiles = n_lat // tq
    scores(0, load_q(0), 0)

    def tile_pair(i, _):
        t = 2 * i
        q_tile(t, t + 1, 0)
        q_tile(t + 1, jnp.minimum(t + 2, n_tiles - 1), len(chunks) % 2)
        return 0

    lax.fori_loop(0, n_tiles // 2, tile_pair, 0)


def _attn_call(q_t, k_ctx, k_lat, vt_ctx, vt_lat, cast_weights, tq, tk):
    B, H, _, n = q_t.shape
    n_ctx = k_ctx.shape[2]
    steps_per_block = B * H // CAST_BLOCKS
    w_blk = lambda w: pl.BlockSpec((w.shape[0] // CAST_BLOCKS, w.shape[1]),
                                   lambda b, h: ((b * H + h) // steps_per_block, 0))
    outs = pl.pallas_call(
        functools.partial(_attn_kernel, tq=tq, tk=tk),
        grid=(B, H),
        in_specs=[pl.BlockSpec((1, 1, QK_HEAD_DIM, n), lambda b, h: (b, h, 0, 0)),
                  pl.BlockSpec((1, 1, n_ctx, QK_HEAD_DIM), lambda b, h: (b, h, 0, 0)),
                  pl.BlockSpec((1, 1, n, QK_HEAD_DIM), lambda b, h: (b, h, 0, 0)),
                  pl.BlockSpec((1, 1, V_HEAD_DIM, n_ctx), lambda b, h: (b, h, 0, 0)),
                  pl.BlockSpec((1, 1, V_HEAD_DIM, n), lambda b, h: (b, h, 0, 0))]
                 + [w_blk(w) for w in cast_weights],
        out_specs=[pl.BlockSpec((1, n, V_HEAD_DIM), lambda b, h: (b, 0, h))]
                  + [w_blk(w) for w in cast_weights],
        out_shape=[jax.ShapeDtypeStruct((B, n, H * V_HEAD_DIM), BF16)]
                  + [jax.ShapeDtypeStruct(w.shape, BF16) for w in cast_weights],
        scratch_shapes=[pltpu.VMEM((2, max(tk, n_ctx), tq), F32)],
        compiler_params=pltpu.CompilerParams(
            dimension_semantics=("arbitrary", "arbitrary"), vmem_limit_bytes=VMEM_LIMIT_BYTES),
        name="attn",
    )(q_t, k_ctx, k_lat, vt_ctx, vt_lat, *cast_weights)
    return outs[0], outs[1:]


_REF_ROWS = ((CHUNK // 2 - 1, CHUNK - 1), (CHUNK // 2, 0))
_CPB = HGRN_BLOCK // CHUNK
_FINISH_ROWS = 512


def _hgrn_kernel(vc_ref, kfc_ref, bfc_ref, kbc_ref, bbc_ref,
                 q_ref, v_ref, hg_ref, kf_ref, bf_ref, kb_ref, bb_ref, gon_ref,
                 o_ref, upd_ref, dec_ref, snap_ref, qd_ref, oin_ref):
    n_ctx = vc_ref.shape[2]
    n_lat = q_ref.shape[2]
    nc_ctx = n_ctx // CHUNK
    nc_lat = n_lat // CHUNK
    masks = _chunk_masks()

    def chunk_rows(x, r):
        return jnp.concatenate(
            [jnp.broadcast_to(x[c * CHUNK + r:c * CHUNK + r + 1], (CHUNK, x.shape[1])) for c in range(_CPB)],
            axis=0)

    def block_a(chunk0, rows, q, v, kks, bs):
        kds, decs, a, qds = [], [], None, []
        for d in range(2):
            r_ref, r_last = _REF_ROWS[d]
            kk, b = kks[d].astype(F32), bs[d]
            kds.append((kk * jnp.exp2(chunk_rows(b, r_last) - b)).astype(BF16))
            decs.append([jnp.exp2(b[c * CHUNK + r_last:c * CHUNK + r_last + 1]) for c in range(_CPB)])
            if q is not None:
                b_ref = chunk_rows(b, r_ref)
                qds.append((q * jnp.exp2(b)).astype(BF16))
                qa = (q * jnp.exp2(b - b_ref)).astype(BF16)
                ka = (kk * jnp.exp2(b_ref - b)).astype(BF16)
                a_d = jnp.where(masks[d], _dot_nt(qa, ka), 0.0)
                a = a_d if a is None else a + a_d
        kd = jnp.concatenate(kds, axis=1)
        for c in range(_CPB):
            rc = slice(c * CHUNK, (c + 1) * CHUNK)
            upd_ref[chunk0 + c] = _dot_tn(v[rc], kd[rc])
            dec_ref[chunk0 + c] = jnp.concatenate([decs[0][c], decs[1][c]], axis=1)
        if q is not None:
            qd_ref[rows, :] = jnp.concatenate(qds, axis=1)
            oin_ref[rows, :] = _dot(a.astype(BF16), v)

    for i in range(n_ctx // HGRN_BLOCK):
        r = slice(i * HGRN_BLOCK, (i + 1) * HGRN_BLOCK)
        block_a(i * _CPB, None, None, vc_ref[0, 0, r, :],
                (kfc_ref[0, 0, r, :], kbc_ref[0, 0, r, :]), (bfc_ref[0, 0, r, :], bbc_ref[0, 0, r, :]))

    def phase_a(i, _):
        r = pl.ds(pl.multiple_of(i * HGRN_BLOCK, HGRN_BLOCK), HGRN_BLOCK)
        block_a(nc_ctx + i * _CPB, r, q_ref[0, 0, r, :].astype(F32), v_ref[0, 0, r, :],
                (kf_ref[0, 0, r, :], kb_ref[0, 0, r, :]), (bf_ref[0, 0, r, :], bb_ref[0, 0, r, :]))
        return 0

    lax.fori_loop(0, n_lat // HGRN_BLOCK, phase_a, 0, unroll=8)

    def advance(st, cf, cb):
        dec = jnp.concatenate([dec_ref[cf][:, :HGRN_DIM], dec_ref[cb][:, HGRN_DIM:]], axis=1)
        upd = jnp.concatenate([upd_ref[cf][:, :HGRN_DIM], upd_ref[cb][:, HGRN_DIM:]], axis=1)
        return st * dec + upd

    st = jnp.zeros((HGRN_DIM, 2 * HGRN_DIM), F32)
    for i in range(nc_ctx):
        st = advance(st, i, nc_ctx - 1 - i)

    def phase_b(i, st):
        cf, cb = i, nc_lat - 1 - i
        sb = st.astype(BF16)
        snap_ref[cf, :, :HGRN_DIM] = sb[:, :HGRN_DIM]
        snap_ref[cb, :, HGRN_DIM:] = sb[:, HGRN_DIM:]
        return advance(st, nc_ctx + cf, nc_ctx + cb)

    lax.fori_loop(0, nc_lat, phase_b, st, unroll=4)

    cpf = _FINISH_ROWS // CHUNK
    def phase_c(i, _):
        r0 = pl.multiple_of(i * _FINISH_ROWS, _FINISH_ROWS)
        inter = [_dot_nt(qd_ref[pl.ds(r0 + c * CHUNK, CHUNK), :], snap_ref[i * cpf + c]) for c in range(cpf)]
        o = oin_ref[pl.ds(r0, _FINISH_ROWS), :] + jnp.concatenate(inter, axis=0)
        y = _rms(o) * gon_ref[...]
        gate = _silu(hg_ref[0, 0, pl.ds(r0, _FINISH_ROWS), :].astype(F32))
        o_ref[0, pl.ds(r0, _FINISH_ROWS), :] = (y * gate).astype(o_ref.dtype)
        return 0

    lax.fori_loop(0, n_lat // _FINISH_ROWS, phase_c, 0, unroll=8)


def _hgrn_call(ctx_ops, lat_ops, g_on):
    B, H, n, _ = lat_ops[0].shape
    n_ctx = ctx_ops[0].shape[2]
    n_chunks = (n + n_ctx) // CHUNK
    ctx_blk = pl.BlockSpec((1, 1, n_ctx, 128), lambda b, h: (b, h, 0, 0))
    lat_blk = pl.BlockSpec((1, 1, n, 128), lambda b, h: (b, h, 0, 0))
    return pl.pallas_call(
        _hgrn_kernel,
        grid=(B, H),
        in_specs=[ctx_blk] * len(ctx_ops) + [lat_blk] * len(lat_ops)
                 + [pl.BlockSpec((1, 128), lambda b, h: (0, 0))],
        out_specs=pl.BlockSpec((1, n, 128), lambda b, h: (b, 0, h)),
        out_shape=jax.ShapeDtypeStruct((B, n, H * 128), BF16),
        scratch_shapes=[pltpu.VMEM((n_chunks, HGRN_DIM, 2 * HGRN_DIM), F32),
                        pltpu.VMEM((n_chunks, 1, 2 * HGRN_DIM), F32),
                        pltpu.VMEM((n // CHUNK, HGRN_DIM, 2 * HGRN_DIM), BF16),
                        pltpu.VMEM((n, 2 * HGRN_DIM), BF16),
                        pltpu.VMEM((n, HGRN_DIM), F32)],
        compiler_params=pltpu.CompilerParams(
            dimension_semantics=("parallel", "parallel"), vmem_limit_bytes=VMEM_LIMIT_BYTES),
        name="hgrn",
    )(*ctx_ops, *lat_ops, g_on)


def _out_ffn_kernel(x_ref, om_ref, oh_ref, mod_ref, gffn_ref, wout_ref, wg_ref, wu_ref, wd_ref,
                    gfin_ref, o_ref):
    x = x_ref[0]
    mix = _dot(om_ref[0], wout_ref[:HEADS * V_HEAD_DIM, :]) + _dot(oh_ref[0], wout_ref[HEADS * V_HEAD_DIM:, :])
    x1 = x + mod_ref[0, 2:3, :] * mix
    gain = gffn_ref[...] * (1.0 + mod_ref[0, 4:5, :])
    h2 = (_rms(x1) * gain + mod_ref[0, 3:4, :]).astype(BF16)
    acc = jnp.zeros((x.shape[0], D_MODEL), F32)
    for j in range(D_FF // FF_CHUNK):
        c0 = j * FF_CHUNK
        g = _dot(h2, wg_ref[:, c0:c0 + FF_CHUNK])
        u = _dot(h2, wu_ref[:, c0:c0 + FF_CHUNK])
        a = (_silu(g) * u).astype(BF16)
        acc = acc + _dot(a, wd_ref[c0:c0 + FF_CHUNK, :])
    x2 = x1 + mod_ref[0, 5:6, :] * acc
    o_ref[0] = _rms(x2) * gfin_ref[...]


def _out_ffn_call(x, o_mla, o_hgrn, mod_rows, g_ffn, w_out, w_gate, w_up, w_down, g_final, tm):
    B, n, _ = x.shape
    const = lambda b, i: (0, 0)
    resident = lambda shape: pl.BlockSpec(shape, const, pipeline_mode=pl.Buffered(1))
    return pl.pallas_call(
        _out_ffn_kernel,
        grid=(B, n // tm),
        in_specs=[pl.BlockSpec((1, tm, D_MODEL), lambda b, i: (b, i, 0)),
                  pl.BlockSpec((1, tm, 512), lambda b, i: (b, i, 0)),
                  pl.BlockSpec((1, tm, 512), lambda b, i: (b, i, 0)),
                  pl.BlockSpec((1, 8, D_MODEL), lambda b, i: (b, 0, 0)),
                  pl.BlockSpec((1, D_MODEL), const),
                  resident((D_MODEL, D_MODEL)),
                  resident((D_MODEL, D_FF)),
                  resident((D_MODEL, D_FF)),
                  resident((D_FF, D_MODEL)),
                  pl.BlockSpec((1, D_MODEL), const)],
        out_specs=pl.BlockSpec((1, tm, D_MODEL), lambda b, i: (b, i, 0)),
        out_shape=jax.ShapeDtypeStruct((B, n, D_MODEL), F32),
        compiler_params=pltpu.CompilerParams(
            dimension_semantics=("parallel", "parallel"), vmem_limit_bytes=VMEM_LIMIT_BYTES),
        name="out_ffn",
    )(x, o_mla, o_hgrn, mod_rows, g_ffn, w_out, w_gate, w_up, w_down, g_final)


_HALF_SWAP = np.concatenate([np.arange(16, 32), np.arange(0, 16), np.arange(48, 64), np.arange(32, 48)])


def _prep_weights(w_uq, w_ukv):
    uq = w_uq.reshape(Q_LORA_RANK, HEADS, QK_HEAD_DIM)
    rope = uq[:, :, QK_NOPE_DIM:]
    w_uq_p = jnp.concatenate([uq[:, :, :QK_NOPE_DIM].reshape(Q_LORA_RANK, -1),
                              rope.reshape(Q_LORA_RANK, -1),
                              rope[:, :, _HALF_SWAP].reshape(Q_LORA_RANK, -1)], axis=1).astype(BF16)
    ukv = w_ukv.reshape(KV_LORA_RANK, HEADS, QK_NOPE_DIM + V_HEAD_DIM)
    w_ukv_p = jnp.concatenate([ukv[:, :, :QK_NOPE_DIM].reshape(KV_LORA_RANK, -1),
                               ukv[:, :, QK_NOPE_DIM:].reshape(KV_LORA_RANK, -1)], axis=1).astype(BF16)
    return w_uq_p, w_ukv_p


def _rope_tables(n):
    rows = n // GRID_W
    row = jnp.broadcast_to(jnp.arange(rows)[:, None], (rows, GRID_W)).reshape(n)
    col = jnp.broadcast_to(jnp.arange(GRID_W)[None, :], (rows, GRID_W)).reshape(n)
    axis_dim = QK_ROPE_DIM // 2
    inv = 1.0 / (ROPE_THETA ** (jnp.arange(0, axis_dim, 2, dtype=F32) / axis_dim))
    ang_r = row.astype(F32)[:, None] * inv
    ang_c = col.astype(F32)[:, None] * inv
    cos = jnp.concatenate([jnp.cos(ang_r)] * 2 + [jnp.cos(ang_c)] * 2, axis=-1)
    sin = jnp.concatenate([-jnp.sin(ang_r), jnp.sin(ang_r), -jnp.sin(ang_c), jnp.sin(ang_c)], axis=-1)
    return jnp.tile(cos, (1, HEADS)), jnp.tile(sin, (1, HEADS))


def kernel(x, c, ctx, c_ctx, w_mod, b_mod, g_norm_mix, g_norm_ffn, w_in, g_q_norm, w_uq, g_kv_norm,
           w_ukv, lb_fwd, lb_bwd, g_hgrn_norm, w_out, w_gate, w_up, w_down, g_final):
    B, N, D = x.shape
    L = ctx.shape[1]
    layer = 0

    cc = jnp.concatenate([c, c_ctx[None, :], jnp.zeros((7, D), F32)], axis=0)
    mod, *w_in_parts = _mod_call(cc, w_mod[layer], b_mod[layer][None, :], w_in[layer])
    pad = jnp.zeros((B, 2, D), F32)
    mod_lat = jnp.concatenate([mod[:B].reshape(B, 6, D), pad], axis=1)
    mod_ctx = jnp.concatenate([jnp.broadcast_to(mod[B].reshape(1, 6, D), (B, 6, D)), pad], axis=1)

    w_uq_p, w_ukv_p = _prep_weights(w_uq[layer], w_ukv[layer])
    cos4, sin4 = _rope_tables(N)
    row2 = lambda v: v.reshape(1, -1)
    proj_args = (row2(g_norm_mix[layer]), *w_in_parts, row2(g_q_norm[layer]), w_uq_p,
                 row2(g_kv_norm[layer]), w_ukv_p)

    q_l, k_l, v_l, hq_l, hi_l, hg_l, *decay_l = _in_proj_call(
        x, mod_lat, *proj_args, cos4, sin4, lb_fwd, lb_bwd, tm=PROJ_ROWS)
    _, k_c, v_c, _, hi_c, _, *decay_c = _in_proj_call(
        ctx, mod_ctx, *proj_args, jnp.ones((L, 256), F32), jnp.zeros((L, 256), F32), lb_fwd, lb_bwd, tm=L)

    o_mla, (w_out16, w_gate16, w_up16, w_down16) = _attn_call(
        q_l, k_c, k_l, v_c, v_l, (w_out[layer], w_gate[layer], w_up[layer], w_down[layer]),
        tq=ATTN_Q_TILE, tk=ATTN_K_TILE)
    o_hgrn = _hgrn_call((hi_c, *decay_c), (hq_l, hi_l, hg_l, *decay_l), row2(g_hgrn_norm[layer]))

    return _out_ffn_call(x, o_mla, o_hgrn, mod_lat, row2(g_norm_ffn[layer]),
                         w_out16, w_gate16, w_up16, w_down16,
                         row2(g_final), tm=FFN_ROWS)
```

```python
import functools

import numpy as np
import jax
import jax.numpy as jnp
from jax import lax
from jax.experimental import pallas as pl
from jax.experimental.pallas import tpu as pltpu

D_MODEL = 1024
GRID_W = 64
EPS = 1e-6
ROPE_THETA = 10000.0
V_HEAD_DIM = 128
QK_NOPE_DIM = 128
QK_ROPE_DIM = 64
Q_LORA_RANK = 256
KV_LORA_RANK = 256
HEADS = 4
QK_HEAD_DIM = QK_NOPE_DIM + QK_ROPE_DIM
HGRN_DIM = 128
HGRN_WIDTH = HEADS * HGRN_DIM
CHUNK = 64
IN_SIZES = (Q_LORA_RANK, KV_LORA_RANK, QK_ROPE_DIM,
            HGRN_WIDTH, HGRN_WIDTH, HGRN_WIDTH, HGRN_WIDTH, HGRN_WIDTH)
D_FF = 2816
FF_CHUNK = 256
HGRN_BLOCK = 256
PROJ_ROWS = 1024
ATTN_Q_TILE = 512
ATTN_K_TILE = 512
FFN_ROWS = 512
CAST_BLOCKS = 16
VMEM_LIMIT_BYTES = 56 * 1024 * 1024
Q_SCALE = float(np.log2(np.e) / np.sqrt(QK_HEAD_DIM))

BF16 = jnp.bfloat16
F32 = jnp.float32


def _dot(a, b):
    return jnp.dot(a, b, preferred_element_type=F32)


def _dot_nt(a, b):
    return lax.dot_general(a, b, (((1,), (1,)), ((), ())), preferred_element_type=F32)


def _dot_tn(a, b):
    return lax.dot_general(a, b, (((0,), (0,)), ((), ())), preferred_element_type=F32)


def _silu(x):
    return x * jax.nn.sigmoid(x)


def _rms(x):
    return x * lax.rsqrt(jnp.mean(x * x, axis=-1, keepdims=True) + EPS)


def _mod_kernel(c_ref, w_ref, b_ref, win_ref, o_ref, wlat_ref, whg_ref, wkp_ref):
    a = _silu(c_ref[...]).astype(BF16)
    o_ref[...] = _dot(a, w_ref[...].astype(BF16)) + b_ref[...]

    w = win_ref[...]
    offs = np.cumsum((0,) + IN_SIZES)
    wlat_ref[...] = w[:, :offs[2]].astype(BF16)
    kpe = w[:, offs[2]:offs[3]]
    quarter = QK_ROPE_DIM // 4
    swapped = [kpe[:, j * quarter:(j + 1) * quarter] for j in (1, 0, 3, 2)]
    wkp_ref[...] = jnp.concatenate([kpe] + swapped, axis=1).astype(BF16)
    whg_ref[...] = w[:, offs[3]:].astype(BF16)


def _mod_call(cc, w_mod, b_mod, w_in):
    rows = cc.shape[0]
    cols = w_mod.shape[1]
    steps = 8
    tn = cols // steps
    tr = w_in.shape[0] // steps
    n_lat, n_hg, n_kp = Q_LORA_RANK + KV_LORA_RANK, 5 * HGRN_WIDTH, 2 * QK_ROPE_DIM
    row_blk = lambda w: pl.BlockSpec((tr, w), lambda j: (j, 0))
    return pl.pallas_call(
        _mod_kernel,
        grid=(steps,),
        in_specs=[pl.BlockSpec((rows, D_MODEL), lambda j: (0, 0)),
                  pl.BlockSpec((D_MODEL, tn), lambda j: (0, j)),
                  pl.BlockSpec((1, tn), lambda j: (0, j)),
                  row_blk(w_in.shape[1])],
        out_specs=[pl.BlockSpec((rows, tn), lambda j: (0, j)), row_blk(n_lat), row_blk(n_hg), row_blk(n_kp)],
        out_shape=[jax.ShapeDtypeStruct((rows, cols), F32)]
                  + [jax.ShapeDtypeStruct((w_in.shape[0], w), BF16) for w in (n_lat, n_hg, n_kp)],
        compiler_params=pltpu.CompilerParams(dimension_semantics=("arbitrary",)),
        name="mod",
    )(cc, w_mod, b_mod, w_in)


def _chunk_cumsum(g, reverse):
    rows, w = g.shape
    x = g.reshape(rows // 8, 8, w)
    sub = lax.broadcasted_iota(jnp.int32, (1, 8, w), 1)
    for s in (1, 2, 4):
        if reverse:
            x = x + jnp.where(sub < 8 - s, pltpu.roll(x, 8 - s, axis=1), 0.0)
        else:
            x = x + jnp.where(sub >= s, pltpu.roll(x, s, axis=1), 0.0)
    groups = CHUNK // 8
    x = x.reshape(rows // CHUNK, groups, 8, w)
    edge = 0 if reverse else 7
    outs = [None] * groups
    carry = None
    for j in (range(groups - 1, -1, -1) if reverse else range(groups)):
        blk = x[:, j] if carry is None else x[:, j] + carry
        outs[j] = blk
        carry = jnp.broadcast_to(blk[:, edge:edge + 1, :], blk.shape)
    return jnp.stack(outs, axis=1).reshape(rows, w)


def _chunk_masks():
    row = lax.broadcasted_iota(jnp.int32, (HGRN_BLOCK, HGRN_BLOCK), 0)
    col = lax.broadcasted_iota(jnp.int32, (HGRN_BLOCK, HGRN_BLOCK), 1)
    same_chunk = (row // CHUNK) == (col // CHUNK)
    return same_chunk & (col <= row), same_chunk & (col >= row)


def _in_proj_kernel(x_ref, mod_ref, gmix_ref, wlat_ref, whg_ref, wkp_ref, gq_ref, wuq_ref, gkv_ref, wukv_ref,
                    cos_ref, sin_ref, lbf_ref, lbb_ref, *out_refs, queries):
    if queries:
        qt_ref, k_ref, vt_ref, hq_ref, hi_ref, hg_ref, kf_ref, bf_ref, kb_ref, bb_ref = out_refs
        hgrn_outs = ((0, hq_ref), (1, hi_ref), (2, hg_ref))
    else:
        k_ref, vt_ref, hi_ref, kf_ref, bf_ref, kb_ref, bb_ref = out_refs
        hgrn_outs = ((1, hi_ref),)
    x = x_ref[0]
    shift = mod_ref[0, 0:1, :]
    gain = gmix_ref[...] * (1.0 + mod_ref[0, 1:2, :])
    h = (_rms(x) * gain + shift).astype(BF16)

    cos4 = cos_ref[...]
    sin4 = sin_ref[...]

    lat = _dot(h, wlat_ref[...])
    ckv = (_rms(lat[:, Q_LORA_RANK:]) * gkv_ref[...]).astype(BF16)

    if queries:
        q = _dot((_rms(lat[:, :Q_LORA_RANK]) * gq_ref[...]).astype(BF16), wuq_ref[...])
        q_nope_t = (q[:, :512] * Q_SCALE).T
        q_rope_t = ((q[:, 512:768] * cos4 + q[:, 768:1024] * sin4) * Q_SCALE).T
        for hd in range(HEADS):
            qt_ref[0, hd, :QK_NOPE_DIM, :] = q_nope_t[hd * 128:(hd + 1) * 128].astype(BF16)
            qt_ref[0, hd, QK_NOPE_DIM:, :] = q_rope_t[hd * 64:(hd + 1) * 64].astype(BF16)
    kv = _dot(ckv, wukv_ref[...])
    kp = _dot(h, wkp_ref[...])
    k_rope = (kp[:, :QK_ROPE_DIM] * cos4[:, :QK_ROPE_DIM]
              + kp[:, QK_ROPE_DIM:] * sin4[:, :QK_ROPE_DIM]).astype(BF16)
    v_t = kv[:, 512:].T
    for hd in range(HEADS):
        k_ref[0, hd, :, :QK_NOPE_DIM] = kv[:, hd * 128:(hd + 1) * 128].astype(BF16)
        k_ref[0, hd, :, QK_NOPE_DIM:] = k_rope
        vt_ref[0, hd] = v_t[hd * 128:(hd + 1) * 128].astype(BF16)

    for d, (lb_ref, kk_ref, b_ref) in enumerate(((lbf_ref, kf_ref, bf_ref), (lbb_ref, kb_ref, bb_ref))):
        c0 = (3 + d) * HGRN_WIDTH
        t = lb_ref[...]
        e = jnp.exp(t - jnp.max(t, axis=0, keepdims=True))
        lb = e[0:1] / jnp.sum(e, axis=0, keepdims=True)
        f = lb + (1.0 - lb) * jax.nn.sigmoid(_dot(h, whg_ref[:, c0:c0 + HGRN_WIDTH]))
        kk = (1.0 - f).astype(BF16)
        b = _chunk_cumsum(jnp.log2(f), reverse=(d == 1))
        for hd in range(HEADS):
            b_ref[0, hd] = b[:, hd * 128:(hd + 1) * 128]
            kk_ref[0, hd] = kk[:, hd * 128:(hd + 1) * 128]

    for j, o_ref in hgrn_outs:
        t = _dot(h, whg_ref[:, j * HGRN_WIDTH:(j + 1) * HGRN_WIDTH])
        for hd in range(HEADS):
            o_ref[0, hd] = t[:, hd * 128:(hd + 1) * 128].astype(o_ref.dtype)


def _in_proj_call(x, mod_rows, g_mix, w_lat, w_hgrn, w_kpe, g_q, w_uq, g_kv, w_ukv, cos4, sin4,
                  lb_fwd, lb_bwd, tm, queries):
    B, n, _ = x.shape
    const = lambda b, i: (0, 0)
    head_blk = lambda w: pl.BlockSpec((1, HEADS, tm, w), lambda b, i: (b, 0, i, 0))
    hshape = lambda w, dt: jax.ShapeDtypeStruct((B, HEADS, n, w), dt)
    head_blk_t = lambda w: pl.BlockSpec((1, HEADS, w, tm), lambda b, i: (b, 0, 0, i))
    hshape_t = lambda w: jax.ShapeDtypeStruct((B, HEADS, w, n), BF16)
    outs = [(head_blk(QK_HEAD_DIM), hshape(QK_HEAD_DIM, BF16)), (head_blk_t(V_HEAD_DIM), hshape_t(V_HEAD_DIM)),
            (head_blk(128), hshape(128, BF16)),
            (head_blk(128), hshape(128, BF16)), (head_blk(128), hshape(128, F32)),
            (head_blk(128), hshape(128, BF16)), (head_blk(128), hshape(128, F32))]
    if queries:
        outs = ([(head_blk_t(QK_HEAD_DIM), hshape_t(QK_HEAD_DIM))] + outs[:2]
                + [(head_blk(128), hshape(128, BF16)), outs[2], (head_blk(128), hshape(128, BF16))] + outs[3:])
    return pl.pallas_call(
        functools.partial(_in_proj_kernel, queries=queries),
        grid=(B, n // tm),
        in_specs=[pl.BlockSpec((1, tm, D_MODEL), lambda b, i: (b, i, 0)),
                  pl.BlockSpec((1, 8, D_MODEL), lambda b, i: (b, 0, 0)),
                  pl.BlockSpec((1, D_MODEL), const),
                  pl.BlockSpec(w_lat.shape, const),
                  pl.BlockSpec(w_hgrn.shape, const),
                  pl.BlockSpec(w_kpe.shape, const),
                  pl.BlockSpec((1, Q_LORA_RANK), const),
                  pl.BlockSpec((Q_LORA_RANK, 1024), const),
                  pl.BlockSpec((1, KV_LORA_RANK), const),
                  pl.BlockSpec((KV_LORA_RANK, 1024), const),
                  pl.BlockSpec((tm, 256), lambda b, i: (i, 0)),
                  pl.BlockSpec((tm, 256), lambda b, i: (i, 0)),
                  pl.BlockSpec(lb_fwd.shape, const),
                  pl.BlockSpec(lb_bwd.shape, const)],
        out_specs=[spec for spec, _ in outs],
        out_shape=[shape for _, shape in outs],
        compiler_params=pltpu.CompilerParams(
            dimension_semantics=("parallel", "parallel"), vmem_limit_bytes=VMEM_LIMIT_BYTES),
        name="in_proj",
    )(x, mod_rows, g_mix, w_lat, w_hgrn, w_kpe, g_q, w_uq, g_kv, w_ukv, cos4, sin4, lb_fwd, lb_bwd)


def _attn_kernel(qt_ref, kc_ref, kl_ref, vtc_ref, vtl_ref, *rest, tq, tk):
    n_cast = (len(rest) - 2) // 2
    o_ref, s_ref = rest[n_cast], rest[-1]
    for w_ref, w16_ref in zip(rest[:n_cast], rest[n_cast + 1:-1]):
        w16_ref[...] = w_ref[...].astype(BF16)

    n_ctx = kc_ref.shape[2]
    n_lat = kl_ref.shape[2]
    chunks = [(n_ctx, lambda: kc_ref[0, 0], lambda: vtc_ref[0, 0])]
    for j in range(n_lat // tk):
        chunks.append((tk, lambda j=j: kl_ref[0, 0, j * tk:(j + 1) * tk, :],
                       lambda j=j: vtl_ref[0, 0, :, j * tk:(j + 1) * tk]))

    def scores(j, qt, slot):
        rows, keys, _ = chunks[j]
        s_ref[slot, :rows, :] = _dot(keys(), qt)

    def load_q(t):
        return qt_ref[0, 0, :, pl.ds(pl.multiple_of(t * tq, tq), tq)]

    def q_tile(t, t_next, slot0):
        qt = load_q(t)
        m = l = acc = None
        for j, (rows, _, values_t) in enumerate(chunks):
            slot = (slot0 + j) % 2
            if j + 1 < len(chunks):
                scores(j + 1, qt, 1 - slot)
            else:
                scores(0, load_q(t_next), 1 - slot)
            s = s_ref[slot, :rows, :]
            m_chunk = jnp.max(s, axis=0, keepdims=True)
            m_new = m_chunk if m is None else jnp.maximum(m, m_chunk)
            p = jnp.exp2(s - m_new)
            l_chunk = jnp.sum(p, axis=0, keepdims=True)
            pv = _dot(values_t(), p.astype(BF16))
            if m is None:
                l, acc = l_chunk, pv
            else:
                alpha = jnp.exp2(m - m_new)
                l = alpha * l + l_chunk
                acc = alpha * acc + pv
            m = m_new
        o_ref[0, pl.ds(pl.multiple_of(t * tq, tq), tq), :] = (acc * (1.0 / l)).T.astype(o_ref.dtype)

    n_tiles = n_lat // tq
    scores(0, load_q(0), 0)

    def tile_pair(i, _):
        t = 2 * i
        q_tile(t, t + 1, 0)
        q_tile(t + 1, jnp.minimum(t + 2, n_tiles - 1), len(chunks) % 2)
        return 0

    lax.fori_loop(0, n_tiles // 2, tile_pair, 0)


def _attn_call(q_t, k_ctx, k_lat, vt_ctx, vt_lat, cast_weights, tq, tk):
    B, H, _, n = q_t.shape
    n_ctx = k_ctx.shape[2]
    steps_per_block = B * H // CAST_BLOCKS
    w_blk = lambda w: pl.BlockSpec((w.shape[0] // CAST_BLOCKS, w.shape[1]),
                                   lambda b, h: ((b * H + h) // steps_per_block, 0))
    outs = pl.pallas_call(
        functools.partial(_attn_kernel, tq=tq, tk=tk),
        grid=(B, H),
        in_specs=[pl.BlockSpec((1, 1, QK_HEAD_DIM, n), lambda b, h: (b, h, 0, 0)),
                  pl.BlockSpec((1, 1, n_ctx, QK_HEAD_DIM), lambda b, h: (b, h, 0, 0)),
                  pl.BlockSpec((1, 1, n, QK_HEAD_DIM), lambda b, h: (b, h, 0, 0)),
                  pl.BlockSpec((1, 1, V_HEAD_DIM, n_ctx), lambda b, h: (b, h, 0, 0)),
                  pl.BlockSpec((1, 1, V_HEAD_DIM, n), lambda b, h: (b, h, 0, 0))]
                 + [w_blk(w) for w in cast_weights],
        out_specs=[pl.BlockSpec((1, n, V_HEAD_DIM), lambda b, h: (b, 0, h))]
                  + [w_blk(w) for w in cast_weights],
        out_shape=[jax.ShapeDtypeStruct((B, n, H * V_HEAD_DIM), BF16)]
                  + [jax.ShapeDtypeStruct(w.shape, BF16) for w in cast_weights],
        scratch_shapes=[pltpu.VMEM((2, max(tk, n_ctx), tq), F32)],
        compiler_params=pltpu.CompilerParams(
            dimension_semantics=("arbitrary", "arbitrary"), vmem_limit_bytes=VMEM_LIMIT_BYTES),
        name="attn",
    )(q_t, k_ctx, k_lat, vt_ctx, vt_lat, *cast_weights)
    return outs[0], outs[1:]


_REF_ROWS = ((CHUNK // 2 - 1, CHUNK - 1), (CHUNK // 2, 0))
_CPB = HGRN_BLOCK // CHUNK
_FINISH_ROWS = 512


def _hgrn_kernel(vc_ref, kfc_ref, bfc_ref, kbc_ref, bbc_ref,
                 q_ref, v_ref, hg_ref, kf_ref, bf_ref, kb_ref, bb_ref, gon_ref,
                 o_ref, upd_ref, dec_ref, snap_ref, qd_ref, oin_ref):
    n_ctx = vc_ref.shape[2]
    n_lat = q_ref.shape[2]
    nc_ctx = n_ctx // CHUNK
    nc_lat = n_lat // CHUNK
    masks = _chunk_masks()

    def chunk_rows(x, r):
        return jnp.concatenate(
            [jnp.broadcast_to(x[c * CHUNK + r:c * CHUNK + r + 1], (CHUNK, x.shape[1])) for c in range(_CPB)],
            axis=0)

    def block_a(chunk0, rows, q, v, kks, bs):
        kds, decs, a, qds = [], [], None, []
        for d in range(2):
            r_ref, r_last = _REF_ROWS[d]
            kk, b = kks[d].astype(F32), bs[d]
            kds.append((kk * jnp.exp2(chunk_rows(b, r_last) - b)).astype(BF16))
            decs.append([jnp.exp2(b[c * CHUNK + r_last:c * CHUNK + r_last + 1]) for c in range(_CPB)])
            if q is not None:
                b_ref = chunk_rows(b, r_ref)
                qds.append((q * jnp.exp2(b)).astype(BF16))
                qa = (q * jnp.exp2(b - b_ref)).astype(BF16)
                ka = (kk * jnp.exp2(b_ref - b)).astype(BF16)
                a_d = jnp.where(masks[d], _dot_nt(qa, ka), 0.0)
                a = a_d if a is None else a + a_d
        kd = jnp.concatenate(kds, axis=1)
        for c in range(_CPB):
            rc = slice(c * CHUNK, (c + 1) * CHUNK)
            upd_ref[chunk0 + c] = _dot_tn(v[rc], kd[rc])
            dec_ref[chunk0 + c] = jnp.concatenate([decs[0][c], decs[1][c]], axis=1)
        if q is not None:
            qd_ref[rows, :] = jnp.concatenate(qds, axis=1)
            oin_ref[rows, :] = _dot(a.astype(BF16), v)

    for i in range(n_ctx // HGRN_BLOCK):
        r = slice(i * HGRN_BLOCK, (i + 1) * HGRN_BLOCK)
        block_a(i * _CPB, None, None, vc_ref[0, 0, r, :],
                (kfc_ref[0, 0, r, :], kbc_ref[0, 0, r, :]), (bfc_ref[0, 0, r, :], bbc_ref[0, 0, r, :]))

    def phase_a(i, _):
        r = pl.ds(pl.multiple_of(i * HGRN_BLOCK, HGRN_BLOCK), HGRN_BLOCK)
        block_a(nc_ctx + i * _CPB, r, q_ref[0, 0, r, :].astype(F32), v_ref[0, 0, r, :],
                (kf_ref[0, 0, r, :], kb_ref[0, 0, r, :]), (bf_ref[0, 0, r, :], bb_ref[0, 0, r, :]))
        return 0

    lax.fori_loop(0, n_lat // HGRN_BLOCK, phase_a, 0, unroll=8)

    def advance(st, cf, cb):
        dec = jnp.concatenate([dec_ref[cf][:, :HGRN_DIM], dec_ref[cb][:, HGRN_DIM:]], axis=1)
        upd = jnp.concatenate([upd_ref[cf][:, :HGRN_DIM], upd_ref[cb][:, HGRN_DIM:]], axis=1)
        return st * dec + upd

    st = jnp.zeros((HGRN_DIM, 2 * HGRN_DIM), F32)
    for i in range(nc_ctx):
        st = advance(st, i, nc_ctx - 1 - i)

    def phase_b(i, st):
        cf, cb = i, nc_lat - 1 - i
        sb = st.astype(BF16)
        snap_ref[cf, :, :HGRN_DIM] = sb[:, :HGRN_DIM]
        snap_ref[cb, :, HGRN_DIM:] = sb[:, HGRN_DIM:]
        return advance(st, nc_ctx + cf, nc_ctx + cb)

    lax.fori_loop(0, nc_lat, phase_b, st, unroll=4)

    cpf = _FINISH_ROWS // CHUNK
    def phase_c(i, _):
        r0 = pl.multiple_of(i * _FINISH_ROWS, _FINISH_ROWS)
        inter = [_dot_nt(qd_ref[pl.ds(r0 + c * CHUNK, CHUNK), :], snap_ref[i * cpf + c]) for c in range(cpf)]
        o = oin_ref[pl.ds(r0, _FINISH_ROWS), :] + jnp.concatenate(inter, axis=0)
        y = _rms(o) * gon_ref[...]
        gate = _silu(hg_ref[0, 0, pl.ds(r0, _FINISH_ROWS), :].astype(F32))
        o_ref[0, pl.ds(r0, _FINISH_ROWS), :] = (y * gate).astype(o_ref.dtype)
        return 0

    lax.fori_loop(0, n_lat // _FINISH_ROWS, phase_c, 0, unroll=8)


def _hgrn_call(ctx_ops, lat_ops, g_on):
    B, H, n, _ = lat_ops[0].shape
    n_ctx = ctx_ops[0].shape[2]
    n_chunks = (n + n_ctx) // CHUNK
    ctx_blk = pl.BlockSpec((1, 1, n_ctx, 128), lambda b, h: (b, h, 0, 0))
    lat_blk = pl.BlockSpec((1, 1, n, 128), lambda b, h: (b, h, 0, 0))
    return pl.pallas_call(
        _hgrn_kernel,
        grid=(B, H),
        in_specs=[ctx_blk] * len(ctx_ops) + [lat_blk] * len(lat_ops)
                 + [pl.BlockSpec((1, 128), lambda b, h: (0, 0))],
        out_specs=pl.BlockSpec((1, n, 128), lambda b, h: (b, 0, h)),
        out_shape=jax.ShapeDtypeStruct((B, n, H * 128), BF16),
        scratch_shapes=[pltpu.VMEM((n_chunks, HGRN_DIM, 2 * HGRN_DIM), F32),
                        pltpu.VMEM((n_chunks, 1, 2 * HGRN_DIM), F32),
                        pltpu.VMEM((n // CHUNK, HGRN_DIM, 2 * HGRN_DIM), BF16),
                        pltpu.VMEM((n, 2 * HGRN_DIM), BF16),
                        pltpu.VMEM((n, HGRN_DIM), F32)],
        compiler_params=pltpu.CompilerParams(
            dimension_semantics=("parallel", "parallel"), vmem_limit_bytes=VMEM_LIMIT_BYTES),
        name="hgrn",
    )(*ctx_ops, *lat_ops, g_on)


def _out_ffn_kernel(x_ref, om_ref, oh_ref, mod_ref, gffn_ref, wout_ref, wg_ref, wu_ref, wd_ref,
                    gfin_ref, o_ref):
    x = x_ref[0]
    mix = _dot(om_ref[0], wout_ref[:HEADS * V_HEAD_DIM, :]) + _dot(oh_ref[0], wout_ref[HEADS * V_HEAD_DIM:, :])
    x1 = x + mod_ref[0, 2:3, :] * mix
    gain = gffn_ref[...] * (1.0 + mod_ref[0, 4:5, :])
    h2 = (_rms(x1) * gain + mod_ref[0, 3:4, :]).astype(BF16)
    acc = jnp.zeros((x.shape[0], D_MODEL), F32)
    for j in range(D_FF // FF_CHUNK):
        c0 = j * FF_CHUNK
        g = _dot(h2, wg_ref[:, c0:c0 + FF_CHUNK])
        u = _dot(h2, wu_ref[:, c0:c0 + FF_CHUNK])
        a = (_silu(g) * u).astype(BF16)
        acc = acc + _dot(a, wd_ref[c0:c0 + FF_CHUNK, :])
    x2 = x1 + mod_ref[0, 5:6, :] * acc
    o_ref[0] = _rms(x2) * gfin_ref[...]


def _out_ffn_call(x, o_mla, o_hgrn, mod_rows, g_ffn, w_out, w_gate, w_up, w_down, g_final, tm):
    B, n, _ = x.shape
    const = lambda b, i: (0, 0)
    resident = lambda shape: pl.BlockSpec(shape, const, pipeline_mode=pl.Buffered(1))
    return pl.pallas_call(
        _out_ffn_kernel,
        grid=(B, n // tm),
        in_specs=[pl.BlockSpec((1, tm, D_MODEL), lambda b, i: (b, i, 0)),
                  pl.BlockSpec((1, tm, 512), lambda b, i: (b, i, 0)),
                  pl.BlockSpec((1, tm, 512), lambda b, i: (b, i, 0)),
                  pl.BlockSpec((1, 8, D_MODEL), lambda b, i: (b, 0, 0)),
                  pl.BlockSpec((1, D_MODEL), const),
                  resident((D_MODEL, D_MODEL)),
                  resident((D_MODEL, D_FF)),
                  resident((D_MODEL, D_FF)),
                  resident((D_FF, D_MODEL)),
                  pl.BlockSpec((1, D_MODEL), const)],
        out_specs=pl.BlockSpec((1, tm, D_MODEL), lambda b, i: (b, i, 0)),
        out_shape=jax.ShapeDtypeStruct((B, n, D_MODEL), F32),
        compiler_params=pltpu.CompilerParams(
            dimension_semantics=("parallel", "parallel"), vmem_limit_bytes=VMEM_LIMIT_BYTES),
        name="out_ffn",
    )(x, o_mla, o_hgrn, mod_rows, g_ffn, w_out, w_gate, w_up, w_down, g_final)


_HALF_SWAP = np.concatenate([np.arange(16, 32), np.arange(0, 16), np.arange(48, 64), np.arange(32, 48)])


def _prep_weights(w_uq, w_ukv):
    uq = w_uq.reshape(Q_LORA_RANK, HEADS, QK_HEAD_DIM)
    rope = uq[:, :, QK_NOPE_DIM:]
    w_uq_p = jnp.concatenate([uq[:, :, :QK_NOPE_DIM].reshape(Q_LORA_RANK, -1),
                              rope.reshape(Q_LORA_RANK, -1),
                              rope[:, :, _HALF_SWAP].reshape(Q_LORA_RANK, -1)], axis=1).astype(BF16)
    ukv = w_ukv.reshape(KV_LORA_RANK, HEADS, QK_NOPE_DIM + V_HEAD_DIM)
    w_ukv_p = jnp.concatenate([ukv[:, :, :QK_NOPE_DIM].reshape(KV_LORA_RANK, -1),
                               ukv[:, :, QK_NOPE_DIM:].reshape(KV_LORA_RANK, -1)], axis=1).astype(BF16)
    return w_uq_p, w_ukv_p


def _rope_tables(n):
    rows = n // GRID_W
    row = jnp.broadcast_to(jnp.arange(rows)[:, None], (rows, GRID_W)).reshape(n)
    col = jnp.broadcast_to(jnp.arange(GRID_W)[None, :], (rows, GRID_W)).reshape(n)
    axis_dim = QK_ROPE_DIM // 2
    inv = 1.0 / (ROPE_THETA ** (jnp.arange(0, axis_dim, 2, dtype=F32) / axis_dim))
    ang_r = row.astype(F32)[:, None] * inv
    ang_c = col.astype(F32)[:, None] * inv
    cos = jnp.concatenate([jnp.cos(ang_r)] * 2 + [jnp.cos(ang_c)] * 2, axis=-1)
    sin = jnp.concatenate([-jnp.sin(ang_r), jnp.sin(ang_r), -jnp.sin(ang_c), jnp.sin(ang_c)], axis=-1)
    return jnp.tile(cos, (1, HEADS)), jnp.tile(sin, (1, HEADS))


def kernel(x, c, ctx, c_ctx, w_mod, b_mod, g_norm_mix, g_norm_ffn, w_in, g_q_norm, w_uq, g_kv_norm,
           w_ukv, lb_fwd, lb_bwd, g_hgrn_norm, w_out, w_gate, w_up, w_down, g_final):
    B, N, D = x.shape
    L = ctx.shape[1]
    layer = 0

    cc = jnp.concatenate([c, c_ctx[None, :], jnp.zeros((7, D), F32)], axis=0)
    mod, *w_in_parts = _mod_call(cc, w_mod[layer], b_mod[layer][None, :], w_in[layer])
    pad = jnp.zeros((B, 2, D), F32)
    mod_lat = jnp.concatenate([mod[:B].reshape(B, 6, D), pad], axis=1)
    mod_ctx = jnp.concatenate([jnp.broadcast_to(mod[B].reshape(1, 6, D), (B, 6, D)), pad], axis=1)

    w_uq_p, w_ukv_p = _prep_weights(w_uq[layer], w_ukv[layer])
    cos4, sin4 = _rope_tables(N)
    row2 = lambda v: v.reshape(1, -1)
    proj_args = (row2(g_norm_mix[layer]), *w_in_parts, row2(g_q_norm[layer]), w_uq_p,
                 row2(g_kv_norm[layer]), w_ukv_p)

    q_l, k_l, v_l, hq_l, hi_l, hg_l, *decay_l = _in_proj_call(
        x, mod_lat, *proj_args, cos4, sin4, lb_fwd, lb_bwd, tm=PROJ_ROWS, queries=True)
    k_c, v_c, hi_c, *decay_c = _in_proj_call(
        ctx, mod_ctx, *proj_args, jnp.ones((L, 256), F32), jnp.zeros((L, 256), F32), lb_fwd, lb_bwd,
        tm=L, queries=False)

    o_mla, (w_out16, w_gate16, w_up16, w_down16) = _attn_call(
        q_l, k_c, k_l, v_c, v_l, (w_out[layer], w_gate[layer], w_up[layer], w_down[layer]),
        tq=ATTN_Q_TILE, tk=ATTN_K_TILE)
    o_hgrn = _hgrn_call((hi_c, *decay_c), (hq_l, hi_l, hg_l, *decay_l), row2(g_hgrn_norm[layer]))

    return _out_ffn_call(x, o_mla, o_hgrn, mod_lat, row2(g_norm_ffn[layer]),
                         w_out16, w_gate16, w_up16, w_down16,
                         row2(g_final), tm=FFN_ROWS)
```

```python
import functools

import numpy as np
import jax
import jax.numpy as jnp
from jax import lax
from jax.experimental import pallas as pl
from jax.experimental.pallas import tpu as pltpu

D_MODEL = 1024
GRID_W = 64
EPS = 1e-6
ROPE_THETA = 10000.0
V_HEAD_DIM = 128
QK_NOPE_DIM = 128
QK_ROPE_DIM = 64
Q_LORA_RANK = 256
KV_LORA_RANK = 256
HEADS = 4
QK_HEAD_DIM = QK_NOPE_DIM + QK_ROPE_DIM
HGRN_DIM = 128
HGRN_WIDTH = HEADS * HGRN_DIM
CHUNK = 64
IN_SIZES = (Q_LORA_RANK, KV_LORA_RANK, QK_ROPE_DIM,
            HGRN_WIDTH, HGRN_WIDTH, HGRN_WIDTH, HGRN_WIDTH, HGRN_WIDTH)
D_FF = 2816
FF_CHUNK = 256
HGRN_BLOCK = 256
PROJ_ROWS = 1024
ATTN_Q_TILE = 512
ATTN_K_TILE = 512
FFN_ROWS = 512
CAST_BLOCKS = 16
VMEM_LIMIT_BYTES = 56 * 1024 * 1024
Q_SCALE = float(np.log2(np.e) / np.sqrt(QK_HEAD_DIM))

BF16 = jnp.bfloat16
F32 = jnp.float32


def _dot(a, b):
    return jnp.dot(a, b, preferred_element_type=F32)


def _dot_nt(a, b):
    return lax.dot_general(a, b, (((1,), (1,)), ((), ())), preferred_element_type=F32)


def _dot_tn(a, b):
    return lax.dot_general(a, b, (((0,), (0,)), ((), ())), preferred_element_type=F32)


def _silu(x):
    return x * jax.nn.sigmoid(x)


def _rms(x):
    return x * lax.rsqrt(jnp.mean(x * x, axis=-1, keepdims=True) + EPS)


_MOD_STEPS = 8
_HG_COLS = 512


def _mod_kernel(c_ref, w_ref, b_ref, wt_ref, o_ref, wlat_ref, whg_ref, wkp_ref):
    a = _silu(c_ref[...]).astype(BF16)
    o_ref[...] = _dot(a, w_ref[...].astype(BF16)) + b_ref[...]

    j = pl.program_id(0)
    offs = np.cumsum((0,) + IN_SIZES)
    n_hg = (offs[-1] - offs[3]) // _HG_COLS

    @pl.when(j < n_hg)
    def _():
        r0 = pl.multiple_of(offs[3] + j * _HG_COLS, CHUNK)
        whg_ref[...] = wt_ref[pl.ds(r0, _HG_COLS), :].T.astype(BF16)

    @pl.when(j == n_hg)
    def _():
        wlat_ref[...] = wt_ref[:offs[2], :].T.astype(BF16)

    @pl.when(j == n_hg + 1)
    def _():
        kpe = wt_ref[offs[2]:offs[3], :]
        quarter = QK_ROPE_DIM // 4
        swapped = [kpe[q * quarter:(q + 1) * quarter] for q in (1, 0, 3, 2)]
        wkp_ref[...] = jnp.concatenate([kpe] + swapped, axis=0).T.astype(BF16)


def _mod_call(cc, w_mod, b_mod, w_in_t):
    rows = cc.shape[0]
    cols = w_mod.shape[1]
    tn = cols // _MOD_STEPS
    n_lat, n_hg, n_kp = Q_LORA_RANK + KV_LORA_RANK, 5 * HGRN_WIDTH, 2 * QK_ROPE_DIM
    assert n_hg // _HG_COLS + 2 <= _MOD_STEPS
    const = lambda j: (0, 0)
    return pl.pallas_call(
        _mod_kernel,
        grid=(_MOD_STEPS,),
        in_specs=[pl.BlockSpec((rows, D_MODEL), const),
                  pl.BlockSpec((D_MODEL, tn), lambda j: (0, j)),
                  pl.BlockSpec((1, tn), lambda j: (0, j)),
                  pl.BlockSpec(w_in_t.shape, const, pipeline_mode=pl.Buffered(1))],
        out_specs=[pl.BlockSpec((rows, tn), lambda j: (0, j)),
                   pl.BlockSpec((D_MODEL, n_lat), const),
                   pl.BlockSpec((D_MODEL, _HG_COLS), lambda j: (0, jnp.minimum(j, n_hg // _HG_COLS - 1))),
                   pl.BlockSpec((D_MODEL, n_kp), const)],
        out_shape=[jax.ShapeDtypeStruct((rows, cols), F32)]
                  + [jax.ShapeDtypeStruct((D_MODEL, w), BF16) for w in (n_lat, n_hg, n_kp)],
        compiler_params=pltpu.CompilerParams(dimension_semantics=("arbitrary",),
                                             vmem_limit_bytes=VMEM_LIMIT_BYTES),
        name="mod",
    )(cc, w_mod, b_mod, w_in_t)


def _chunk_cumsum(g, reverse):
    rows, w = g.shape
    x = g.reshape(rows // 8, 8, w)
    sub = lax.broadcasted_iota(jnp.int32, (1, 8, w), 1)
    for s in (1, 2, 4):
        if reverse:
            x = x + jnp.where(sub < 8 - s, pltpu.roll(x, 8 - s, axis=1), 0.0)
        else:
            x = x + jnp.where(sub >= s, pltpu.roll(x, s, axis=1), 0.0)
    groups = CHUNK // 8
    x = x.reshape(rows // CHUNK, groups, 8, w)
    edge = 0 if reverse else 7
    outs = [None] * groups
    carry = None
    for j in (range(groups - 1, -1, -1) if reverse else range(groups)):
        blk = x[:, j] if carry is None else x[:, j] + carry
        outs[j] = blk
        carry = jnp.broadcast_to(blk[:, edge:edge + 1, :], blk.shape)
    return jnp.stack(outs, axis=1).reshape(rows, w)


def _chunk_masks():
    row = lax.broadcasted_iota(jnp.int32, (HGRN_BLOCK, HGRN_BLOCK), 0)
    col = lax.broadcasted_iota(jnp.int32, (HGRN_BLOCK, HGRN_BLOCK), 1)
    same_chunk = (row // CHUNK) == (col // CHUNK)
    return same_chunk & (col <= row), same_chunk & (col >= row)


def _in_proj_kernel(x_ref, mod_ref, gmix_ref, wlat_ref, whg_ref, wkp_ref, gq_ref, wuq_ref, gkv_ref, wukv_ref,
                    cos_ref, sin_ref, lbf_ref, lbb_ref, *out_refs, queries):
    if queries:
        qt_ref, k_ref, vt_ref, hq_ref, hi_ref, hg_ref, kf_ref, bf_ref, kb_ref, bb_ref = out_refs
        hgrn_outs = ((0, hq_ref), (1, hi_ref), (2, hg_ref))
    else:
        k_ref, vt_ref, hi_ref, kf_ref, bf_ref, kb_ref, bb_ref = out_refs
        hgrn_outs = ((1, hi_ref),)
    x = x_ref[0]
    shift = mod_ref[0, 0:1, :]
    gain = gmix_ref[...] * (1.0 + mod_ref[0, 1:2, :])
    h = (_rms(x) * gain + shift).astype(BF16)

    cos1 = cos_ref[...]
    sin1 = sin_ref[...]

    lat = _dot(h, wlat_ref[...])
    ckv = (_rms(lat[:, Q_LORA_RANK:]) * gkv_ref[...]).astype(BF16)

    if queries:
        q = _dot((_rms(lat[:, :Q_LORA_RANK]) * gq_ref[...]).astype(BF16), wuq_ref[...])
        cos4 = jnp.concatenate([cos1] * HEADS, axis=1)
        sin4 = jnp.concatenate([sin1] * HEADS, axis=1)
        q_nope_t = (q[:, :512] * Q_SCALE).T
        q_rope_t = ((q[:, 512:768] * cos4 + q[:, 768:1024] * sin4) * Q_SCALE).T
        for hd in range(HEADS):
            qt_ref[0, hd, :QK_NOPE_DIM, :] = q_nope_t[hd * 128:(hd + 1) * 128].astype(BF16)
            qt_ref[0, hd, QK_NOPE_DIM:, :] = q_rope_t[hd * 64:(hd + 1) * 64].astype(BF16)
    kv = _dot(ckv, wukv_ref[...])
    kp = _dot(h, wkp_ref[...])
    k_rope = (kp[:, :QK_ROPE_DIM] * cos1 + kp[:, QK_ROPE_DIM:] * sin1).astype(BF16)
    v_t = kv[:, 512:].T
    for hd in range(HEADS):
        k_ref[0, hd, :, :QK_NOPE_DIM] = kv[:, hd * 128:(hd + 1) * 128].astype(BF16)
        k_ref[0, hd, :, QK_NOPE_DIM:] = k_rope
        vt_ref[0, hd] = v_t[hd * 128:(hd + 1) * 128].astype(BF16)

    for d, (lb_ref, kk_ref, b_ref) in enumerate(((lbf_ref, kf_ref, bf_ref), (lbb_ref, kb_ref, bb_ref))):
        c0 = (3 + d) * HGRN_WIDTH
        t = lb_ref[...]
        e = jnp.exp(t - jnp.max(t, axis=0, keepdims=True))
        lb = e[0:1] / jnp.sum(e, axis=0, keepdims=True)
        f = lb + (1.0 - lb) * jax.nn.sigmoid(_dot(h, whg_ref[:, c0:c0 + HGRN_WIDTH]))
        kk = (1.0 - f).astype(BF16)
        b = _chunk_cumsum(jnp.log2(f), reverse=(d == 1))
        for hd in range(HEADS):
            b_ref[0, hd] = b[:, hd * 128:(hd + 1) * 128]
            kk_ref[0, hd] = kk[:, hd * 128:(hd + 1) * 128]

    for j, o_ref in hgrn_outs:
        t = _dot(h, whg_ref[:, j * HGRN_WIDTH:(j + 1) * HGRN_WIDTH])
        for hd in range(HEADS):
            o_ref[0, hd] = t[:, hd * 128:(hd + 1) * 128].astype(o_ref.dtype)


def _in_proj_call(x, mod_rows, g_mix, w_lat, w_hgrn, w_kpe, g_q, w_uq, g_kv, w_ukv, cos, sin,
                  lb_fwd, lb_bwd, tm, queries):
    B, n, _ = x.shape
    const = lambda b, i: (0, 0)
    head_blk = lambda w: pl.BlockSpec((1, HEADS, tm, w), lambda b, i: (b, 0, i, 0))
    hshape = lambda w, dt: jax.ShapeDtypeStruct((B, HEADS, n, w), dt)
    head_blk_t = lambda w: pl.BlockSpec((1, HEADS, w, tm), lambda b, i: (b, 0, 0, i))
    hshape_t = lambda w: jax.ShapeDtypeStruct((B, HEADS, w, n), BF16)
    outs = [(head_blk(QK_HEAD_DIM), hshape(QK_HEAD_DIM, BF16)), (head_blk_t(V_HEAD_DIM), hshape_t(V_HEAD_DIM)),
            (head_blk(128), hshape(128, BF16)),
            (head_blk(128), hshape(128, BF16)), (head_blk(128), hshape(128, F32)),
            (head_blk(128), hshape(128, BF16)), (head_blk(128), hshape(128, F32))]
    if queries:
        outs = ([(head_blk_t(QK_HEAD_DIM), hshape_t(QK_HEAD_DIM))] + outs[:2]
                + [(head_blk(128), hshape(128, BF16)), outs[2], (head_blk(128), hshape(128, BF16))] + outs[3:])
    return pl.pallas_call(
        functools.partial(_in_proj_kernel, queries=queries),
        grid=(B, n // tm),
        in_specs=[pl.BlockSpec((1, tm, D_MODEL), lambda b, i: (b, i, 0)),
                  pl.BlockSpec((1, 8, D_MODEL), lambda b, i: (b, 0, 0)),
                  pl.BlockSpec((1, D_MODEL), const),
                  pl.BlockSpec(w_lat.shape, const),
                  pl.BlockSpec(w_hgrn.shape, const),
                  pl.BlockSpec(w_kpe.shape, const),
                  pl.BlockSpec((1, Q_LORA_RANK), const),
                  pl.BlockSpec((Q_LORA_RANK, 1024), const),
                  pl.BlockSpec((1, KV_LORA_RANK), const),
                  pl.BlockSpec((KV_LORA_RANK, 1024), const),
                  pl.BlockSpec((tm, QK_ROPE_DIM), lambda b, i: (i, 0)),
                  pl.BlockSpec((tm, QK_ROPE_DIM), lambda b, i: (i, 0)),
                  pl.BlockSpec(lb_fwd.shape, const),
                  pl.BlockSpec(lb_bwd.shape, const)],
        out_specs=[spec for spec, _ in outs],
        out_shape=[shape for _, shape in outs],
        compiler_params=pltpu.CompilerParams(
            dimension_semantics=("parallel", "parallel"), vmem_limit_bytes=VMEM_LIMIT_BYTES),
        name="in_proj",
    )(x, mod_rows, g_mix, w_lat, w_hgrn, w_kpe, g_q, w_uq, g_kv, w_ukv, cos, sin, lb_fwd, lb_bwd)


def _attn_kernel(qt_ref, kc_ref, kl_ref, vtc_ref, vtl_ref, *rest, tq, tk):
    n_cast = (len(rest) - 2) // 2
    o_ref, s_ref = rest[n_cast], rest[-1]
    for w_ref, w16_ref in zip(rest[:n_cast], rest[n_cast + 1:-1]):
        w16_ref[...] = w_ref[...].astype(BF16)

    n_ctx = kc_ref.shape[2]
    n_lat = kl_ref.shape[2]
    chunks = [(n_ctx, lambda: kc_ref[0, 0], lambda: vtc_ref[0, 0])]
    for j in range(n_lat // tk):
        chunks.append((tk, lambda j=j: kl_ref[0, 0, j * tk:(j + 1) * tk, :],
                       lambda j=j: vtl_ref[0, 0, :, j * tk:(j + 1) * tk]))

    def scores(j, qt, slot):
        rows, keys, _ = chunks[j]
        s_ref[slot, :rows, :] = _dot(keys(), qt)

    def load_q(t):
        return qt_ref[0, 0, :, pl.ds(pl.multiple_of(t * tq, tq), tq)]

    def q_tile(t, t_next, slot0):
        qt = load_q(t)
        m = l = acc = None
        for j, (rows, _, values_t) in enumerate(chunks):
            slot = (slot0 + j) % 2
            if j + 1 < len(chunks):
                scores(j + 1, qt, 1 - slot)
            else:
                scores(0, load_q(t_next), 1 - slot)
            s = s_ref[slot, :rows, :]
            m_chunk = jnp.max(s, axis=0, keepdims=True)
            m_new = m_chunk if m is None else jnp.maximum(m, m_chunk)
            p = jnp.exp2(s - m_new)
            l_chunk = jnp.sum(p, axis=0, keepdims=True)
            pv = _dot(values_t(), p.astype(BF16))
            if m is None:
                l, acc = l_chunk, pv
            else:
                alpha = jnp.exp2(m - m_new)
                l = alpha * l + l_chunk
                acc = alpha * acc + pv
            m = m_new
        o_ref[0, pl.ds(pl.multiple_of(t * tq, tq), tq), :] = (acc * (1.0 / l)).T.astype(o_ref.dtype)

    n_tiles = n_lat // tq
    scores(0, load_q(0), 0)

    def tile_pair(i, _):
        t = 2 * i
        q_tile(t, t + 1, 0)
        q_tile(t + 1, jnp.minimum(t + 2, n_tiles - 1), len(chunks) % 2)
        return 0

    lax.fori_loop(0, n_tiles // 2, tile_pair, 0)


def _attn_call(q_t, k_ctx, k_lat, vt_ctx, vt_lat, cast_weights, tq, tk):
    B, H, _, n = q_t.shape
    n_ctx = k_ctx.shape[2]
    steps_per_block = B * H // CAST_BLOCKS
    w_blk = lambda w: pl.BlockSpec((w.shape[0] // CAST_BLOCKS, w.shape[1]),
                                   lambda b, h: ((b * H + h) // steps_per_block, 0))
    outs = pl.pallas_call(
        functools.partial(_attn_kernel, tq=tq, tk=tk),
        grid=(B, H),
        in_specs=[pl.BlockSpec((1, 1, QK_HEAD_DIM, n), lambda b, h: (b, h, 0, 0)),
                  pl.BlockSpec((1, 1, n_ctx, QK_HEAD_DIM), lambda b, h: (b, h, 0, 0)),
                  pl.BlockSpec((1, 1, n, QK_HEAD_DIM), lambda b, h: (b, h, 0, 0)),
                  pl.BlockSpec((1, 1, V_HEAD_DIM, n_ctx), lambda b, h: (b, h, 0, 0)),
                  pl.BlockSpec((1, 1, V_HEAD_DIM, n), lambda b, h: (b, h, 0, 0))]
                 + [w_blk(w) for w in cast_weights],
        out_specs=[pl.BlockSpec((1, n, V_HEAD_DIM), lambda b, h: (b, 0, h))]
                  + [w_blk(w) for w in cast_weights],
        out_shape=[jax.ShapeDtypeStruct((B, n, H * V_HEAD_DIM), BF16)]
                  + [jax.ShapeDtypeStruct(w.shape, BF16) for w in cast_weights],
        scratch_shapes=[pltpu.VMEM((2, max(tk, n_ctx), tq), F32)],
        compiler_params=pltpu.CompilerParams(
            dimension_semantics=("arbitrary", "arbitrary"), vmem_limit_bytes=VMEM_LIMIT_BYTES),
        name="attn",
    )(q_t, k_ctx, k_lat, vt_ctx, vt_lat, *cast_weights)
    return outs[0], outs[1:]


_REF_ROWS = ((CHUNK // 2 - 1, CHUNK - 1), (CHUNK // 2, 0))
_CPB = HGRN_BLOCK // CHUNK
_FINISH_ROWS = 512


def _hgrn_kernel(vc_ref, kfc_ref, bfc_ref, kbc_ref, bbc_ref,
                 q_ref, v_ref, hg_ref, kf_ref, bf_ref, kb_ref, bb_ref, gon_ref,
                 o_ref, upd_ref, dec_ref, snap_ref, qd_ref, oin_ref):
    n_ctx = vc_ref.shape[2]
    n_lat = q_ref.shape[2]
    nc_ctx = n_ctx // CHUNK
    nc_lat = n_lat // CHUNK
    masks = _chunk_masks()

    def chunk_rows(x, r):
        return jnp.concatenate(
            [jnp.broadcast_to(x[c * CHUNK + r:c * CHUNK + r + 1], (CHUNK, x.shape[1])) for c in range(_CPB)],
            axis=0)

    def block_a(chunk0, rows, q, v, kks, bs):
        kds, decs, a, qds = [], [], None, []
        for d in range(2):
            r_ref, r_last = _REF_ROWS[d]
            kk, b = kks[d].astype(F32), bs[d]
            kds.append((kk * jnp.exp2(chunk_rows(b, r_last) - b)).astype(BF16))
            decs.append([jnp.exp2(b[c * CHUNK + r_last:c * CHUNK + r_last + 1]) for c in range(_CPB)])
            if q is not None:
                b_ref = chunk_rows(b, r_ref)
                qds.append((q * jnp.exp2(b)).astype(BF16))
                qa = (q * jnp.exp2(b - b_ref)).astype(BF16)
                ka = (kk * jnp.exp2(b_ref - b)).astype(BF16)
                a_d = jnp.where(masks[d], _dot_nt(qa, ka), 0.0)
                a = a_d if a is None else a + a_d
        kd = jnp.concatenate(kds, axis=1)
        for c in range(_CPB):
            rc = slice(c * CHUNK, (c + 1) * CHUNK)
            upd_ref[chunk0 + c] = _dot_tn(v[rc], kd[rc])
            dec_ref[chunk0 + c] = jnp.concatenate([decs[0][c], decs[1][c]], axis=1)
        if q is not None:
            qd_ref[rows, :] = jnp.concatenate(qds, axis=1)
            oin_ref[rows, :] = _dot(a.astype(BF16), v)

    for i in range(n_ctx // HGRN_BLOCK):
        r = slice(i * HGRN_BLOCK, (i + 1) * HGRN_BLOCK)
        block_a(i * _CPB, None, None, vc_ref[0, 0, r, :],
                (kfc_ref[0, 0, r, :], kbc_ref[0, 0, r, :]), (bfc_ref[0, 0, r, :], bbc_ref[0, 0, r, :]))

    def phase_a(i, _):
        r = pl.ds(pl.multiple_of(i * HGRN_BLOCK, HGRN_BLOCK), HGRN_BLOCK)
        block_a(nc_ctx + i * _CPB, r, q_ref[0, 0, r, :].astype(F32), v_ref[0, 0, r, :],
                (kf_ref[0, 0, r, :], kb_ref[0, 0, r, :]), (bf_ref[0, 0, r, :], bb_ref[0, 0, r, :]))
        return 0

    lax.fori_loop(0, n_lat // HGRN_BLOCK, phase_a, 0, unroll=8)

    def advance(st, cf, cb):
        dec = jnp.concatenate([dec_ref[cf][:, :HGRN_DIM], dec_ref[cb][:, HGRN_DIM:]], axis=1)
        upd = jnp.concatenate([upd_ref[cf][:, :HGRN_DIM], upd_ref[cb][:, HGRN_DIM:]], axis=1)
        return st * dec + upd

    st = jnp.zeros((HGRN_DIM, 2 * HGRN_DIM), F32)
    for i in range(nc_ctx):
        st = advance(st, i, nc_ctx - 1 - i)

    def phase_b(i, st):
        cf, cb = i, nc_lat - 1 - i
        sb = st.astype(BF16)
        snap_ref[cf, :, :HGRN_DIM] = sb[:, :HGRN_DIM]
        snap_ref[cb, :, HGRN_DIM:] = sb[:, HGRN_DIM:]
        return advance(st, nc_ctx + cf, nc_ctx + cb)

    lax.fori_loop(0, nc_lat, phase_b, st, unroll=4)

    cpf = _FINISH_ROWS // CHUNK
    def phase_c(i, _):
        r0 = pl.multiple_of(i * _FINISH_ROWS, _FINISH_ROWS)
        inter = [_dot_nt(qd_ref[pl.ds(r0 + c * CHUNK, CHUNK), :], snap_ref[i * cpf + c]) for c in range(cpf)]
        o = oin_ref[pl.ds(r0, _FINISH_ROWS), :] + jnp.concatenate(inter, axis=0)
        y = _rms(o) * gon_ref[...]
        gate = _silu(hg_ref[0, 0, pl.ds(r0, _FINISH_ROWS), :].astype(F32))
        o_ref[0, pl.ds(r0, _FINISH_ROWS), :] = (y * gate).astype(o_ref.dtype)
        return 0

    lax.fori_loop(0, n_lat // _FINISH_ROWS, phase_c, 0, unroll=8)


def _hgrn_call(ctx_ops, lat_ops, g_on):
    B, H, n, _ = lat_ops[0].shape
    n_ctx = ctx_ops[0].shape[2]
    n_chunks = (n + n_ctx) // CHUNK
    ctx_blk = pl.BlockSpec((1, 1, n_ctx, 128), lambda b, h: (b, h, 0, 0))
    lat_blk = pl.BlockSpec((1, 1, n, 128), lambda b, h: (b, h, 0, 0))
    return pl.pallas_call(
        _hgrn_kernel,
        grid=(B, H),
        in_specs=[ctx_blk] * len(ctx_ops) + [lat_blk] * len(lat_ops)
                 + [pl.BlockSpec((1, 128), lambda b, h: (0, 0))],
        out_specs=pl.BlockSpec((1, n, 128), lambda b, h: (b, 0, h)),
        out_shape=jax.ShapeDtypeStruct((B, n, H * 128), BF16),
        scratch_shapes=[pltpu.VMEM((n_chunks, HGRN_DIM, 2 * HGRN_DIM), F32),
                        pltpu.VMEM((n_chunks, 1, 2 * HGRN_DIM), F32),
                        pltpu.VMEM((n // CHUNK, HGRN_DIM, 2 * HGRN_DIM), BF16),
                        pltpu.VMEM((n, 2 * HGRN_DIM), BF16),
                        pltpu.VMEM((n, HGRN_DIM), F32)],
        compiler_params=pltpu.CompilerParams(
            dimension_semantics=("parallel", "parallel"), vmem_limit_bytes=VMEM_LIMIT_BYTES),
        name="hgrn",
    )(*ctx_ops, *lat_ops, g_on)


def _out_ffn_kernel(x_ref, om_ref, oh_ref, mod_ref, gffn_ref, wout_ref, wg_ref, wu_ref, wd_ref,
                    gfin_ref, o_ref):
    x = x_ref[0]
    mix = _dot(om_ref[0], wout_ref[:HEADS * V_HEAD_DIM, :]) + _dot(oh_ref[0], wout_ref[HEADS * V_HEAD_DIM:, :])
    x1 = x + mod_ref[0, 2:3, :] * mix
    gain = gffn_ref[...] * (1.0 + mod_ref[0, 4:5, :])
    h2 = (_rms(x1) * gain + mod_ref[0, 3:4, :]).astype(BF16)
    acc = jnp.zeros((x.shape[0], D_MODEL), F32)
    for j in range(D_FF // FF_CHUNK):
        c0 = j * FF_CHUNK
        g = _dot(h2, wg_ref[:, c0:c0 + FF_CHUNK])
        u = _dot(h2, wu_ref[:, c0:c0 + FF_CHUNK])
        a = (_silu(g) * u).astype(BF16)
        acc = acc + _dot(a, wd_ref[c0:c0 + FF_CHUNK, :])
    x2 = x1 + mod_ref[0, 5:6, :] * acc
    o_ref[0] = _rms(x2) * gfin_ref[...]


def _out_ffn_call(x, o_mla, o_hgrn, mod_rows, g_ffn, w_out, w_gate, w_up, w_down, g_final, tm):
    B, n, _ = x.shape
    const = lambda b, i: (0, 0)
    resident = lambda shape: pl.BlockSpec(shape, const, pipeline_mode=pl.Buffered(1))
    return pl.pallas_call(
        _out_ffn_kernel,
        grid=(B, n // tm),
        in_specs=[pl.BlockSpec((1, tm, D_MODEL), lambda b, i: (b, i, 0)),
                  pl.BlockSpec((1, tm, 512), lambda b, i: (b, i, 0)),
                  pl.BlockSpec((1, tm, 512), lambda b, i: (b, i, 0)),
                  pl.BlockSpec((1, 8, D_MODEL), lambda b, i: (b, 0, 0)),
                  pl.BlockSpec((1, D_MODEL), const),
                  resident((D_MODEL, D_MODEL)),
                  resident((D_MODEL, D_FF)),
                  resident((D_MODEL, D_FF)),
                  resident((D_FF, D_MODEL)),
                  pl.BlockSpec((1, D_MODEL), const)],
        out_specs=pl.BlockSpec((1, tm, D_MODEL), lambda b, i: (b, i, 0)),
        out_shape=jax.ShapeDtypeStruct((B, n, D_MODEL), F32),
        compiler_params=pltpu.CompilerParams(
            dimension_semantics=("parallel", "parallel"), vmem_limit_bytes=VMEM_LIMIT_BYTES),
        name="out_ffn",
    )(x, o_mla, o_hgrn, mod_rows, g_ffn, w_out, w_gate, w_up, w_down, g_final)


_HALF_SWAP = np.concatenate([np.arange(16, 32), np.arange(0, 16), np.arange(48, 64), np.arange(32, 48)])


def _prep_weights(w_uq, w_ukv):
    uq = w_uq.reshape(Q_LORA_RANK, HEADS, QK_HEAD_DIM)
    rope = uq[:, :, QK_NOPE_DIM:]
    w_uq_p = jnp.concatenate([uq[:, :, :QK_NOPE_DIM].reshape(Q_LORA_RANK, -1),
                              rope.reshape(Q_LORA_RANK, -1),
                              rope[:, :, _HALF_SWAP].reshape(Q_LORA_RANK, -1)], axis=1).astype(BF16)
    ukv = w_ukv.reshape(KV_LORA_RANK, HEADS, QK_NOPE_DIM + V_HEAD_DIM)
    w_ukv_p = jnp.concatenate([ukv[:, :, :QK_NOPE_DIM].reshape(KV_LORA_RANK, -1),
                               ukv[:, :, QK_NOPE_DIM:].reshape(KV_LORA_RANK, -1)], axis=1).astype(BF16)
    return w_uq_p, w_ukv_p


def _rope_tables(n):
    rows = n // GRID_W
    row = jnp.broadcast_to(jnp.arange(rows)[:, None], (rows, GRID_W)).reshape(n)
    col = jnp.broadcast_to(jnp.arange(GRID_W)[None, :], (rows, GRID_W)).reshape(n)
    axis_dim = QK_ROPE_DIM // 2
    inv = 1.0 / (ROPE_THETA ** (jnp.arange(0, axis_dim, 2, dtype=F32) / axis_dim))
    ang_r = row.astype(F32)[:, None] * inv
    ang_c = col.astype(F32)[:, None] * inv
    cos = jnp.concatenate([jnp.cos(ang_r)] * 2 + [jnp.cos(ang_c)] * 2, axis=-1)
    sin = jnp.concatenate([-jnp.sin(ang_r), jnp.sin(ang_r), -jnp.sin(ang_c), jnp.sin(ang_c)], axis=-1)
    return cos, sin


def kernel(x, c, ctx, c_ctx, w_mod, b_mod, g_norm_mix, g_norm_ffn, w_in, g_q_norm, w_uq, g_kv_norm,
           w_ukv, lb_fwd, lb_bwd, g_hgrn_norm, w_out, w_gate, w_up, w_down, g_final):
    B, N, D = x.shape
    L = ctx.shape[1]
    layer = 0

    cc = jnp.concatenate([c, c_ctx[None, :], jnp.zeros((7, D), F32)], axis=0)
    mod, *w_in_parts = _mod_call(cc, w_mod[layer], b_mod[layer][None, :], w_in[layer].T)
    pad = jnp.zeros((B, 2, D), F32)
    mod_lat = jnp.concatenate([mod[:B].reshape(B, 6, D), pad], axis=1)
    mod_ctx = jnp.concatenate([jnp.broadcast_to(mod[B].reshape(1, 6, D), (B, 6, D)), pad], axis=1)

    w_uq_p, w_ukv_p = _prep_weights(w_uq[layer], w_ukv[layer])
    cos, sin = _rope_tables(N)
    row2 = lambda v: v.reshape(1, -1)
    proj_args = (row2(g_norm_mix[layer]), *w_in_parts, row2(g_q_norm[layer]), w_uq_p,
                 row2(g_kv_norm[layer]), w_ukv_p)

    q_l, k_l, v_l, hq_l, hi_l, hg_l, *decay_l = _in_proj_call(
        x, mod_lat, *proj_args, cos, sin, lb_fwd, lb_bwd, tm=PROJ_ROWS, queries=True)
    k_c, v_c, hi_c, *decay_c = _in_proj_call(
        ctx, mod_ctx, *proj_args, jnp.ones((L, QK_ROPE_DIM), F32), jnp.zeros((L, QK_ROPE_DIM), F32), lb_fwd, lb_bwd,
        tm=L, queries=False)

    o_mla, (w_out16, w_gate16, w_up16, w_down16) = _attn_call(
        q_l, k_c, k_l, v_c, v_l, (w_out[layer], w_gate[layer], w_up[layer], w_down[layer]),
        tq=ATTN_Q_TILE, tk=ATTN_K_TILE)
    o_hgrn = _hgrn_call((hi_c, *decay_c), (hq_l, hi_l, hg_l, *decay_l), row2(g_hgrn_norm[layer]))

    return _out_ffn_call(x, o_mla, o_hgrn, mod_lat, row2(g_norm_ffn[layer]),
                         w_out16, w_gate16, w_up16, w_down16,
                         row2(g_final), tm=FFN_ROWS)
```

```python
import functools

import numpy as np
import jax
import jax.numpy as jnp
from jax import lax
from jax.experimental import pallas as pl
from jax.experimental.pallas import tpu as pltpu

D_MODEL = 1024
GRID_W = 64
EPS = 1e-6
ROPE_THETA = 10000.0
V_HEAD_DIM = 128
QK_NOPE_DIM = 128
QK_ROPE_DIM = 64
Q_LORA_RANK = 256
KV_LORA_RANK = 256
HEADS = 4
QK_HEAD_DIM = QK_NOPE_DIM + QK_ROPE_DIM
HGRN_DIM = 128
HGRN_WIDTH = HEADS * HGRN_DIM
CHUNK = 64
IN_SIZES = (Q_LORA_RANK, KV_LORA_RANK, QK_ROPE_DIM,
            HGRN_WIDTH, HGRN_WIDTH, HGRN_WIDTH, HGRN_WIDTH, HGRN_WIDTH)
D_FF = 2816
FF_CHUNK = 256
HGRN_BLOCK = 256
PROJ_ROWS = 1024
ATTN_Q_TILE = 512
ATTN_K_TILE = 512
FFN_ROWS = 512
CAST_BLOCKS = 16
VMEM_LIMIT_BYTES = 56 * 1024 * 1024
Q_SCALE = float(np.log2(np.e) / np.sqrt(QK_HEAD_DIM))

BF16 = jnp.bfloat16
F32 = jnp.float32


def _dot(a, b):
    return jnp.dot(a, b, preferred_element_type=F32)


def _dot_nt(a, b):
    return lax.dot_general(a, b, (((1,), (1,)), ((), ())), preferred_element_type=F32)


def _dot_tn(a, b):
    return lax.dot_general(a, b, (((0,), (0,)), ((), ())), preferred_element_type=F32)


def _silu(x):
    return x * jax.nn.sigmoid(x)


def _rms(x):
    return x * lax.rsqrt(jnp.mean(x * x, axis=-1, keepdims=True) + EPS)


_MOD_STEPS = 8
_HG_COLS = 512


def _mod_kernel(c_ref, w_ref, b_ref, wt_ref, o_ref, wlat_ref, whg_ref, wkp_ref):
    a = _silu(c_ref[...]).astype(BF16)
    o_ref[...] = _dot(a, w_ref[...].astype(BF16)) + b_ref[...]

    j = pl.program_id(0)
    offs = np.cumsum((0,) + IN_SIZES)
    n_hg = (offs[-1] - offs[3]) // _HG_COLS

    @pl.when(j < n_hg)
    def _():
        r0 = pl.multiple_of(offs[3] + j * _HG_COLS, CHUNK)
        whg_ref[...] = wt_ref[pl.ds(r0, _HG_COLS), :].T.astype(BF16)

    @pl.when(j == n_hg)
    def _():
        wlat_ref[...] = wt_ref[:offs[2], :].T.astype(BF16)

    @pl.when(j == n_hg + 1)
    def _():
        kpe = wt_ref[offs[2]:offs[3], :]
        quarter = QK_ROPE_DIM // 4
        swapped = [kpe[q * quarter:(q + 1) * quarter] for q in (1, 0, 3, 2)]
        wkp_ref[...] = jnp.concatenate([kpe] + swapped, axis=0).T.astype(BF16)


def _mod_call(cc, w_mod, b_mod, w_in_t):
    rows = cc.shape[0]
    cols = w_mod.shape[1]
    tn = cols // _MOD_STEPS
    n_lat, n_hg, n_kp = Q_LORA_RANK + KV_LORA_RANK, 5 * HGRN_WIDTH, 2 * QK_ROPE_DIM
    assert n_hg // _HG_COLS + 2 <= _MOD_STEPS
    const = lambda j: (0, 0)
    return pl.pallas_call(
        _mod_kernel,
        grid=(_MOD_STEPS,),
        in_specs=[pl.BlockSpec((rows, D_MODEL), const),
                  pl.BlockSpec((D_MODEL, tn), lambda j: (0, j)),
                  pl.BlockSpec((1, tn), lambda j: (0, j)),
                  pl.BlockSpec(w_in_t.shape, const, pipeline_mode=pl.Buffered(1))],
        out_specs=[pl.BlockSpec((rows, tn), lambda j: (0, j)),
                   pl.BlockSpec((D_MODEL, n_lat), const),
                   pl.BlockSpec((D_MODEL, _HG_COLS), lambda j: (0, jnp.minimum(j, n_hg // _HG_COLS - 1))),
                   pl.BlockSpec((D_MODEL, n_kp), const)],
        out_shape=[jax.ShapeDtypeStruct((rows, cols), F32)]
                  + [jax.ShapeDtypeStruct((D_MODEL, w), BF16) for w in (n_lat, n_hg, n_kp)],
        compiler_params=pltpu.CompilerParams(dimension_semantics=("arbitrary",),
                                             vmem_limit_bytes=VMEM_LIMIT_BYTES),
        name="mod",
    )(cc, w_mod, b_mod, w_in_t)


def _chunk_cumsum(g, reverse):
    rows, w = g.shape
    x = g.reshape(rows // 8, 8, w)
    sub = lax.broadcasted_iota(jnp.int32, (1, 8, w), 1)
    for s in (1, 2, 4):
        if reverse:
            x = x + jnp.where(sub < 8 - s, pltpu.roll(x, 8 - s, axis=1), 0.0)
        else:
            x = x + jnp.where(sub >= s, pltpu.roll(x, s, axis=1), 0.0)
    groups = CHUNK // 8
    x = x.reshape(rows // CHUNK, groups, 8, w)
    edge = 0 if reverse else 7
    outs = [None] * groups
    carry = None
    for j in (range(groups - 1, -1, -1) if reverse else range(groups)):
        blk = x[:, j] if carry is None else x[:, j] + carry
        outs[j] = blk
        carry = jnp.broadcast_to(blk[:, edge:edge + 1, :], blk.shape)
    return jnp.stack(outs, axis=1).reshape(rows, w)


def _chunk_masks():
    row = lax.broadcasted_iota(jnp.int32, (HGRN_BLOCK, HGRN_BLOCK), 0)
    col = lax.broadcasted_iota(jnp.int32, (HGRN_BLOCK, HGRN_BLOCK), 1)
    same_chunk = (row // CHUNK) == (col // CHUNK)
    return same_chunk & (col <= row), same_chunk & (col >= row)


def _in_proj_kernel(x_ref, mod_ref, gmix_ref, wlat_ref, whg_ref, wkp_ref, gq_ref, wuq_ref, gkv_ref, wukv_ref,
                    cos_ref, sin_ref, lbf_ref, lbb_ref, *out_refs, queries):
    if queries:
        qt_ref, k_ref, vt_ref, hq_ref, hi_ref, hg_ref, kf_ref, bf_ref, kb_ref, bb_ref = out_refs
        hgrn_outs = ((0, hq_ref), (1, hi_ref), (2, hg_ref))
    else:
        k_ref, vt_ref, hi_ref, kf_ref, bf_ref, kb_ref, bb_ref = out_refs
        hgrn_outs = ((1, hi_ref),)
    x = x_ref[0]
    shift = mod_ref[0, 0:1, :]
    gain = gmix_ref[...] * (1.0 + mod_ref[0, 1:2, :])
    h = (_rms(x) * gain + shift).astype(BF16)

    cos1 = cos_ref[...]
    sin1 = sin_ref[...]

    lat = _dot(h, wlat_ref[...])
    ckv = (_rms(lat[:, Q_LORA_RANK:]) * gkv_ref[...]).astype(BF16)

    if queries:
        q = _dot((_rms(lat[:, :Q_LORA_RANK]) * gq_ref[...]).astype(BF16), wuq_ref[...])
        cos4 = jnp.concatenate([cos1] * HEADS, axis=1)
        sin4 = jnp.concatenate([sin1] * HEADS, axis=1)
        q_nope_t = (q[:, :512] * Q_SCALE).T
        q_rope_t = ((q[:, 512:768] * cos4 + q[:, 768:1024] * sin4) * Q_SCALE).T
        for hd in range(HEADS):
            qt_ref[0, hd, :QK_NOPE_DIM, :] = q_nope_t[hd * 128:(hd + 1) * 128].astype(BF16)
            qt_ref[0, hd, QK_NOPE_DIM:, :] = q_rope_t[hd * 64:(hd + 1) * 64].astype(BF16)
    kv = _dot(ckv, wukv_ref[...])
    kp = _dot(h, wkp_ref[...])
    k_rope = (kp[:, :QK_ROPE_DIM] * cos1 + kp[:, QK_ROPE_DIM:] * sin1).astype(BF16)
    v_t = kv[:, 512:].T
    for hd in range(HEADS):
        k_ref[0, hd, :, :QK_NOPE_DIM] = kv[:, hd * 128:(hd + 1) * 128].astype(BF16)
        k_ref[0, hd, :, QK_NOPE_DIM:] = k_rope
        vt_ref[0, hd] = v_t[hd * 128:(hd + 1) * 128].astype(BF16)

    for d, (lb_ref, kk_ref, b_ref) in enumerate(((lbf_ref, kf_ref, bf_ref), (lbb_ref, kb_ref, bb_ref))):
        c0 = (3 + d) * HGRN_WIDTH
        t = lb_ref[...]
        e = jnp.exp(t - jnp.max(t, axis=0, keepdims=True))
        lb = e[0:1] / jnp.sum(e, axis=0, keepdims=True)
        f = lb + (1.0 - lb) * jax.nn.sigmoid(_dot(h, whg_ref[:, c0:c0 + HGRN_WIDTH]))
        kk = (1.0 - f).astype(BF16)
        b = _chunk_cumsum(jnp.log2(f), reverse=(d == 1))
        for hd in range(HEADS):
            b_ref[0, hd] = b[:, hd * 128:(hd + 1) * 128]
            kk_ref[0, hd] = kk[:, hd * 128:(hd + 1) * 128]

    for j, o_ref in hgrn_outs:
        t = _dot(h, whg_ref[:, j * HGRN_WIDTH:(j + 1) * HGRN_WIDTH])
        for hd in range(HEADS):
            o_ref[0, hd] = t[:, hd * 128:(hd + 1) * 128].astype(o_ref.dtype)


def _in_proj_call(x, mod_rows, g_mix, w_lat, w_hgrn, w_kpe, g_q, w_uq, g_kv, w_ukv, cos, sin,
                  lb_fwd, lb_bwd, tm, queries):
    B, n, _ = x.shape
    const = lambda b, i: (0, 0)
    head_blk = lambda w: pl.BlockSpec((1, HEADS, tm, w), lambda b, i: (b, 0, i, 0))
    hshape = lambda w, dt: jax.ShapeDtypeStruct((B, HEADS, n, w), dt)
    head_blk_t = lambda w: pl.BlockSpec((1, HEADS, w, tm), lambda b, i: (b, 0, 0, i))
    hshape_t = lambda w: jax.ShapeDtypeStruct((B, HEADS, w, n), BF16)
    outs = [(head_blk(QK_HEAD_DIM), hshape(QK_HEAD_DIM, BF16)), (head_blk_t(V_HEAD_DIM), hshape_t(V_HEAD_DIM)),
            (head_blk(128), hshape(128, BF16)),
            (head_blk(128), hshape(128, BF16)), (head_blk(128), hshape(128, F32)),
            (head_blk(128), hshape(128, BF16)), (head_blk(128), hshape(128, F32))]
    if queries:
        outs = ([(head_blk_t(QK_HEAD_DIM), hshape_t(QK_HEAD_DIM))] + outs[:2]
                + [(head_blk(128), hshape(128, BF16)), outs[2], (head_blk(128), hshape(128, BF16))] + outs[3:])
    return pl.pallas_call(
        functools.partial(_in_proj_kernel, queries=queries),
        grid=(B, n // tm),
        in_specs=[pl.BlockSpec((1, tm, D_MODEL), lambda b, i: (b, i, 0)),
                  pl.BlockSpec((1, 8, D_MODEL), lambda b, i: (b, 0, 0)),
                  pl.BlockSpec((1, D_MODEL), const),
                  pl.BlockSpec(w_lat.shape, const),
                  pl.BlockSpec(w_hgrn.shape, const),
                  pl.BlockSpec(w_kpe.shape, const),
                  pl.BlockSpec((1, Q_LORA_RANK), const),
                  pl.BlockSpec((Q_LORA_RANK, 1024), const),
                  pl.BlockSpec((1, KV_LORA_RANK), const),
                  pl.BlockSpec((KV_LORA_RANK, 1024), const),
                  pl.BlockSpec((tm, QK_ROPE_DIM), lambda b, i: (i, 0)),
                  pl.BlockSpec((tm, QK_ROPE_DIM), lambda b, i: (i, 0)),
                  pl.BlockSpec(lb_fwd.shape, const),
                  pl.BlockSpec(lb_bwd.shape, const)],
        out_specs=[spec for spec, _ in outs],
        out_shape=[shape for _, shape in outs],
        compiler_params=pltpu.CompilerParams(
            dimension_semantics=("parallel", "parallel"), vmem_limit_bytes=VMEM_LIMIT_BYTES),
        name="in_proj",
    )(x, mod_rows, g_mix, w_lat, w_hgrn, w_kpe, g_q, w_uq, g_kv, w_ukv, cos, sin, lb_fwd, lb_bwd)


def _attn_kernel(qt_ref, kc_ref, kl_ref, vtc_ref, vtl_ref, *rest, tq, tk):
    n_cast = (len(rest) - 2) // 2
    o_ref, s_ref = rest[n_cast], rest[-1]
    for w_ref, w16_ref in zip(rest[:n_cast], rest[n_cast + 1:-1]):
        w16_ref[...] = w_ref[...].astype(BF16)

    n_ctx = kc_ref.shape[2]
    n_lat = kl_ref.shape[2]
    chunks = [(n_ctx, lambda: kc_ref[0, 0], lambda: vtc_ref[0, 0])]
    for j in range(n_lat // tk):
        chunks.append((tk, lambda j=j: kl_ref[0, 0, j * tk:(j + 1) * tk, :],
                       lambda j=j: vtl_ref[0, 0, :, j * tk:(j + 1) * tk]))

    def scores(j, qt, slot):
        rows, keys, _ = chunks[j]
        s_ref[slot, :rows, :] = _dot(keys(), qt)

    def load_q(t):
        return qt_ref[0, 0, :, pl.ds(pl.multiple_of(t * tq, tq), tq)]

    def q_tile(t, t_next, slot0):
        qt = load_q(t)
        m = l = acc = None
        for j, (rows, _, values_t) in enumerate(chunks):
            slot = (slot0 + j) % 2
            if j + 1 < len(chunks):
                scores(j + 1, qt, 1 - slot)
            else:
                scores(0, load_q(t_next), 1 - slot)
            s = s_ref[slot, :rows, :]
            m_chunk = jnp.max(s, axis=0, keepdims=True)
            m_new = m_chunk if m is None else jnp.maximum(m, m_chunk)
            p = jnp.exp2(s - m_new)
            l_chunk = jnp.sum(p, axis=0, keepdims=True)
            pv = _dot(values_t(), p.astype(BF16))
            if m is None:
                l, acc = l_chunk, pv
            else:
                alpha = jnp.exp2(m - m_new)
                l = alpha * l + l_chunk
                acc = alpha * acc + pv
            m = m_new
        o_ref[0, pl.ds(pl.multiple_of(t * tq, tq), tq), :] = (acc * (1.0 / l)).T.astype(o_ref.dtype)

    n_tiles = n_lat // tq
    scores(0, load_q(0), 0)

    def tile_pair(i, _):
        t = 2 * i
        q_tile(t, t + 1, 0)
        q_tile(t + 1, jnp.minimum(t + 2, n_tiles - 1), len(chunks) % 2)
        return 0

    lax.fori_loop(0, n_tiles // 2, tile_pair, 0)


def _attn_call(q_t, k_ctx, k_lat, vt_ctx, vt_lat, cast_weights, tq, tk):
    B, H, _, n = q_t.shape
    n_ctx = k_ctx.shape[2]
    steps_per_block = B * H // CAST_BLOCKS
    w_blk = lambda w: pl.BlockSpec((w.shape[0] // CAST_BLOCKS, w.shape[1]),
                                   lambda b, h: ((b * H + h) // steps_per_block, 0))
    outs = pl.pallas_call(
        functools.partial(_attn_kernel, tq=tq, tk=tk),
        grid=(B, H),
        in_specs=[pl.BlockSpec((1, 1, QK_HEAD_DIM, n), lambda b, h: (b, h, 0, 0)),
                  pl.BlockSpec((1, 1, n_ctx, QK_HEAD_DIM), lambda b, h: (b, h, 0, 0)),
                  pl.BlockSpec((1, 1, n, QK_HEAD_DIM), lambda b, h: (b, h, 0, 0)),
                  pl.BlockSpec((1, 1, V_HEAD_DIM, n_ctx), lambda b, h: (b, h, 0, 0)),
                  pl.BlockSpec((1, 1, V_HEAD_DIM, n), lambda b, h: (b, h, 0, 0))]
                 + [w_blk(w) for w in cast_weights],
        out_specs=[pl.BlockSpec((1, n, V_HEAD_DIM), lambda b, h: (b, 0, h))]
                  + [w_blk(w) for w in cast_weights],
        out_shape=[jax.ShapeDtypeStruct((B, n, H * V_HEAD_DIM), BF16)]
                  + [jax.ShapeDtypeStruct(w.shape, BF16) for w in cast_weights],
        scratch_shapes=[pltpu.VMEM((2, max(tk, n_ctx), tq), F32)],
        compiler_params=pltpu.CompilerParams(
            dimension_semantics=("arbitrary", "arbitrary"), vmem_limit_bytes=VMEM_LIMIT_BYTES),
        name="attn",
    )(q_t, k_ctx, k_lat, vt_ctx, vt_lat, *cast_weights)
    return outs[0], outs[1:]


_REF_ROWS = ((CHUNK // 2 - 1, CHUNK - 1), (CHUNK // 2, 0))
_CPB = HGRN_BLOCK // CHUNK
_FINISH_ROWS = 512


def _hgrn_kernel(vc_ref, kfc_ref, bfc_ref, kbc_ref, bbc_ref,
                 q_ref, v_ref, hg_ref, kf_ref, bf_ref, kb_ref, bb_ref, gon_ref,
                 o_ref, upd_ref, dec_ref, snap_ref, qd_ref, oin_ref):
    n_ctx = vc_ref.shape[2]
    n_lat = q_ref.shape[2]
    nc_ctx = n_ctx // CHUNK
    nc_lat = n_lat // CHUNK
    masks = _chunk_masks()

    def chunk_rows(x, r):
        return jnp.concatenate(
            [jnp.broadcast_to(x[c * CHUNK + r:c * CHUNK + r + 1], (CHUNK, x.shape[1])) for c in range(_CPB)],
            axis=0)

    def block_a(chunk0, rows, q, v, kks, bs):
        kds, decs, a, qds = [], [], None, []
        for d in range(2):
            r_ref, r_last = _REF_ROWS[d]
            kk, b = kks[d].astype(F32), bs[d]
            kds.append((kk * jnp.exp2(chunk_rows(b, r_last) - b)).astype(BF16))
            decs.append([jnp.exp2(b[c * CHUNK + r_last:c * CHUNK + r_last + 1]) for c in range(_CPB)])
            if q is not None:
                b_ref = chunk_rows(b, r_ref)
                qds.append((q * jnp.exp2(b)).astype(BF16))
                qa = (q * jnp.exp2(b - b_ref)).astype(BF16)
                ka = (kk * jnp.exp2(b_ref - b)).astype(BF16)
                a_d = jnp.where(masks[d], _dot_nt(qa, ka), 0.0)
                a = a_d if a is None else a + a_d
        kd = jnp.concatenate(kds, axis=1)
        for c in range(_CPB):
            rc = slice(c * CHUNK, (c + 1) * CHUNK)
            upd_ref[chunk0 + c] = _dot_tn(v[rc], kd[rc])
            dec_ref[chunk0 + c] = jnp.concatenate([decs[0][c], decs[1][c]], axis=1)
        if q is not None:
            qd_ref[rows, :] = jnp.concatenate(qds, axis=1)
            oin_ref[rows, :] = _dot(a.astype(BF16), v)

    for i in range(n_ctx // HGRN_BLOCK):
        r = slice(i * HGRN_BLOCK, (i + 1) * HGRN_BLOCK)
        block_a(i * _CPB, None, None, vc_ref[0, 0, r, :],
                (kfc_ref[0, 0, r, :], kbc_ref[0, 0, r, :]), (bfc_ref[0, 0, r, :], bbc_ref[0, 0, r, :]))

    def phase_a(i, _):
        r = pl.ds(pl.multiple_of(i * HGRN_BLOCK, HGRN_BLOCK), HGRN_BLOCK)
        block_a(nc_ctx + i * _CPB, r, q_ref[0, 0, r, :].astype(F32), v_ref[0, 0, r, :],
                (kf_ref[0, 0, r, :], kb_ref[0, 0, r, :]), (bf_ref[0, 0, r, :], bb_ref[0, 0, r, :]))
        return 0

    lax.fori_loop(0, n_lat // HGRN_BLOCK, phase_a, 0, unroll=8)

    def advance(st, cf, cb):
        dec = jnp.concatenate([dec_ref[cf][:, :HGRN_DIM], dec_ref[cb][:, HGRN_DIM:]], axis=1)
        upd = jnp.concatenate([upd_ref[cf][:, :HGRN_DIM], upd_ref[cb][:, HGRN_DIM:]], axis=1)
        return st * dec + upd

    st = jnp.zeros((HGRN_DIM, 2 * HGRN_DIM), F32)
    for i in range(nc_ctx):
        st = advance(st, i, nc_ctx - 1 - i)

    def phase_b(i, st):
        cf, cb = i, nc_lat - 1 - i
        sb = st.astype(BF16)
        snap_ref[cf, :, :HGRN_DIM] = sb[:, :HGRN_DIM]
        snap_ref[cb, :, HGRN_DIM:] = sb[:, HGRN_DIM:]
        return advance(st, nc_ctx + cf, nc_ctx + cb)

    lax.fori_loop(0, nc_lat, phase_b, st, unroll=4)

    cpf = _FINISH_ROWS // CHUNK
    def phase_c(i, _):
        r0 = pl.multiple_of(i * _FINISH_ROWS, _FINISH_ROWS)
        inter = [_dot_nt(qd_ref[pl.ds(r0 + c * CHUNK, CHUNK), :], snap_ref[i * cpf + c]) for c in range(cpf)]
        o = oin_ref[pl.ds(r0, _FINISH_ROWS), :] + jnp.concatenate(inter, axis=0)
        y = _rms(o) * gon_ref[...]
        gate = _silu(hg_ref[0, 0, pl.ds(r0, _FINISH_ROWS), :].astype(F32))
        o_ref[0, pl.ds(r0, _FINISH_ROWS), :] = (y * gate).astype(o_ref.dtype)
        return 0

    lax.fori_loop(0, n_lat // _FINISH_ROWS, phase_c, 0, unroll=8)


def _hgrn_call(ctx_ops, lat_ops, g_on):
    B, H, n, _ = lat_ops[0].shape
    n_ctx = ctx_ops[0].shape[2]
    n_chunks = (n + n_ctx) // CHUNK
    ctx_blk = pl.BlockSpec((1, 1, n_ctx, 128), lambda b, h: (b, h, 0, 0))
    lat_blk = pl.BlockSpec((1, 1, n, 128), lambda b, h: (b, h, 0, 0))
    return pl.pallas_call(
        _hgrn_kernel,
        grid=(B, H),
        in_specs=[ctx_blk] * len(ctx_ops) + [lat_blk] * len(lat_ops)
                 + [pl.BlockSpec((1, 128), lambda b, h: (0, 0))],
        out_specs=pl.BlockSpec((1, n, 128), lambda b, h: (b, 0, h)),
        out_shape=jax.ShapeDtypeStruct((B, n, H * 128), BF16),
        scratch_shapes=[pltpu.VMEM((n_chunks, HGRN_DIM, 2 * HGRN_DIM), F32),
                        pltpu.VMEM((n_chunks, 1, 2 * HGRN_DIM), F32),
                        pltpu.VMEM((n // CHUNK, HGRN_DIM, 2 * HGRN_DIM), BF16),
                        pltpu.VMEM((n, 2 * HGRN_DIM), BF16),
                        pltpu.VMEM((n, HGRN_DIM), F32)],
        compiler_params=pltpu.CompilerParams(
            dimension_semantics=("parallel", "parallel"), vmem_limit_bytes=VMEM_LIMIT_BYTES),
        name="hgrn",
    )(*ctx_ops, *lat_ops, g_on)


def _out_ffn_kernel(x_ref, om_ref, oh_ref, mod_ref, gffn_ref, wout_ref, wg_ref, wu_ref, wd_ref,
                    gfin_ref, o_ref):
    x = x_ref[0]
    mix = _dot(om_ref[0], wout_ref[:HEADS * V_HEAD_DIM, :]) + _dot(oh_ref[0], wout_ref[HEADS * V_HEAD_DIM:, :])
    x1 = x + mod_ref[0, 2:3, :] * mix
    gain = gffn_ref[...] * (1.0 + mod_ref[0, 4:5, :])
    h2 = (_rms(x1) * gain + mod_ref[0, 3:4, :]).astype(BF16)
    acc = jnp.zeros((x.shape[0], D_MODEL), F32)
    for j in range(D_FF // FF_CHUNK):
        c0 = j * FF_CHUNK
        g = _dot(h2, wg_ref[:, c0:c0 + FF_CHUNK])
        u = _dot(h2, wu_ref[:, c0:c0 + FF_CHUNK])
        a = (_silu(g) * u).astype(BF16)
        acc = acc + _dot(a, wd_ref[c0:c0 + FF_CHUNK, :])
    x2 = x1 + mod_ref[0, 5:6, :] * acc
    o_ref[0] = _rms(x2) * gfin_ref[...]


def _out_ffn_call(x, o_mla, o_hgrn, mod_rows, g_ffn, w_out, w_gate, w_up, w_down, g_final, tm):
    B, n, _ = x.shape
    const = lambda b, i: (0, 0)
    resident = lambda shape: pl.BlockSpec(shape, const, pipeline_mode=pl.Buffered(1))
    return pl.pallas_call(
        _out_ffn_kernel,
        grid=(B, n // tm),
        in_specs=[pl.BlockSpec((1, tm, D_MODEL), lambda b, i: (b, i, 0)),
                  pl.BlockSpec((1, tm, 512), lambda b, i: (b, i, 0)),
                  pl.BlockSpec((1, tm, 512), lambda b, i: (b, i, 0)),
                  pl.BlockSpec((1, 8, D_MODEL), lambda b, i: (b, 0, 0)),
                  pl.BlockSpec((1, D_MODEL), const),
                  resident((D_MODEL, D_MODEL)),
                  resident((D_MODEL, D_FF)),
                  resident((D_MODEL, D_FF)),
                  resident((D_FF, D_MODEL)),
                  pl.BlockSpec((1, D_MODEL), const)],
        out_specs=pl.BlockSpec((1, tm, D_MODEL), lambda b, i: (b, i, 0)),
        out_shape=jax.ShapeDtypeStruct((B, n, D_MODEL), F32),
        compiler_params=pltpu.CompilerParams(
            dimension_semantics=("parallel", "parallel"), vmem_limit_bytes=VMEM_LIMIT_BYTES),
        name="out_ffn",
    )(x, o_mla, o_hgrn, mod_rows, g_ffn, w_out, w_gate, w_up, w_down, g_final)


_HALF_SWAP = np.concatenate([np.arange(16, 32), np.arange(0, 16), np.arange(48, 64), np.arange(32, 48)])


def _prep_weights(w_uq, w_ukv):
    uq = w_uq.reshape(Q_LORA_RANK, HEADS, QK_HEAD_DIM)
    rope = uq[:, :, QK_NOPE_DIM:]
    w_uq_p = jnp.concatenate([uq[:, :, :QK_NOPE_DIM].reshape(Q_LORA_RANK, -1),
                              rope.reshape(Q_LORA_RANK, -1),
                              rope[:, :, _HALF_SWAP].reshape(Q_LORA_RANK, -1)], axis=1).astype(BF16)
    ukv = w_ukv.reshape(KV_LORA_RANK, HEADS, QK_NOPE_DIM + V_HEAD_DIM)
    w_ukv_p = jnp.concatenate([ukv[:, :, :QK_NOPE_DIM].reshape(KV_LORA_RANK, -1),
                               ukv[:, :, QK_NOPE_DIM:].reshape(KV_LORA_RANK, -1)], axis=1).astype(BF16)
    return w_uq_p, w_ukv_p


def _rope_tables(n):
    rows = n // GRID_W
    row = np.repeat(np.arange(rows), GRID_W).astype(np.float32)
    col = np.tile(np.arange(GRID_W), rows).astype(np.float32)
    axis_dim = QK_ROPE_DIM // 2
    inv = (1.0 / (np.float32(ROPE_THETA) ** (np.arange(0, axis_dim, 2, dtype=np.float32) / np.float32(axis_dim)))
           ).astype(np.float32)
    ang_r = row[:, None] * inv
    ang_c = col[:, None] * inv
    cos = np.concatenate([np.cos(ang_r)] * 2 + [np.cos(ang_c)] * 2, axis=-1)
    sin = np.concatenate([-np.sin(ang_r), np.sin(ang_r), -np.sin(ang_c), np.sin(ang_c)], axis=-1)
    return jnp.asarray(cos, F32), jnp.asarray(sin, F32)


def kernel(x, c, ctx, c_ctx, w_mod, b_mod, g_norm_mix, g_norm_ffn, w_in, g_q_norm, w_uq, g_kv_norm,
           w_ukv, lb_fwd, lb_bwd, g_hgrn_norm, w_out, w_gate, w_up, w_down, g_final):
    B, N, D = x.shape
    L = ctx.shape[1]
    layer = 0

    cc = jnp.concatenate([c, c_ctx[None, :], jnp.zeros((7, D), F32)], axis=0)
    mod, *w_in_parts = _mod_call(cc, w_mod[layer], b_mod[layer][None, :], w_in[layer].T)
    pad = jnp.zeros((B, 2, D), F32)
    mod_lat = jnp.concatenate([mod[:B].reshape(B, 6, D), pad], axis=1)
    mod_ctx = jnp.concatenate([jnp.broadcast_to(mod[B].reshape(1, 6, D), (B, 6, D)), pad], axis=1)

    w_uq_p, w_ukv_p = _prep_weights(w_uq[layer], w_ukv[layer])
    cos, sin = _rope_tables(N)
    row2 = lambda v: v.reshape(1, -1)
    proj_args = (row2(g_norm_mix[layer]), *w_in_parts, row2(g_q_norm[layer]), w_uq_p,
                 row2(g_kv_norm[layer]), w_ukv_p)

    q_l, k_l, v_l, hq_l, hi_l, hg_l, *decay_l = _in_proj_call(
        x, mod_lat, *proj_args, cos, sin, lb_fwd, lb_bwd, tm=PROJ_ROWS, queries=True)
    k_c, v_c, hi_c, *decay_c = _in_proj_call(
        ctx, mod_ctx, *proj_args, jnp.ones((L, QK_ROPE_DIM), F32), jnp.zeros((L, QK_ROPE_DIM), F32), lb_fwd, lb_bwd,
        tm=L, queries=False)

    o_mla, (w_out16, w_gate16, w_up16, w_down16) = _attn_call(
        q_l, k_c, k_l, v_c, v_l, (w_out[layer], w_gate[layer], w_up[layer], w_down[layer]),
        tq=ATTN_Q_TILE, tk=ATTN_K_TILE)
    o_hgrn = _hgrn_call((hi_c, *decay_c), (hq_l, hi_l, hg_l, *decay_l), row2(g_hgrn_norm[layer]))

    return _out_ffn_call(x, o_mla, o_hgrn, mod_lat, row2(g_norm_ffn[layer]),
                         w_out16, w_gate16, w_up16, w_down16,
                         row2(g_final), tm=FFN_ROWS)
```

```python
import functools

import numpy as np
import jax
import jax.numpy as jnp
from jax import lax
from jax.experimental import pallas as pl
from jax.experimental.pallas import tpu as pltpu

D_MODEL = 1024
GRID_W = 64
EPS = 1e-6
ROPE_THETA = 10000.0
V_HEAD_DIM = 128
QK_NOPE_DIM = 128
QK_ROPE_DIM = 64
Q_LORA_RANK = 256
KV_LORA_RANK = 256
HEADS = 4
QK_HEAD_DIM = QK_NOPE_DIM + QK_ROPE_DIM
HGRN_DIM = 128
HGRN_WIDTH = HEADS * HGRN_DIM
CHUNK = 64
IN_SIZES = (Q_LORA_RANK, KV_LORA_RANK, QK_ROPE_DIM,
            HGRN_WIDTH, HGRN_WIDTH, HGRN_WIDTH, HGRN_WIDTH, HGRN_WIDTH)
D_FF = 2816
FF_CHUNK = 256
HGRN_BLOCK = 256
PROJ_ROWS = 1024
ATTN_Q_TILE = 512
ATTN_K_TILE = 1024
FFN_ROWS = 512
CAST_BLOCKS = 16
VMEM_LIMIT_BYTES = 56 * 1024 * 1024
Q_SCALE = float(np.log2(np.e) / np.sqrt(QK_HEAD_DIM))

BF16 = jnp.bfloat16
F32 = jnp.float32


def _dot(a, b):
    return jnp.dot(a, b, preferred_element_type=F32)


def _dot_nt(a, b):
    return lax.dot_general(a, b, (((1,), (1,)), ((), ())), preferred_element_type=F32)


def _dot_tn(a, b):
    return lax.dot_general(a, b, (((0,), (0,)), ((), ())), preferred_element_type=F32)


def _silu(x):
    return x * jax.nn.sigmoid(x)


def _rms(x):
    return x * lax.rsqrt(jnp.mean(x * x, axis=-1, keepdims=True) + EPS)


_MOD_STEPS = 8
_HG_COLS = 512


def _mod_kernel(c_ref, w_ref, b_ref, wt_ref, o_ref, wlat_ref, whg_ref, wkp_ref):
    a = _silu(c_ref[...]).astype(BF16)
    o_ref[...] = _dot(a, w_ref[...].astype(BF16)) + b_ref[...]

    j = pl.program_id(0)
    offs = np.cumsum((0,) + IN_SIZES)
    n_hg = (offs[-1] - offs[3]) // _HG_COLS

    @pl.when(j < n_hg)
    def _():
        r0 = pl.multiple_of(offs[3] + j * _HG_COLS, CHUNK)
        whg_ref[...] = wt_ref[pl.ds(r0, _HG_COLS), :].T.astype(BF16)

    @pl.when(j == n_hg)
    def _():
        wlat_ref[...] = wt_ref[:offs[2], :].T.astype(BF16)

    @pl.when(j == n_hg + 1)
    def _():
        kpe = wt_ref[offs[2]:offs[3], :]
        quarter = QK_ROPE_DIM // 4
        swapped = [kpe[q * quarter:(q + 1) * quarter] for q in (1, 0, 3, 2)]
        wkp_ref[...] = jnp.concatenate([kpe] + swapped, axis=0).T.astype(BF16)


def _mod_call(cc, w_mod, b_mod, w_in_t):
    rows = cc.shape[0]
    cols = w_mod.shape[1]
    tn = cols // _MOD_STEPS
    n_lat, n_hg, n_kp = Q_LORA_RANK + KV_LORA_RANK, 5 * HGRN_WIDTH, 2 * QK_ROPE_DIM
    assert n_hg // _HG_COLS + 2 <= _MOD_STEPS
    const = lambda j: (0, 0)
    return pl.pallas_call(
        _mod_kernel,
        grid=(_MOD_STEPS,),
        in_specs=[pl.BlockSpec((rows, D_MODEL), const),
                  pl.BlockSpec((D_MODEL, tn), lambda j: (0, j)),
                  pl.BlockSpec((1, tn), lambda j: (0, j)),
                  pl.BlockSpec(w_in_t.shape, const, pipeline_mode=pl.Buffered(1))],
        out_specs=[pl.BlockSpec((rows, tn), lambda j: (0, j)),
                   pl.BlockSpec((D_MODEL, n_lat), const),
                   pl.BlockSpec((D_MODEL, _HG_COLS), lambda j: (0, jnp.minimum(j, n_hg // _HG_COLS - 1))),
                   pl.BlockSpec((D_MODEL, n_kp), const)],
        out_shape=[jax.ShapeDtypeStruct((rows, cols), F32)]
                  + [jax.ShapeDtypeStruct((D_MODEL, w), BF16) for w in (n_lat, n_hg, n_kp)],
        compiler_params=pltpu.CompilerParams(dimension_semantics=("arbitrary",),
                                             vmem_limit_bytes=VMEM_LIMIT_BYTES),
        name="mod",
    )(cc, w_mod, b_mod, w_in_t)


def _chunk_cumsum(g, reverse):
    rows, w = g.shape
    x = g.reshape(rows // 8, 8, w)
    sub = lax.broadcasted_iota(jnp.int32, (1, 8, w), 1)
    for s in (1, 2, 4):
        if reverse:
            x = x + jnp.where(sub < 8 - s, pltpu.roll(x, 8 - s, axis=1), 0.0)
        else:
            x = x + jnp.where(sub >= s, pltpu.roll(x, s, axis=1), 0.0)
    groups = CHUNK // 8
    x = x.reshape(rows // CHUNK, groups, 8, w)
    edge = 0 if reverse else 7
    outs = [None] * groups
    carry = None
    for j in (range(groups - 1, -1, -1) if reverse else range(groups)):
        blk = x[:, j] if carry is None else x[:, j] + carry
        outs[j] = blk
        carry = jnp.broadcast_to(blk[:, edge:edge + 1, :], blk.shape)
    return jnp.stack(outs, axis=1).reshape(rows, w)


def _chunk_masks():
    row = lax.broadcasted_iota(jnp.int32, (HGRN_BLOCK, HGRN_BLOCK), 0)
    col = lax.broadcasted_iota(jnp.int32, (HGRN_BLOCK, HGRN_BLOCK), 1)
    same_chunk = (row // CHUNK) == (col // CHUNK)
    return same_chunk & (col <= row), same_chunk & (col >= row)


def _in_proj_kernel(x_ref, mod_ref, gmix_ref, wlat_ref, whg_ref, wkp_ref, gq_ref, wuq_ref, gkv_ref, wukv_ref,
                    cos_ref, sin_ref, lbf_ref, lbb_ref, *out_refs, queries):
    if queries:
        qt_ref, k_ref, vt_ref, hq_ref, hi_ref, hg_ref, kf_ref, bf_ref, kb_ref, bb_ref = out_refs
        hgrn_outs = ((0, hq_ref), (1, hi_ref), (2, hg_ref))
    else:
        k_ref, vt_ref, hi_ref, kf_ref, bf_ref, kb_ref, bb_ref = out_refs
        hgrn_outs = ((1, hi_ref),)
    x = x_ref[0]
    shift = mod_ref[0, 0:1, :]
    gain = gmix_ref[...] * (1.0 + mod_ref[0, 1:2, :])
    h = (_rms(x) * gain + shift).astype(BF16)

    cos1 = cos_ref[...]
    sin1 = sin_ref[...]

    lat = _dot(h, wlat_ref[...])
    ckv = (_rms(lat[:, Q_LORA_RANK:]) * gkv_ref[...]).astype(BF16)

    if queries:
        q = _dot((_rms(lat[:, :Q_LORA_RANK]) * gq_ref[...]).astype(BF16), wuq_ref[...])
        cos4 = jnp.concatenate([cos1] * HEADS, axis=1)
        sin4 = jnp.concatenate([sin1] * HEADS, axis=1)
        q_nope_t = (q[:, :512] * Q_SCALE).T
        q_rope_t = ((q[:, 512:768] * cos4 + q[:, 768:1024] * sin4) * Q_SCALE).T
        for hd in range(HEADS):
            qt_ref[0, hd, :QK_NOPE_DIM, :] = q_nope_t[hd * 128:(hd + 1) * 128].astype(BF16)
            qt_ref[0, hd, QK_NOPE_DIM:, :] = q_rope_t[hd * 64:(hd + 1) * 64].astype(BF16)
    kv = _dot(ckv, wukv_ref[...])
    kp = _dot(h, wkp_ref[...])
    k_rope = (kp[:, :QK_ROPE_DIM] * cos1 + kp[:, QK_ROPE_DIM:] * sin1).astype(BF16)
    v_t = kv[:, 512:].T
    for hd in range(HEADS):
        k_ref[0, hd, :, :QK_NOPE_DIM] = kv[:, hd * 128:(hd + 1) * 128].astype(BF16)
        k_ref[0, hd, :, QK_NOPE_DIM:] = k_rope
        vt_ref[0, hd] = v_t[hd * 128:(hd + 1) * 128].astype(BF16)

    for d, (lb_ref, kk_ref, b_ref) in enumerate(((lbf_ref, kf_ref, bf_ref), (lbb_ref, kb_ref, bb_ref))):
        c0 = (3 + d) * HGRN_WIDTH
        t = lb_ref[...]
        e = jnp.exp(t - jnp.max(t, axis=0, keepdims=True))
        lb = e[0:1] / jnp.sum(e, axis=0, keepdims=True)
        f = lb + (1.0 - lb) * jax.nn.sigmoid(_dot(h, whg_ref[:, c0:c0 + HGRN_WIDTH]))
        kk = (1.0 - f).astype(BF16)
        b = _chunk_cumsum(jnp.log2(f), reverse=(d == 1))
        for hd in range(HEADS):
            b_ref[0, hd] = b[:, hd * 128:(hd + 1) * 128]
            kk_ref[0, hd] = kk[:, hd * 128:(hd + 1) * 128]

    for j, o_ref in hgrn_outs:
        t = _dot(h, whg_ref[:, j * HGRN_WIDTH:(j + 1) * HGRN_WIDTH])
        for hd in range(HEADS):
            o_ref[0, hd] = t[:, hd * 128:(hd + 1) * 128].astype(o_ref.dtype)


def _in_proj_call(x, mod_rows, g_mix, w_lat, w_hgrn, w_kpe, g_q, w_uq, g_kv, w_ukv, cos, sin,
                  lb_fwd, lb_bwd, tm, queries):
    B, n, _ = x.shape
    const = lambda b, i: (0, 0)
    head_blk = lambda w: pl.BlockSpec((1, HEADS, tm, w), lambda b, i: (b, 0, i, 0))
    hshape = lambda w, dt: jax.ShapeDtypeStruct((B, HEADS, n, w), dt)
    head_blk_t = lambda w: pl.BlockSpec((1, HEADS, w, tm), lambda b, i: (b, 0, 0, i))
    hshape_t = lambda w: jax.ShapeDtypeStruct((B, HEADS, w, n), BF16)
    outs = [(head_blk(QK_HEAD_DIM), hshape(QK_HEAD_DIM, BF16)), (head_blk_t(V_HEAD_DIM), hshape_t(V_HEAD_DIM)),
            (head_blk(128), hshape(128, BF16)),
            (head_blk(128), hshape(128, BF16)), (head_blk(128), hshape(128, F32)),
            (head_blk(128), hshape(128, BF16)), (head_blk(128), hshape(128, F32))]
    if queries:
        outs = ([(head_blk_t(QK_HEAD_DIM), hshape_t(QK_HEAD_DIM))] + outs[:2]
                + [(head_blk(128), hshape(128, BF16)), outs[2], (head_blk(128), hshape(128, BF16))] + outs[3:])
    return pl.pallas_call(
        functools.partial(_in_proj_kernel, queries=queries),
        grid=(B, n // tm),
        in_specs=[pl.BlockSpec((1, tm, D_MODEL), lambda b, i: (b, i, 0)),
                  pl.BlockSpec((1, 8, D_MODEL), lambda b, i: (b, 0, 0)),
                  pl.BlockSpec((1, D_MODEL), const),
                  pl.BlockSpec(w_lat.shape, const),
                  pl.BlockSpec(w_hgrn.shape, const),
                  pl.BlockSpec(w_kpe.shape, const),
                  pl.BlockSpec((1, Q_LORA_RANK), const),
                  pl.BlockSpec((Q_LORA_RANK, 1024), const),
                  pl.BlockSpec((1, KV_LORA_RANK), const),
                  pl.BlockSpec((KV_LORA_RANK, 1024), const),
                  pl.BlockSpec((tm, QK_ROPE_DIM), lambda b, i: (i, 0)),
                  pl.BlockSpec((tm, QK_ROPE_DIM), lambda b, i: (i, 0)),
                  pl.BlockSpec(lb_fwd.shape, const),
                  pl.BlockSpec(lb_bwd.shape, const)],
        out_specs=[spec for spec, _ in outs],
        out_shape=[shape for _, shape in outs],
        compiler_params=pltpu.CompilerParams(
            dimension_semantics=("parallel", "parallel"), vmem_limit_bytes=VMEM_LIMIT_BYTES),
        name="in_proj",
    )(x, mod_rows, g_mix, w_lat, w_hgrn, w_kpe, g_q, w_uq, g_kv, w_ukv, cos, sin, lb_fwd, lb_bwd)


def _attn_kernel(qt_ref, kc_ref, kl_ref, vtc_ref, vtl_ref, *rest, tq, tk):
    n_cast = (len(rest) - 2) // 2
    o_ref, s_ref = rest[n_cast], rest[-1]
    for w_ref, w16_ref in zip(rest[:n_cast], rest[n_cast + 1:-1]):
        w16_ref[...] = w_ref[...].astype(BF16)

    n_ctx = kc_ref.shape[2]
    n_lat = kl_ref.shape[2]
    chunks = [(n_ctx, lambda: kc_ref[0, 0], lambda: vtc_ref[0, 0])]
    for j in range(n_lat // tk):
        chunks.append((tk, lambda j=j: kl_ref[0, 0, j * tk:(j + 1) * tk, :],
                       lambda j=j: vtl_ref[0, 0, :, j * tk:(j + 1) * tk]))

    def scores(j, qt, slot):
        rows, keys, _ = chunks[j]
        s_ref[slot, :rows, :] = _dot(keys(), qt)

    def load_q(t):
        return qt_ref[0, 0, :, pl.ds(pl.multiple_of(t * tq, tq), tq)]

    def q_tile(t, t_next, slot0):
        qt = load_q(t)
        m = l = acc = None
        for j, (rows, _, values_t) in enumerate(chunks):
            slot = (slot0 + j) % 2
            if j + 1 < len(chunks):
                scores(j + 1, qt, 1 - slot)
            else:
                scores(0, load_q(t_next), 1 - slot)
            s = s_ref[slot, :rows, :]
            m_chunk = jnp.max(s, axis=0, keepdims=True)
            m_new = m_chunk if m is None else jnp.maximum(m, m_chunk)
            p = jnp.exp2(s - m_new)
            l_chunk = jnp.sum(p, axis=0, keepdims=True)
            pv = _dot(values_t(), p.astype(BF16))
            if m is None:
                l, acc = l_chunk, pv
            else:
                alpha = jnp.exp2(m - m_new)
                l = alpha * l + l_chunk
                acc = alpha * acc + pv
            m = m_new
        o_ref[0, pl.ds(pl.multiple_of(t * tq, tq), tq), :] = (acc * (1.0 / l)).T.astype(o_ref.dtype)

    n_tiles = n_lat // tq
    scores(0, load_q(0), 0)

    def tile_pair(i, _):
        t = 2 * i
        q_tile(t, t + 1, 0)
        q_tile(t + 1, jnp.minimum(t + 2, n_tiles - 1), len(chunks) % 2)
        return 0

    lax.fori_loop(0, n_tiles // 2, tile_pair, 0)


def _attn_call(q_t, k_ctx, k_lat, vt_ctx, vt_lat, cast_weights, tq, tk):
    B, H, _, n = q_t.shape
    n_ctx = k_ctx.shape[2]
    steps_per_block = B * H // CAST_BLOCKS
    w_blk = lambda w: pl.BlockSpec((w.shape[0] // CAST_BLOCKS, w.shape[1]),
                                   lambda b, h: ((b * H + h) // steps_per_block, 0))
    outs = pl.pallas_call(
        functools.partial(_attn_kernel, tq=tq, tk=tk),
        grid=(B, H),
        in_specs=[pl.BlockSpec((1, 1, QK_HEAD_DIM, n), lambda b, h: (b, h, 0, 0)),
                  pl.BlockSpec((1, 1, n_ctx, QK_HEAD_DIM), lambda b, h: (b, h, 0, 0)),
                  pl.BlockSpec((1, 1, n, QK_HEAD_DIM), lambda b, h: (b, h, 0, 0)),
                  pl.BlockSpec((1, 1, V_HEAD_DIM, n_ctx), lambda b, h: (b, h, 0, 0)),
                  pl.BlockSpec((1, 1, V_HEAD_DIM, n), lambda b, h: (b, h, 0, 0))]
                 + [w_blk(w) for w in cast_weights],
        out_specs=[pl.BlockSpec((1, n, V_HEAD_DIM), lambda b, h: (b, 0, h))]
                  + [w_blk(w) for w in cast_weights],
        out_shape=[jax.ShapeDtypeStruct((B, n, H * V_HEAD_DIM), BF16)]
                  + [jax.ShapeDtypeStruct(w.shape, BF16) for w in cast_weights],
        scratch_shapes=[pltpu.VMEM((2, max(tk, n_ctx), tq), F32)],
        compiler_params=pltpu.CompilerParams(
            dimension_semantics=("arbitrary", "arbitrary"), vmem_limit_bytes=VMEM_LIMIT_BYTES),
        name="attn",
    )(q_t, k_ctx, k_lat, vt_ctx, vt_lat, *cast_weights)
    return outs[0], outs[1:]


_REF_ROWS = ((CHUNK // 2 - 1, CHUNK - 1), (CHUNK // 2, 0))
_CPB = HGRN_BLOCK // CHUNK
_FINISH_ROWS = 512


def _hgrn_kernel(vc_ref, kfc_ref, bfc_ref, kbc_ref, bbc_ref,
                 q_ref, v_ref, hg_ref, kf_ref, bf_ref, kb_ref, bb_ref, gon_ref,
                 o_ref, upd_ref, dec_ref, snap_ref, qd_ref, oin_ref):
    n_ctx = vc_ref.shape[2]
    n_lat = q_ref.shape[2]
    nc_ctx = n_ctx // CHUNK
    nc_lat = n_lat // CHUNK
    masks = _chunk_masks()

    def chunk_rows(x, r):
        return jnp.concatenate(
            [jnp.broadcast_to(x[c * CHUNK + r:c * CHUNK + r + 1], (CHUNK, x.shape[1])) for c in range(_CPB)],
            axis=0)

    def block_a(chunk0, rows, q, v, kks, bs):
        kds, decs, a, qds = [], [], None, []
        for d in range(2):
            r_ref, r_last = _REF_ROWS[d]
            kk, b = kks[d].astype(F32), bs[d]
            kds.append((kk * jnp.exp2(chunk_rows(b, r_last) - b)).astype(BF16))
            decs.append([jnp.exp2(b[c * CHUNK + r_last:c * CHUNK + r_last + 1]) for c in range(_CPB)])
            if q is not None:
                b_ref = chunk_rows(b, r_ref)
                qds.append((q * jnp.exp2(b)).astype(BF16))
                qa = (q * jnp.exp2(b - b_ref)).astype(BF16)
                ka = (kk * jnp.exp2(b_ref - b)).astype(BF16)
                a_d = jnp.where(masks[d], _dot_nt(qa, ka), 0.0)
                a = a_d if a is None else a + a_d
        kd = jnp.concatenate(kds, axis=1)
        for c in range(_CPB):
            rc = slice(c * CHUNK, (c + 1) * CHUNK)
            upd_ref[chunk0 + c] = _dot_tn(v[rc], kd[rc])
            dec_ref[chunk0 + c] = jnp.concatenate([decs[0][c], decs[1][c]], axis=1)
        if q is not None:
            qd_ref[rows, :] = jnp.concatenate(qds, axis=1)
            oin_ref[rows, :] = _dot(a.astype(BF16), v)

    for i in range(n_ctx // HGRN_BLOCK):
        r = slice(i * HGRN_BLOCK, (i + 1) * HGRN_BLOCK)
        block_a(i * _CPB, None, None, vc_ref[0, 0, r, :],
                (kfc_ref[0, 0, r, :], kbc_ref[0, 0, r, :]), (bfc_ref[0, 0, r, :], bbc_ref[0, 0, r, :]))

    def phase_a(i, _):
        r = pl.ds(pl.multiple_of(i * HGRN_BLOCK, HGRN_BLOCK), HGRN_BLOCK)
        block_a(nc_ctx + i * _CPB, r, q_ref[0, 0, r, :].astype(F32), v_ref[0, 0, r, :],
                (kf_ref[0, 0, r, :], kb_ref[0, 0, r, :]), (bf_ref[0, 0, r, :], bb_ref[0, 0, r, :]))
        return 0

    lax.fori_loop(0, n_lat // HGRN_BLOCK, phase_a, 0, unroll=8)

    def advance(st, cf, cb):
        dec = jnp.concatenate([dec_ref[cf][:, :HGRN_DIM], dec_ref[cb][:, HGRN_DIM:]], axis=1)
        upd = jnp.concatenate([upd_ref[cf][:, :HGRN_DIM], upd_ref[cb][:, HGRN_DIM:]], axis=1)
        return st * dec + upd

    st = jnp.zeros((HGRN_DIM, 2 * HGRN_DIM), F32)
    for i in range(nc_ctx):
        st = advance(st, i, nc_ctx - 1 - i)

    def phase_b(i, st):
        cf, cb = i, nc_lat - 1 - i
        sb = st.astype(BF16)
        snap_ref[cf, :, :HGRN_DIM] = sb[:, :HGRN_DIM]
        snap_ref[cb, :, HGRN_DIM:] = sb[:, HGRN_DIM:]
        return advance(st, nc_ctx + cf, nc_ctx + cb)

    lax.fori_loop(0, nc_lat, phase_b, st, unroll=4)

    cpf = _FINISH_ROWS // CHUNK
    def phase_c(i, _):
        r0 = pl.multiple_of(i * _FINISH_ROWS, _FINISH_ROWS)
        inter = [_dot_nt(qd_ref[pl.ds(r0 + c * CHUNK, CHUNK), :], snap_ref[i * cpf + c]) for c in range(cpf)]
        o = oin_ref[pl.ds(r0, _FINISH_ROWS), :] + jnp.concatenate(inter, axis=0)
        y = _rms(o) * gon_ref[...]
        gate = _silu(hg_ref[0, 0, pl.ds(r0, _FINISH_ROWS), :].astype(F32))
        o_ref[0, pl.ds(r0, _FINISH_ROWS), :] = (y * gate).astype(o_ref.dtype)
        return 0

    lax.fori_loop(0, n_lat // _FINISH_ROWS, phase_c, 0, unroll=8)


def _hgrn_call(ctx_ops, lat_ops, g_on):
    B, H, n, _ = lat_ops[0].shape
    n_ctx = ctx_ops[0].shape[2]
    n_chunks = (n + n_ctx) // CHUNK
    ctx_blk = pl.BlockSpec((1, 1, n_ctx, 128), lambda b, h: (b, h, 0, 0))
    lat_blk = pl.BlockSpec((1, 1, n, 128), lambda b, h: (b, h, 0, 0))
    return pl.pallas_call(
        _hgrn_kernel,
        grid=(B, H),
        in_specs=[ctx_blk] * len(ctx_ops) + [lat_blk] * len(lat_ops)
                 + [pl.BlockSpec((1, 128), lambda b, h: (0, 0))],
        out_specs=pl.BlockSpec((1, n, 128), lambda b, h: (b, 0, h)),
        out_shape=jax.ShapeDtypeStruct((B, n, H * 128), BF16),
        scratch_shapes=[pltpu.VMEM((n_chunks, HGRN_DIM, 2 * HGRN_DIM), F32),
                        pltpu.VMEM((n_chunks, 1, 2 * HGRN_DIM), F32),
                        pltpu.VMEM((n // CHUNK, HGRN_DIM, 2 * HGRN_DIM), BF16),
                        pltpu.VMEM((n, 2 * HGRN_DIM), BF16),
                        pltpu.VMEM((n, HGRN_DIM), F32)],
        compiler_params=pltpu.CompilerParams(
            dimension_semantics=("parallel", "parallel"), vmem_limit_bytes=VMEM_LIMIT_BYTES),
        name="hgrn",
    )(*ctx_ops, *lat_ops, g_on)


def _out_ffn_kernel(x_ref, om_ref, oh_ref, mod_ref, gffn_ref, wout_ref, wg_ref, wu_ref, wd_ref,
                    gfin_ref, o_ref):
    x = x_ref[0]
    mix = _dot(om_ref[0], wout_ref[:HEADS * V_HEAD_DIM, :]) + _dot(oh_ref[0], wout_ref[HEADS * V_HEAD_DIM:, :])
    x1 = x + mod_ref[0, 2:3, :] * mix
    gain = gffn_ref[...] * (1.0 + mod_ref[0, 4:5, :])
    h2 = (_rms(x1) * gain + mod_ref[0, 3:4, :]).astype(BF16)
    acts = []
    for j in range(D_FF // FF_CHUNK):
        c0 = j * FF_CHUNK
        g = _dot(h2, wg_ref[:, c0:c0 + FF_CHUNK])
        u = _dot(h2, wu_ref[:, c0:c0 + FF_CHUNK])
        acts.append((_silu(g) * u).astype(BF16))
    acc = _dot(jnp.concatenate(acts, axis=1), wd_ref[...])
    x2 = x1 + mod_ref[0, 5:6, :] * acc
    o_ref[0] = _rms(x2) * gfin_ref[...]


def _out_ffn_call(x, o_mla, o_hgrn, mod_rows, g_ffn, w_out, w_gate, w_up, w_down, g_final, tm):
    B, n, _ = x.shape
    const = lambda b, i: (0, 0)
    resident = lambda shape: pl.BlockSpec(shape, const, pipeline_mode=pl.Buffered(1))
    return pl.pallas_call(
        _out_ffn_kernel,
        grid=(B, n // tm),
        in_specs=[pl.BlockSpec((1, tm, D_MODEL), lambda b, i: (b, i, 0)),
                  pl.BlockSpec((1, tm, 512), lambda b, i: (b, i, 0)),
                  pl.BlockSpec((1, tm, 512), lambda b, i: (b, i, 0)),
                  pl.BlockSpec((1, 8, D_MODEL), lambda b, i: (b, 0, 0)),
                  pl.BlockSpec((1, D_MODEL), const),
                  resident((D_MODEL, D_MODEL)),
                  resident((D_MODEL, D_FF)),
                  resident((D_MODEL, D_FF)),
                  resident((D_FF, D_MODEL)),
                  pl.BlockSpec((1, D_MODEL), const)],
        out_specs=pl.BlockSpec((1, tm, D_MODEL), lambda b, i: (b, i, 0)),
        out_shape=jax.ShapeDtypeStruct((B, n, D_MODEL), F32),
        compiler_params=pltpu.CompilerParams(
            dimension_semantics=("parallel", "parallel"), vmem_limit_bytes=VMEM_LIMIT_BYTES),
        name="out_ffn",
    )(x, o_mla, o_hgrn, mod_rows, g_ffn, w_out, w_gate, w_up, w_down, g_final)


_HALF_SWAP = np.concatenate([np.arange(16, 32), np.arange(0, 16), np.arange(48, 64), np.arange(32, 48)])


def _prep_weights(w_uq, w_ukv):
    uq = w_uq.reshape(Q_LORA_RANK, HEADS, QK_HEAD_DIM)
    rope = uq[:, :, QK_NOPE_DIM:]
    w_uq_p = jnp.concatenate([uq[:, :, :QK_NOPE_DIM].reshape(Q_LORA_RANK, -1),
                              rope.reshape(Q_LORA_RANK, -1),
                              rope[:, :, _HALF_SWAP].reshape(Q_LORA_RANK, -1)], axis=1).astype(BF16)
    ukv = w_ukv.reshape(KV_LORA_RANK, HEADS, QK_NOPE_DIM + V_HEAD_DIM)
    w_ukv_p = jnp.concatenate([ukv[:, :, :QK_NOPE_DIM].reshape(KV_LORA_RANK, -1),
                               ukv[:, :, QK_NOPE_DIM:].reshape(KV_LORA_RANK, -1)], axis=1).astype(BF16)
    return w_uq_p, w_ukv_p


def _rope_tables(n):
    rows = n // GRID_W
    row = np.repeat(np.arange(rows), GRID_W).astype(np.float32)
    col = np.tile(np.arange(GRID_W), rows).astype(np.float32)
    axis_dim = QK_ROPE_DIM // 2
    inv = (1.0 / (np.float32(ROPE_THETA) ** (np.arange(0, axis_dim, 2, dtype=np.float32) / np.float32(axis_dim)))
           ).astype(np.float32)
    ang_r = row[:, None] * inv
    ang_c = col[:, None] * inv
    cos = np.concatenate([np.cos(ang_r)] * 2 + [np.cos(ang_c)] * 2, axis=-1)
    sin = np.concatenate([-np.sin(ang_r), np.sin(ang_r), -np.sin(ang_c), np.sin(ang_c)], axis=-1)
    return jnp.asarray(cos, F32), jnp.asarray(sin, F32)


def kernel(x, c, ctx, c_ctx, w_mod, b_mod, g_norm_mix, g_norm_ffn, w_in, g_q_norm, w_uq, g_kv_norm,
           w_ukv, lb_fwd, lb_bwd, g_hgrn_norm, w_out, w_gate, w_up, w_down, g_final):
    B, N, D = x.shape
    L = ctx.shape[1]
    layer = 0

    cc = jnp.concatenate([c, c_ctx[None, :], jnp.zeros((7, D), F32)], axis=0)
    mod, *w_in_parts = _mod_call(cc, w_mod[layer], b_mod[layer][None, :], w_in[layer].T)
    pad = jnp.zeros((B, 2, D), F32)
    mod_lat = jnp.concatenate([mod[:B].reshape(B, 6, D), pad], axis=1)
    mod_ctx = jnp.concatenate([jnp.broadcast_to(mod[B].reshape(1, 6, D), (B, 6, D)), pad], axis=1)

    w_uq_p, w_ukv_p = _prep_weights(w_uq[layer], w_ukv[layer])
    cos, sin = _rope_tables(N)
    row2 = lambda v: v.reshape(1, -1)
    proj_args = (row2(g_norm_mix[layer]), *w_in_parts, row2(g_q_norm[layer]), w_uq_p,
                 row2(g_kv_norm[layer]), w_ukv_p)

    q_l, k_l, v_l, hq_l, hi_l, hg_l, *decay_l = _in_proj_call(
        x, mod_lat, *proj_args, cos, sin, lb_fwd, lb_bwd, tm=PROJ_ROWS, queries=True)
    k_c, v_c, hi_c, *decay_c = _in_proj_call(
        ctx, mod_ctx, *proj_args, jnp.ones((L, QK_ROPE_DIM), F32), jnp.zeros((L, QK_ROPE_DIM), F32), lb_fwd, lb_bwd,
        tm=L, queries=False)

    o_mla, (w_out16, w_gate16, w_up16, w_down16) = _attn_call(
        q_l, k_c, k_l, v_c, v_l, (w_out[layer], w_gate[layer], w_up[layer], w_down[layer]),
        tq=ATTN_Q_TILE, tk=ATTN_K_TILE)
    o_hgrn = _hgrn_call((hi_c, *decay_c), (hq_l, hi_l, hg_l, *decay_l), row2(g_hgrn_norm[layer]))

    return _out_ffn_call(x, o_mla, o_hgrn, mod_lat, row2(g_norm_ffn[layer]),
                         w_out16, w_gate16, w_up16, w_down16,
                         row2(g_final), tm=FFN_ROWS)
```

```python
import functools

import numpy as np
import jax
import jax.numpy as jnp
from jax import lax
from jax.experimental import pallas as pl
from jax.experimental.pallas import tpu as pltpu

D_MODEL = 1024
GRID_W = 64
EPS = 1e-6
ROPE_THETA = 10000.0
V_HEAD_DIM = 128
QK_NOPE_DIM = 128
QK_ROPE_DIM = 64
Q_LORA_RANK = 256
KV_LORA_RANK = 256
HEADS = 4
QK_HEAD_DIM = QK_NOPE_DIM + QK_ROPE_DIM
HGRN_DIM = 128
HGRN_WIDTH = HEADS * HGRN_DIM
CHUNK = 64
IN_SIZES = (Q_LORA_RANK, KV_LORA_RANK, QK_ROPE_DIM,
            HGRN_WIDTH, HGRN_WIDTH, HGRN_WIDTH, HGRN_WIDTH, HGRN_WIDTH)
D_FF = 2816
FF_CHUNK = 256
HGRN_BLOCK = 256
PROJ_ROWS = 1024
ATTN_Q_TILE = 512
ATTN_K_TILE = 1024
FFN_ROWS = 512
CAST_BLOCKS = 16
VMEM_LIMIT_BYTES = 56 * 1024 * 1024
Q_SCALE = float(np.log2(np.e) / np.sqrt(QK_HEAD_DIM))

BF16 = jnp.bfloat16
F32 = jnp.float32


def _dot(a, b):
    return jnp.dot(a, b, preferred_element_type=F32)


def _dot_nt(a, b):
    return lax.dot_general(a, b, (((1,), (1,)), ((), ())), preferred_element_type=F32)


def _dot_tn(a, b):
    return lax.dot_general(a, b, (((0,), (0,)), ((), ())), preferred_element_type=F32)


def _silu(x):
    return x * jax.nn.sigmoid(x)


def _rms(x):
    return x * lax.rsqrt(jnp.mean(x * x, axis=-1, keepdims=True) + EPS)


_MOD_STEPS = 8
_HG_COLS = 512


def _mod_kernel(c_ref, w_ref, b_ref, wt_ref, o_ref, wlat_ref, whg_ref, wkp_ref):
    a = _silu(c_ref[...]).astype(BF16)
    o_ref[...] = _dot(a, w_ref[...].astype(BF16)) + b_ref[...]

    j = pl.program_id(0)
    offs = np.cumsum((0,) + IN_SIZES)
    n_hg = (offs[-1] - offs[3]) // _HG_COLS

    @pl.when(j < n_hg)
    def _():
        r0 = pl.multiple_of(offs[3] + j * _HG_COLS, CHUNK)
        whg_ref[...] = wt_ref[pl.ds(r0, _HG_COLS), :].T.astype(BF16)

    @pl.when(j == n_hg)
    def _():
        wlat_ref[...] = wt_ref[:offs[2], :].T.astype(BF16)

    @pl.when(j == n_hg + 1)
    def _():
        kpe = wt_ref[offs[2]:offs[3], :]
        quarter = QK_ROPE_DIM // 4
        swapped = [kpe[q * quarter:(q + 1) * quarter] for q in (1, 0, 3, 2)]
        wkp_ref[...] = jnp.concatenate([kpe] + swapped, axis=0).T.astype(BF16)


def _mod_call(cc, w_mod, b_mod, w_in_t):
    rows = cc.shape[0]
    cols = w_mod.shape[1]
    tn = cols // _MOD_STEPS
    n_lat, n_hg, n_kp = Q_LORA_RANK + KV_LORA_RANK, 5 * HGRN_WIDTH, 2 * QK_ROPE_DIM
    assert n_hg // _HG_COLS + 2 <= _MOD_STEPS
    const = lambda j: (0, 0)
    return pl.pallas_call(
        _mod_kernel,
        grid=(_MOD_STEPS,),
        in_specs=[pl.BlockSpec((rows, D_MODEL), const),
                  pl.BlockSpec((D_MODEL, tn), lambda j: (0, j)),
                  pl.BlockSpec((1, tn), lambda j: (0, j)),
                  pl.BlockSpec(w_in_t.shape, const, pipeline_mode=pl.Buffered(1))],
        out_specs=[pl.BlockSpec((rows, tn), lambda j: (0, j)),
                   pl.BlockSpec((D_MODEL, n_lat), const),
                   pl.BlockSpec((D_MODEL, _HG_COLS), lambda j: (0, jnp.minimum(j, n_hg // _HG_COLS - 1))),
                   pl.BlockSpec((D_MODEL, n_kp), const)],
        out_shape=[jax.ShapeDtypeStruct((rows, cols), F32)]
                  + [jax.ShapeDtypeStruct((D_MODEL, w), BF16) for w in (n_lat, n_hg, n_kp)],
        compiler_params=pltpu.CompilerParams(dimension_semantics=("arbitrary",),
                                             vmem_limit_bytes=VMEM_LIMIT_BYTES),
        name="mod",
    )(cc, w_mod, b_mod, w_in_t)


def _chunk_cumsum(g, reverse):
    rows, w = g.shape
    x = g.reshape(rows // 8, 8, w)
    sub = lax.broadcasted_iota(jnp.int32, (1, 8, w), 1)
    for s in (1, 2, 4):
        if reverse:
            x = x + jnp.where(sub < 8 - s, pltpu.roll(x, 8 - s, axis=1), 0.0)
        else:
            x = x + jnp.where(sub >= s, pltpu.roll(x, s, axis=1), 0.0)
    groups = CHUNK // 8
    x = x.reshape(rows // CHUNK, groups, 8, w)
    edge = 0 if reverse else 7
    outs = [None] * groups
    carry = None
    for j in (range(groups - 1, -1, -1) if reverse else range(groups)):
        blk = x[:, j] if carry is None else x[:, j] + carry
        outs[j] = blk
        carry = jnp.broadcast_to(blk[:, edge:edge + 1, :], blk.shape)
    return jnp.stack(outs, axis=1).reshape(rows, w)


def _chunk_masks():
    row = lax.broadcasted_iota(jnp.int32, (HGRN_BLOCK, HGRN_BLOCK), 0)
    col = lax.broadcasted_iota(jnp.int32, (HGRN_BLOCK, HGRN_BLOCK), 1)
    same_chunk = (row // CHUNK) == (col // CHUNK)
    return same_chunk & (col <= row), same_chunk & (col >= row)


def _in_proj_kernel(x_ref, mod_ref, gmix_ref, wlat_ref, whg_ref, wkp_ref, gq_ref, wuq_ref, gkv_ref, wukv_ref,
                    cos_ref, sin_ref, lbf_ref, lbb_ref, *out_refs, queries):
    if queries:
        qt_ref, k_ref, vt_ref, hq_ref, hi_ref, hg_ref, kf_ref, bf_ref, kb_ref, bb_ref = out_refs
        hgrn_outs = ((0, hq_ref), (1, hi_ref), (2, hg_ref))
    else:
        k_ref, vt_ref, hi_ref, kf_ref, bf_ref, kb_ref, bb_ref = out_refs
        hgrn_outs = ((1, hi_ref),)
    x = x_ref[0]
    shift = mod_ref[0, 0:1, :]
    gain = gmix_ref[...] * (1.0 + mod_ref[0, 1:2, :])
    h = (_rms(x) * gain + shift).astype(BF16)

    cos1 = cos_ref[...]
    sin1 = sin_ref[...]

    lat = _dot(h, wlat_ref[...])
    ckv = (_rms(lat[:, Q_LORA_RANK:]) * gkv_ref[...]).astype(BF16)

    if queries:
        q = _dot((_rms(lat[:, :Q_LORA_RANK]) * gq_ref[...]).astype(BF16), wuq_ref[...])
        cos4 = jnp.concatenate([cos1] * HEADS, axis=1)
        sin4 = jnp.concatenate([sin1] * HEADS, axis=1)
        q_nope_t = (q[:, :512] * Q_SCALE).T
        q_rope_t = ((q[:, 512:768] * cos4 + q[:, 768:1024] * sin4) * Q_SCALE).T
        for hd in range(HEADS):
            qt_ref[0, hd, :QK_NOPE_DIM, :] = q_nope_t[hd * 128:(hd + 1) * 128].astype(BF16)
            qt_ref[0, hd, QK_NOPE_DIM:, :] = q_rope_t[hd * 64:(hd + 1) * 64].astype(BF16)
    kv = _dot(ckv, wukv_ref[...])
    kp = _dot(h, wkp_ref[...])
    k_rope = (kp[:, :QK_ROPE_DIM] * cos1 + kp[:, QK_ROPE_DIM:] * sin1).astype(BF16)
    v_t = kv[:, 512:].T
    for hd in range(HEADS):
        k_ref[0, hd, :, :QK_NOPE_DIM] = kv[:, hd * 128:(hd + 1) * 128].astype(BF16)
        k_ref[0, hd, :, QK_NOPE_DIM:] = k_rope
        vt_ref[0, hd] = v_t[hd * 128:(hd + 1) * 128].astype(BF16)

    for d, (lb_ref, kk_ref, b_ref) in enumerate(((lbf_ref, kf_ref, bf_ref), (lbb_ref, kb_ref, bb_ref))):
        c0 = (3 + d) * HGRN_WIDTH
        t = lb_ref[...]
        e = jnp.exp(t - jnp.max(t, axis=0, keepdims=True))
        lb = e[0:1] / jnp.sum(e, axis=0, keepdims=True)
        f = lb + (1.0 - lb) * jax.nn.sigmoid(_dot(h, whg_ref[:, c0:c0 + HGRN_WIDTH]))
        kk = (1.0 - f).astype(BF16)
        b = _chunk_cumsum(jnp.log2(f), reverse=(d == 1))
        for hd in range(HEADS):
            b_ref[0, hd] = b[:, hd * 128:(hd + 1) * 128]
            kk_ref[0, hd] = kk[:, hd * 128:(hd + 1) * 128]

    for j, o_ref in hgrn_outs:
        t = _dot(h, whg_ref[:, j * HGRN_WIDTH:(j + 1) * HGRN_WIDTH])
        for hd in range(HEADS):
            o_ref[0, hd] = t[:, hd * 128:(hd + 1) * 128].astype(o_ref.dtype)


def _in_proj_call(x, mod_rows, g_mix, w_lat, w_hgrn, w_kpe, g_q, w_uq, g_kv, w_ukv, cos, sin,
                  lb_fwd, lb_bwd, tm, queries):
    B, n, _ = x.shape
    const = lambda b, i: (0, 0)
    head_blk = lambda w: pl.BlockSpec((1, HEADS, tm, w), lambda b, i: (b, 0, i, 0))
    hshape = lambda w, dt: jax.ShapeDtypeStruct((B, HEADS, n, w), dt)
    head_blk_t = lambda w: pl.BlockSpec((1, HEADS, w, tm), lambda b, i: (b, 0, 0, i))
    hshape_t = lambda w: jax.ShapeDtypeStruct((B, HEADS, w, n), BF16)
    outs = [(head_blk(QK_HEAD_DIM), hshape(QK_HEAD_DIM, BF16)), (head_blk_t(V_HEAD_DIM), hshape_t(V_HEAD_DIM)),
            (head_blk(128), hshape(128, BF16)),
            (head_blk(128), hshape(128, BF16)), (head_blk(128), hshape(128, F32)),
            (head_blk(128), hshape(128, BF16)), (head_blk(128), hshape(128, F32))]
    if queries:
        outs = ([(head_blk_t(QK_HEAD_DIM), hshape_t(QK_HEAD_DIM))] + outs[:2]
                + [(head_blk(128), hshape(128, BF16)), outs[2], (head_blk(128), hshape(128, BF16))] + outs[3:])
    return pl.pallas_call(
        functools.partial(_in_proj_kernel, queries=queries),
        grid=(B, n // tm),
        in_specs=[pl.BlockSpec((1, tm, D_MODEL), lambda b, i: (b, i, 0)),
                  pl.BlockSpec((1, 8, D_MODEL), lambda b, i: (b, 0, 0)),
                  pl.BlockSpec((1, D_MODEL), const),
                  pl.BlockSpec(w_lat.shape, const),
                  pl.BlockSpec(w_hgrn.shape, const),
                  pl.BlockSpec(w_kpe.shape, const),
                  pl.BlockSpec((1, Q_LORA_RANK), const),
                  pl.BlockSpec((Q_LORA_RANK, 1024), const),
                  pl.BlockSpec((1, KV_LORA_RANK), const),
                  pl.BlockSpec((KV_LORA_RANK, 1024), const),
                  pl.BlockSpec((tm, QK_ROPE_DIM), lambda b, i: (i, 0)),
                  pl.BlockSpec((tm, QK_ROPE_DIM), lambda b, i: (i, 0)),
                  pl.BlockSpec(lb_fwd.shape, const),
                  pl.BlockSpec(lb_bwd.shape, const)],
        out_specs=[spec for spec, _ in outs],
        out_shape=[shape for _, shape in outs],
        compiler_params=pltpu.CompilerParams(
            dimension_semantics=("parallel", "parallel"), vmem_limit_bytes=VMEM_LIMIT_BYTES),
        name="in_proj",
    )(x, mod_rows, g_mix, w_lat, w_hgrn, w_kpe, g_q, w_uq, g_kv, w_ukv, cos, sin, lb_fwd, lb_bwd)


def _attn_kernel(qt_ref, kc_ref, kl_ref, vtc_ref, vtl_ref, *rest, tq, tk):
    n_cast = (len(rest) - 3) // 2
    o_ref, s_ref, m_ref = rest[n_cast], rest[-2], rest[-1]
    for w_ref, w16_ref in zip(rest[:n_cast], rest[n_cast + 1:-2]):
        w16_ref[...] = w_ref[...].astype(BF16)

    n_ctx = kc_ref.shape[2]
    n_lat = kl_ref.shape[2]
    chunks = [(n_ctx, lambda: kc_ref[0, 0], lambda: vtc_ref[0, 0])]
    for j in range(n_lat // tk):
        chunks.append((tk, lambda j=j: kl_ref[0, 0, j * tk:(j + 1) * tk, :],
                       lambda j=j: vtl_ref[0, 0, :, j * tk:(j + 1) * tk]))

    def scores(j, qt, slot):
        rows, keys, _ = chunks[j]
        s = _dot(keys(), qt)
        s_ref[slot, :rows, :] = s
        m_ref[slot, 0:1, :] = jnp.max(s, axis=0, keepdims=True)

    def load_q(t):
        return qt_ref[0, 0, :, pl.ds(pl.multiple_of(t * tq, tq), tq)]

    def q_tile(t, t_next, slot0):
        qt = load_q(t)
        m = l = acc = None
        for j, (rows, _, values_t) in enumerate(chunks):
            slot = (slot0 + j) % 2
            if j + 1 < len(chunks):
                scores(j + 1, qt, 1 - slot)
            else:
                scores(0, load_q(t_next), 1 - slot)
            m_chunk = m_ref[slot, 0:1, :]
            m_new = m_chunk if m is None else jnp.maximum(m, m_chunk)
            p = jnp.exp2(s_ref[slot, :rows, :] - m_new)
            l_chunk = jnp.sum(p, axis=0, keepdims=True)
            pv = _dot(values_t(), p.astype(BF16))
            if m is None:
                l, acc = l_chunk, pv
            else:
                alpha = jnp.exp2(m - m_new)
                l = alpha * l + l_chunk
                acc = alpha * acc + pv
            m = m_new
        o_ref[0, pl.ds(pl.multiple_of(t * tq, tq), tq), :] = (acc * (1.0 / l)).T.astype(o_ref.dtype)

    n_tiles = n_lat // tq
    scores(0, load_q(0), 0)

    def tile_pair(i, _):
        t = 2 * i
        q_tile(t, t + 1, 0)
        q_tile(t + 1, jnp.minimum(t + 2, n_tiles - 1), len(chunks) % 2)
        return 0

    lax.fori_loop(0, n_tiles // 2, tile_pair, 0)


def _attn_call(q_t, k_ctx, k_lat, vt_ctx, vt_lat, cast_weights, tq, tk):
    B, H, _, n = q_t.shape
    n_ctx = k_ctx.shape[2]
    steps_per_block = B * H // CAST_BLOCKS
    w_blk = lambda w: pl.BlockSpec((w.shape[0] // CAST_BLOCKS, w.shape[1]),
                                   lambda b, h: ((b * H + h) // steps_per_block, 0))
    outs = pl.pallas_call(
        functools.partial(_attn_kernel, tq=tq, tk=tk),
        grid=(B, H),
        in_specs=[pl.BlockSpec((1, 1, QK_HEAD_DIM, n), lambda b, h: (b, h, 0, 0)),
                  pl.BlockSpec((1, 1, n_ctx, QK_HEAD_DIM), lambda b, h: (b, h, 0, 0)),
                  pl.BlockSpec((1, 1, n, QK_HEAD_DIM), lambda b, h: (b, h, 0, 0)),
                  pl.BlockSpec((1, 1, V_HEAD_DIM, n_ctx), lambda b, h: (b, h, 0, 0)),
                  pl.BlockSpec((1, 1, V_HEAD_DIM, n), lambda b, h: (b, h, 0, 0))]
                 + [w_blk(w) for w in cast_weights],
        out_specs=[pl.BlockSpec((1, n, V_HEAD_DIM), lambda b, h: (b, 0, h))]
                  + [w_blk(w) for w in cast_weights],
        out_shape=[jax.ShapeDtypeStruct((B, n, H * V_HEAD_DIM), BF16)]
                  + [jax.ShapeDtypeStruct(w.shape, BF16) for w in cast_weights],
        scratch_shapes=[pltpu.VMEM((2, max(tk, n_ctx), tq), F32),
                        pltpu.VMEM((2, 8, tq), F32)],
        compiler_params=pltpu.CompilerParams(
            dimension_semantics=("arbitrary", "arbitrary"), vmem_limit_bytes=VMEM_LIMIT_BYTES),
        name="attn",
    )(q_t, k_ctx, k_lat, vt_ctx, vt_lat, *cast_weights)
    return outs[0], outs[1:]


_REF_ROWS = ((CHUNK // 2 - 1, CHUNK - 1), (CHUNK // 2, 0))
_CPB = HGRN_BLOCK // CHUNK
_FINISH_ROWS = 512


def _hgrn_kernel(vc_ref, kfc_ref, bfc_ref, kbc_ref, bbc_ref,
                 q_ref, v_ref, hg_ref, kf_ref, bf_ref, kb_ref, bb_ref, gon_ref,
                 o_ref, upd_ref, dec_ref, snap_ref, qd_ref, oin_ref):
    n_ctx = vc_ref.shape[2]
    n_lat = q_ref.shape[2]
    nc_ctx = n_ctx // CHUNK
    nc_lat = n_lat // CHUNK
    masks = _chunk_masks()

    def chunk_rows(x, r):
        return jnp.concatenate(
            [jnp.broadcast_to(x[c * CHUNK + r:c * CHUNK + r + 1], (CHUNK, x.shape[1])) for c in range(_CPB)],
            axis=0)

    def block_a(chunk0, rows, q, v, kks, bs):
        kds, decs, a, qds = [], [], None, []
        for d in range(2):
            r_ref, r_last = _REF_ROWS[d]
            kk, b = kks[d].astype(F32), bs[d]
            kds.append((kk * jnp.exp2(chunk_rows(b, r_last) - b)).astype(BF16))
            decs.append([jnp.exp2(b[c * CHUNK + r_last:c * CHUNK + r_last + 1]) for c in range(_CPB)])
            if q is not None:
                b_ref = chunk_rows(b, r_ref)
                qds.append((q * jnp.exp2(b)).astype(BF16))
                qa = (q * jnp.exp2(b - b_ref)).astype(BF16)
                ka = (kk * jnp.exp2(b_ref - b)).astype(BF16)
                a_d = jnp.where(masks[d], _dot_nt(qa, ka), 0.0)
                a = a_d if a is None else a + a_d
        kd = jnp.concatenate(kds, axis=1)
        for c in range(_CPB):
            rc = slice(c * CHUNK, (c + 1) * CHUNK)
            upd_ref[chunk0 + c] = _dot_tn(v[rc], kd[rc])
            dec_ref[chunk0 + c] = jnp.concatenate([decs[0][c], decs[1][c]], axis=1)
        if q is not None:
            qd_ref[rows, :] = jnp.concatenate(qds, axis=1)
            oin_ref[rows, :] = _dot(a.astype(BF16), v)

    for i in range(n_ctx // HGRN_BLOCK):
        r = slice(i * HGRN_BLOCK, (i + 1) * HGRN_BLOCK)
        block_a(i * _CPB, None, None, vc_ref[0, 0, r, :],
                (kfc_ref[0, 0, r, :], kbc_ref[0, 0, r, :]), (bfc_ref[0, 0, r, :], bbc_ref[0, 0, r, :]))

    def phase_a(i, _):
        r = pl.ds(pl.multiple_of(i * HGRN_BLOCK, HGRN_BLOCK), HGRN_BLOCK)
        block_a(nc_ctx + i * _CPB, r, q_ref[0, 0, r, :].astype(F32), v_ref[0, 0, r, :],
                (kf_ref[0, 0, r, :], kb_ref[0, 0, r, :]), (bf_ref[0, 0, r, :], bb_ref[0, 0, r, :]))
        return 0

    lax.fori_loop(0, n_lat // HGRN_BLOCK, phase_a, 0, unroll=8)

    def advance(st, cf, cb):
        dec = jnp.concatenate([dec_ref[cf][:, :HGRN_DIM], dec_ref[cb][:, HGRN_DIM:]], axis=1)
        upd = jnp.concatenate([upd_ref[cf][:, :HGRN_DIM], upd_ref[cb][:, HGRN_DIM:]], axis=1)
        return st * dec + upd

    st = jnp.zeros((HGRN_DIM, 2 * HGRN_DIM), F32)
    for i in range(nc_ctx):
        st = advance(st, i, nc_ctx - 1 - i)

    def phase_b(i, st):
        cf, cb = i, nc_lat - 1 - i
        sb = st.astype(BF16)
        snap_ref[cf, :, :HGRN_DIM] = sb[:, :HGRN_DIM]
        snap_ref[cb, :, HGRN_DIM:] = sb[:, HGRN_DIM:]
        return advance(st, nc_ctx + cf, nc_ctx + cb)

    lax.fori_loop(0, nc_lat, phase_b, st, unroll=4)

    cpf = _FINISH_ROWS // CHUNK
    def phase_c(i, _):
        r0 = pl.multiple_of(i * _FINISH_ROWS, _FINISH_ROWS)
        inter = [_dot_nt(qd_ref[pl.ds(r0 + c * CHUNK, CHUNK), :], snap_ref[i * cpf + c]) for c in range(cpf)]
        o = oin_ref[pl.ds(r0, _FINISH_ROWS), :] + jnp.concatenate(inter, axis=0)
        y = _rms(o) * gon_ref[...]
        gate = _silu(hg_ref[0, 0, pl.ds(r0, _FINISH_ROWS), :].astype(F32))
        o_ref[0, pl.ds(r0, _FINISH_ROWS), :] = (y * gate).astype(o_ref.dtype)
        return 0

    lax.fori_loop(0, n_lat // _FINISH_ROWS, phase_c, 0, unroll=8)


def _hgrn_call(ctx_ops, lat_ops, g_on):
    B, H, n, _ = lat_ops[0].shape
    n_ctx = ctx_ops[0].shape[2]
    n_chunks = (n + n_ctx) // CHUNK
    ctx_blk = pl.BlockSpec((1, 1, n_ctx, 128), lambda b, h: (b, h, 0, 0))
    lat_blk = pl.BlockSpec((1, 1, n, 128), lambda b, h: (b, h, 0, 0))
    return pl.pallas_call(
        _hgrn_kernel,
        grid=(B, H),
        in_specs=[ctx_blk] * len(ctx_ops) + [lat_blk] * len(lat_ops)
                 + [pl.BlockSpec((1, 128), lambda b, h: (0, 0))],
        out_specs=pl.BlockSpec((1, n, 128), lambda b, h: (b, 0, h)),
        out_shape=jax.ShapeDtypeStruct((B, n, H * 128), BF16),
        scratch_shapes=[pltpu.VMEM((n_chunks, HGRN_DIM, 2 * HGRN_DIM), F32),
                        pltpu.VMEM((n_chunks, 1, 2 * HGRN_DIM), F32),
                        pltpu.VMEM((n // CHUNK, HGRN_DIM, 2 * HGRN_DIM), BF16),
                        pltpu.VMEM((n, 2 * HGRN_DIM), BF16),
                        pltpu.VMEM((n, HGRN_DIM), F32)],
        compiler_params=pltpu.CompilerParams(
            dimension_semantics=("parallel", "parallel"), vmem_limit_bytes=VMEM_LIMIT_BYTES),
        name="hgrn",
    )(*ctx_ops, *lat_ops, g_on)


def _out_ffn_kernel(x_ref, om_ref, oh_ref, mod_ref, gffn_ref, wout_ref, wg_ref, wu_ref, wd_ref,
                    gfin_ref, o_ref):
    x = x_ref[0]
    mix = _dot(om_ref[0], wout_ref[:HEADS * V_HEAD_DIM, :]) + _dot(oh_ref[0], wout_ref[HEADS * V_HEAD_DIM:, :])
    x1 = x + mod_ref[0, 2:3, :] * mix
    gain = gffn_ref[...] * (1.0 + mod_ref[0, 4:5, :])
    h2 = (_rms(x1) * gain + mod_ref[0, 3:4, :]).astype(BF16)
    acts = []
    for j in range(D_FF // FF_CHUNK):
        c0 = j * FF_CHUNK
        g = _dot(h2, wg_ref[:, c0:c0 + FF_CHUNK])
        u = _dot(h2, wu_ref[:, c0:c0 + FF_CHUNK])
        acts.append((_silu(g) * u).astype(BF16))
    acc = _dot(jnp.concatenate(acts, axis=1), wd_ref[...])
    x2 = x1 + mod_ref[0, 5:6, :] * acc
    o_ref[0] = _rms(x2) * gfin_ref[...]


def _out_ffn_call(x, o_mla, o_hgrn, mod_rows, g_ffn, w_out, w_gate, w_up, w_down, g_final, tm):
    B, n, _ = x.shape
    const = lambda b, i: (0, 0)
    resident = lambda shape: pl.BlockSpec(shape, const, pipeline_mode=pl.Buffered(1))
    return pl.pallas_call(
        _out_ffn_kernel,
        grid=(B, n // tm),
        in_specs=[pl.BlockSpec((1, tm, D_MODEL), lambda b, i: (b, i, 0)),
                  pl.BlockSpec((1, tm, 512), lambda b, i: (b, i, 0)),
                  pl.BlockSpec((1, tm, 512), lambda b, i: (b, i, 0)),
                  pl.BlockSpec((1, 8, D_MODEL), lambda b, i: (b, 0, 0)),
                  pl.BlockSpec((1, D_MODEL), const),
                  resident((D_MODEL, D_MODEL)),
                  resident((D_MODEL, D_FF)),
                  resident((D_MODEL, D_FF)),
                  resident((D_FF, D_MODEL)),
                  pl.BlockSpec((1, D_MODEL), const)],
        out_specs=pl.BlockSpec((1, tm, D_MODEL), lambda b, i: (b, i, 0)),
        out_shape=jax.ShapeDtypeStruct((B, n, D_MODEL), F32),
        compiler_params=pltpu.CompilerParams(
            dimension_semantics=("parallel", "parallel"), vmem_limit_bytes=VMEM_LIMIT_BYTES),
        name="out_ffn",
    )(x, o_mla, o_hgrn, mod_rows, g_ffn, w_out, w_gate, w_up, w_down, g_final)


_HALF_SWAP = np.concatenate([np.arange(16, 32), np.arange(0, 16), np.arange(48, 64), np.arange(32, 48)])


def _prep_weights(w_uq, w_ukv):
    uq = w_uq.reshape(Q_LORA_RANK, HEADS, QK_HEAD_DIM)
    rope = uq[:, :, QK_NOPE_DIM:]
    w_uq_p = jnp.concatenate([uq[:, :, :QK_NOPE_DIM].reshape(Q_LORA_RANK, -1),
                              rope.reshape(Q_LORA_RANK, -1),
                              rope[:, :, _HALF_SWAP].reshape(Q_LORA_RANK, -1)], axis=1).astype(BF16)
    ukv = w_ukv.reshape(KV_LORA_RANK, HEADS, QK_NOPE_DIM + V_HEAD_DIM)
    w_ukv_p = jnp.concatenate([ukv[:, :, :QK_NOPE_DIM].reshape(KV_LORA_RANK, -1),
                               ukv[:, :, QK_NOPE_DIM:].reshape(KV_LORA_RANK, -1)], axis=1).astype(BF16)
    return w_uq_p, w_ukv_p


def _rope_tables(n):
    rows = n // GRID_W
    row = np.repeat(np.arange(rows), GRID_W).astype(np.float32)
    col = np.tile(np.arange(GRID_W), rows).astype(np.float32)
    axis_dim = QK_ROPE_DIM // 2
    inv = (1.0 / (np.float32(ROPE_THETA) ** (np.arange(0, axis_dim, 2, dtype=np.float32) / np.float32(axis_dim)))
           ).astype(np.float32)
    ang_r = row[:, None] * inv
    ang_c = col[:, None] * inv
    cos = np.concatenate([np.cos(ang_r)] * 2 + [np.cos(ang_c)] * 2, axis=-1)
    sin = np.concatenate([-np.sin(ang_r), np.sin(ang_r), -np.sin(ang_c), np.sin(ang_c)], axis=-1)
    return jnp.asarray(cos, F32), jnp.asarray(sin, F32)


def kernel(x, c, ctx, c_ctx, w_mod, b_mod, g_norm_mix, g_norm_ffn, w_in, g_q_norm, w_uq, g_kv_norm,
           w_ukv, lb_fwd, lb_bwd, g_hgrn_norm, w_out, w_gate, w_up, w_down, g_final):
    B, N, D = x.shape
    L = ctx.shape[1]
    layer = 0

    cc = jnp.concatenate([c, c_ctx[None, :], jnp.zeros((7, D), F32)], axis=0)
    mod, *w_in_parts = _mod_call(cc, w_mod[layer], b_mod[layer][None, :], w_in[layer].T)
    pad = jnp.zeros((B, 2, D), F32)
    mod_lat = jnp.concatenate([mod[:B].reshape(B, 6, D), pad], axis=1)
    mod_ctx = jnp.concatenate([jnp.broadcast_to(mod[B].reshape(1, 6, D), (B, 6, D)), pad], axis=1)

    w_uq_p, w_ukv_p = _prep_weights(w_uq[layer], w_ukv[layer])
    cos, sin = _rope_tables(N)
    row2 = lambda v: v.reshape(1, -1)
    proj_args = (row2(g_norm_mix[layer]), *w_in_parts, row2(g_q_norm[layer]), w_uq_p,
                 row2(g_kv_norm[layer]), w_ukv_p)

    q_l, k_l, v_l, hq_l, hi_l, hg_l, *decay_l = _in_proj_call(
        x, mod_lat, *proj_args, cos, sin, lb_fwd, lb_bwd, tm=PROJ_ROWS, queries=True)
    k_c, v_c, hi_c, *decay_c = _in_proj_call(
        ctx, mod_ctx, *proj_args, jnp.ones((L, QK_ROPE_DIM), F32), jnp.zeros((L, QK_ROPE_DIM), F32), lb_fwd, lb_bwd,
        tm=L, queries=False)

    o_mla, (w_out16, w_gate16, w_up16, w_down16) = _attn_call(
        q_l, k_c, k_l, v_c, v_l, (w_out[layer], w_gate[layer], w_up[layer], w_down[layer]),
        tq=ATTN_Q_TILE, tk=ATTN_K_TILE)
    o_hgrn = _hgrn_call((hi_c, *decay_c), (hq_l, hi_l, hg_l, *decay_l), row2(g_hgrn_norm[layer]))

    return _out_ffn_call(x, o_mla, o_hgrn, mod_lat, row2(g_norm_ffn[layer]),
                         w_out16, w_gate16, w_up16, w_down16,
                         row2(g_final), tm=FFN_ROWS)
```

```python
import functools

import numpy as np
import jax
import jax.numpy as jnp
from jax import lax
from jax.experimental import pallas as pl
from jax.experimental.pallas import tpu as pltpu

D_MODEL = 1024
GRID_W = 64
EPS = 1e-6
ROPE_THETA = 10000.0
V_HEAD_DIM = 128
QK_NOPE_DIM = 128
QK_ROPE_DIM = 64
Q_LORA_RANK = 256
KV_LORA_RANK = 256
HEADS = 4
QK_HEAD_DIM = QK_NOPE_DIM + QK_ROPE_DIM
HGRN_DIM = 128
HGRN_WIDTH = HEADS * HGRN_DIM
CHUNK = 64
IN_SIZES = (Q_LORA_RANK, KV_LORA_RANK, QK_ROPE_DIM,
            HGRN_WIDTH, HGRN_WIDTH, HGRN_WIDTH, HGRN_WIDTH, HGRN_WIDTH)
D_FF = 2816
FF_CHUNK = 256
HGRN_BLOCK = 256
PROJ_ROWS = 1024
ATTN_Q_TILE = 512
ATTN_K_TILE = 1024
FFN_ROWS = 512
CAST_BLOCKS = 16
VMEM_LIMIT_BYTES = 56 * 1024 * 1024
Q_SCALE = float(np.log2(np.e) / np.sqrt(QK_HEAD_DIM))

BF16 = jnp.bfloat16
F32 = jnp.float32


def _dot(a, b):
    return jnp.dot(a, b, preferred_element_type=F32)


def _dot_nt(a, b):
    return lax.dot_general(a, b, (((1,), (1,)), ((), ())), preferred_element_type=F32)


def _dot_tn(a, b):
    return lax.dot_general(a, b, (((0,), (0,)), ((), ())), preferred_element_type=F32)


def _silu(x):
    return x * jax.nn.sigmoid(x)


def _rms(x):
    return x * lax.rsqrt(jnp.mean(x * x, axis=-1, keepdims=True) + EPS)


_MOD_STEPS = 8
_HG_COLS = 512


def _mod_kernel(c_ref, w_ref, b_ref, wt_ref, o_ref, wlat_ref, whg_ref, wkp_ref):
    a = _silu(c_ref[...]).astype(BF16)
    o_ref[...] = _dot(a, w_ref[...].astype(BF16)) + b_ref[...]

    j = pl.program_id(0)
    offs = np.cumsum((0,) + IN_SIZES)
    n_hg = (offs[-1] - offs[3]) // _HG_COLS

    @pl.when(j < n_hg)
    def _():
        r0 = pl.multiple_of(offs[3] + j * _HG_COLS, CHUNK)
        whg_ref[...] = wt_ref[pl.ds(r0, _HG_COLS), :].T.astype(BF16)

    @pl.when(j == n_hg)
    def _():
        wlat_ref[...] = wt_ref[:offs[2], :].T.astype(BF16)

    @pl.when(j == n_hg + 1)
    def _():
        kpe = wt_ref[offs[2]:offs[3], :]
        quarter = QK_ROPE_DIM // 4
        swapped = [kpe[q * quarter:(q + 1) * quarter] for q in (1, 0, 3, 2)]
        wkp_ref[...] = jnp.concatenate([kpe] + swapped, axis=0).T.astype(BF16)


def _mod_call(cc, w_mod, b_mod, w_in_t):
    rows = cc.shape[0]
    cols = w_mod.shape[1]
    tn = cols // _MOD_STEPS
    n_lat, n_hg, n_kp = Q_LORA_RANK + KV_LORA_RANK, 5 * HGRN_WIDTH, 2 * QK_ROPE_DIM
    assert n_hg // _HG_COLS + 2 <= _MOD_STEPS
    const = lambda j: (0, 0)
    return pl.pallas_call(
        _mod_kernel,
        grid=(_MOD_STEPS,),
        in_specs=[pl.BlockSpec((rows, D_MODEL), const),
                  pl.BlockSpec((D_MODEL, tn), lambda j: (0, j)),
                  pl.BlockSpec((1, tn), lambda j: (0, j)),
                  pl.BlockSpec(w_in_t.shape, const, pipeline_mode=pl.Buffered(1))],
        out_specs=[pl.BlockSpec((rows, tn), lambda j: (0, j)),
                   pl.BlockSpec((D_MODEL, n_lat), const),
                   pl.BlockSpec((D_MODEL, _HG_COLS), lambda j: (0, jnp.minimum(j, n_hg // _HG_COLS - 1))),
                   pl.BlockSpec((D_MODEL, n_kp), const)],
        out_shape=[jax.ShapeDtypeStruct((rows, cols), F32)]
                  + [jax.ShapeDtypeStruct((D_MODEL, w), BF16) for w in (n_lat, n_hg, n_kp)],
        compiler_params=pltpu.CompilerParams(dimension_semantics=("arbitrary",),
                                             vmem_limit_bytes=VMEM_LIMIT_BYTES),
        name="mod",
    )(cc, w_mod, b_mod, w_in_t)


def _chunk_cumsum(g, reverse):
    rows, w = g.shape
    x = g.reshape(rows // 8, 8, w)
    sub = lax.broadcasted_iota(jnp.int32, (1, 8, w), 1)
    for s in (1, 2, 4):
        if reverse:
            x = x + jnp.where(sub < 8 - s, pltpu.roll(x, 8 - s, axis=1), 0.0)
        else:
            x = x + jnp.where(sub >= s, pltpu.roll(x, s, axis=1), 0.0)
    groups = CHUNK // 8
    x = x.reshape(rows // CHUNK, groups, 8, w)
    edge = 0 if reverse else 7
    outs = [None] * groups
    carry = None
    for j in (range(groups - 1, -1, -1) if reverse else range(groups)):
        blk = x[:, j] if carry is None else x[:, j] + carry
        outs[j] = blk
        carry = jnp.broadcast_to(blk[:, edge:edge + 1, :], blk.shape)
    return jnp.stack(outs, axis=1).reshape(rows, w)


def _chunk_masks():
    row = lax.broadcasted_iota(jnp.int32, (HGRN_BLOCK, HGRN_BLOCK), 0)
    col = lax.broadcasted_iota(jnp.int32, (HGRN_BLOCK, HGRN_BLOCK), 1)
    same_chunk = (row // CHUNK) == (col // CHUNK)
    return same_chunk & (col <= row), same_chunk & (col >= row)


def _in_proj_kernel(x_ref, mod_ref, gmix_ref, wlat_ref, whg_ref, wkp_ref, gq_ref, wuq_ref, gkv_ref, wukv_ref,
                    cos_ref, sin_ref, lbf_ref, lbb_ref, *out_refs, queries):
    if queries:
        qt_ref, k_ref, vt_ref, hq_ref, hi_ref, hg_ref, kf_ref, bf_ref, kb_ref, bb_ref = out_refs
        hgrn_outs = ((0, hq_ref), (1, hi_ref), (2, hg_ref))
    else:
        k_ref, vt_ref, hi_ref, kf_ref, bf_ref, kb_ref, bb_ref = out_refs
        hgrn_outs = ((1, hi_ref),)
    x = x_ref[0]
    shift = mod_ref[0, 0:1, :]
    gain = gmix_ref[...] * (1.0 + mod_ref[0, 1:2, :])
    h = (_rms(x) * gain + shift).astype(BF16)

    cos1 = cos_ref[...]
    sin1 = sin_ref[...]

    lat = _dot(h, wlat_ref[...])
    ckv = (_rms(lat[:, Q_LORA_RANK:]) * gkv_ref[...]).astype(BF16)

    if queries:
        q = _dot((_rms(lat[:, :Q_LORA_RANK]) * gq_ref[...]).astype(BF16), wuq_ref[...])
        cos4 = jnp.concatenate([cos1] * HEADS, axis=1)
        sin4 = jnp.concatenate([sin1] * HEADS, axis=1)
        q_nope_t = (q[:, :512] * Q_SCALE).T
        q_rope_t = ((q[:, 512:768] * cos4 + q[:, 768:1024] * sin4) * Q_SCALE).T
        for hd in range(HEADS):
            qt_ref[0, hd, :QK_NOPE_DIM, :] = q_nope_t[hd * 128:(hd + 1) * 128].astype(BF16)
            qt_ref[0, hd, QK_NOPE_DIM:, :] = q_rope_t[hd * 64:(hd + 1) * 64].astype(BF16)
    kv = _dot(ckv, wukv_ref[...])
    kp = _dot(h, wkp_ref[...])
    k_rope = (kp[:, :QK_ROPE_DIM] * cos1 + kp[:, QK_ROPE_DIM:] * sin1).astype(BF16)
    v_t = kv[:, 512:].T
    for hd in range(HEADS):
        k_ref[0, hd, :, :QK_NOPE_DIM] = kv[:, hd * 128:(hd + 1) * 128].astype(BF16)
        k_ref[0, hd, :, QK_NOPE_DIM:] = k_rope
        vt_ref[0, hd] = v_t[hd * 128:(hd + 1) * 128].astype(BF16)

    for d, (lb_ref, kk_ref, b_ref) in enumerate(((lbf_ref, kf_ref, bf_ref), (lbb_ref, kb_ref, bb_ref))):
        c0 = (3 + d) * HGRN_WIDTH
        t = lb_ref[...]
        e = jnp.exp(t - jnp.max(t, axis=0, keepdims=True))
        lb = e[0:1] / jnp.sum(e, axis=0, keepdims=True)
        f = lb + (1.0 - lb) * jax.nn.sigmoid(_dot(h, whg_ref[:, c0:c0 + HGRN_WIDTH]))
        kk = (1.0 - f).astype(BF16)
        b = _chunk_cumsum(jnp.log2(f), reverse=(d == 1))
        for hd in range(HEADS):
            b_ref[0, hd] = b[:, hd * 128:(hd + 1) * 128]
            kk_ref[0, hd] = kk[:, hd * 128:(hd + 1) * 128]

    for j, o_ref in hgrn_outs:
        t = _dot(h, whg_ref[:, j * HGRN_WIDTH:(j + 1) * HGRN_WIDTH])
        for hd in range(HEADS):
            o_ref[0, hd] = t[:, hd * 128:(hd + 1) * 128].astype(o_ref.dtype)


def _in_proj_call(x, mod_rows, g_mix, w_lat, w_hgrn, w_kpe, g_q, w_uq, g_kv, w_ukv, cos, sin,
                  lb_fwd, lb_bwd, tm, queries):
    B, n, _ = x.shape
    const = lambda b, i: (0, 0)
    head_blk = lambda w: pl.BlockSpec((1, HEADS, tm, w), lambda b, i: (b, 0, i, 0))
    hshape = lambda w, dt: jax.ShapeDtypeStruct((B, HEADS, n, w), dt)
    head_blk_t = lambda w: pl.BlockSpec((1, HEADS, w, tm), lambda b, i: (b, 0, 0, i))
    hshape_t = lambda w: jax.ShapeDtypeStruct((B, HEADS, w, n), BF16)
    outs = [(head_blk(QK_HEAD_DIM), hshape(QK_HEAD_DIM, BF16)), (head_blk_t(V_HEAD_DIM), hshape_t(V_HEAD_DIM)),
            (head_blk(128), hshape(128, BF16)),
            (head_blk(128), hshape(128, BF16)), (head_blk(128), hshape(128, F32)),
            (head_blk(128), hshape(128, BF16)), (head_blk(128), hshape(128, F32))]
    if queries:
        outs = ([(head_blk_t(QK_HEAD_DIM), hshape_t(QK_HEAD_DIM))] + outs[:2]
                + [(head_blk(128), hshape(128, BF16)), outs[2], (head_blk(128), hshape(128, BF16))] + outs[3:])
    return pl.pallas_call(
        functools.partial(_in_proj_kernel, queries=queries),
        grid=(B, n // tm),
        in_specs=[pl.BlockSpec((1, tm, D_MODEL), lambda b, i: (b, i, 0)),
                  pl.BlockSpec((1, 8, D_MODEL), lambda b, i: (b, 0, 0)),
                  pl.BlockSpec((1, D_MODEL), const),
                  pl.BlockSpec(w_lat.shape, const),
                  pl.BlockSpec(w_hgrn.shape, const),
                  pl.BlockSpec(w_kpe.shape, const),
                  pl.BlockSpec((1, Q_LORA_RANK), const),
                  pl.BlockSpec((Q_LORA_RANK, 1024), const),
                  pl.BlockSpec((1, KV_LORA_RANK), const),
                  pl.BlockSpec((KV_LORA_RANK, 1024), const),
                  pl.BlockSpec((tm, QK_ROPE_DIM), lambda b, i: (i, 0)),
                  pl.BlockSpec((tm, QK_ROPE_DIM), lambda b, i: (i, 0)),
                  pl.BlockSpec(lb_fwd.shape, const),
                  pl.BlockSpec(lb_bwd.shape, const)],
        out_specs=[spec for spec, _ in outs],
        out_shape=[shape for _, shape in outs],
        compiler_params=pltpu.CompilerParams(
            dimension_semantics=("parallel", "parallel"), vmem_limit_bytes=VMEM_LIMIT_BYTES),
        name="in_proj",
    )(x, mod_rows, g_mix, w_lat, w_hgrn, w_kpe, g_q, w_uq, g_kv, w_ukv, cos, sin, lb_fwd, lb_bwd)


def _attn_kernel(qt_ref, kc_ref, kl_ref, vtc_ref, vtl_ref, *rest, tq, tk):
    n_cast = (len(rest) - 3) // 2
    o_ref, s_ref, m_ref = rest[n_cast], rest[-2], rest[-1]
    for w_ref, w16_ref in zip(rest[:n_cast], rest[n_cast + 1:-2]):
        w16_ref[...] = w_ref[...].astype(BF16)

    n_ctx = kc_ref.shape[2]
    n_lat = kl_ref.shape[2]
    chunks = [(n_ctx, lambda: kc_ref[0, 0], lambda: vtc_ref[0, 0])]
    for j in range(n_lat // tk):
        chunks.append((tk, lambda j=j: kl_ref[0, 0, j * tk:(j + 1) * tk, :],
                       lambda j=j: vtl_ref[0, 0, :, j * tk:(j + 1) * tk]))

    def scores(j, qt, slot):
        rows, keys, _ = chunks[j]
        s = _dot(keys(), qt)
        s_ref[slot, :rows, :] = s
        m_ref[slot, 0:1, :] = jnp.max(s, axis=0, keepdims=True)

    def load_q(t):
        return qt_ref[0, 0, :, pl.ds(pl.multiple_of(t * tq, tq), tq)]

    def q_tile(t, t_next, slot0):
        qt = load_q(t)
        m = l = acc = None
        for j, (rows, _, values_t) in enumerate(chunks):
            slot = (slot0 + j) % 2
            if j + 1 < len(chunks):
                scores(j + 1, qt, 1 - slot)
            else:
                scores(0, load_q(t_next), 1 - slot)
            m_chunk = m_ref[slot, 0:1, :]
            m_new = m_chunk if m is None else jnp.maximum(m, m_chunk)
            p = jnp.exp2(s_ref[slot, :rows, :] - m_new)
            l_chunk = jnp.sum(p, axis=0, keepdims=True)
            pv = _dot(values_t(), p.astype(BF16))
            if m is None:
                l, acc = l_chunk, pv
            else:
                alpha = jnp.exp2(m - m_new)
                l = alpha * l + l_chunk
                acc = alpha * acc + pv
            m = m_new
        o_ref[0, pl.ds(pl.multiple_of(t * tq, tq), tq), :] = (acc * (1.0 / l)).T.astype(o_ref.dtype)

    n_tiles = n_lat // tq
    scores(0, load_q(0), 0)

    def tile_pair(i, _):
        t = 2 * i
        q_tile(t, t + 1, 0)
        q_tile(t + 1, jnp.minimum(t + 2, n_tiles - 1), len(chunks) % 2)
        return 0

    lax.fori_loop(0, n_tiles // 2, tile_pair, 0)


def _attn_call(q_t, k_ctx, k_lat, vt_ctx, vt_lat, cast_weights, tq, tk):
    B, H, _, n = q_t.shape
    n_ctx = k_ctx.shape[2]
    steps_per_block = B * H // CAST_BLOCKS
    w_blk = lambda w: pl.BlockSpec((w.shape[0] // CAST_BLOCKS, w.shape[1]),
                                   lambda b, h: ((b * H + h) // steps_per_block, 0))
    outs = pl.pallas_call(
        functools.partial(_attn_kernel, tq=tq, tk=tk),
        grid=(B, H),
        in_specs=[pl.BlockSpec((1, 1, QK_HEAD_DIM, n), lambda b, h: (b, h, 0, 0)),
                  pl.BlockSpec((1, 1, n_ctx, QK_HEAD_DIM), lambda b, h: (b, h, 0, 0)),
                  pl.BlockSpec((1, 1, n, QK_HEAD_DIM), lambda b, h: (b, h, 0, 0)),
                  pl.BlockSpec((1, 1, V_HEAD_DIM, n_ctx), lambda b, h: (b, h, 0, 0)),
                  pl.BlockSpec((1, 1, V_HEAD_DIM, n), lambda b, h: (b, h, 0, 0))]
                 + [w_blk(w) for w in cast_weights],
        out_specs=[pl.BlockSpec((1, n, V_HEAD_DIM), lambda b, h: (b, 0, h))]
                  + [w_blk(w) for w in cast_weights],
        out_shape=[jax.ShapeDtypeStruct((B, n, H * V_HEAD_DIM), BF16)]
                  + [jax.ShapeDtypeStruct(w.shape, BF16) for w in cast_weights],
        scratch_shapes=[pltpu.VMEM((2, max(tk, n_ctx), tq), F32),
                        pltpu.VMEM((2, 8, tq), F32)],
        compiler_params=pltpu.CompilerParams(
            dimension_semantics=("arbitrary", "arbitrary"), vmem_limit_bytes=VMEM_LIMIT_BYTES),
        name="attn",
    )(q_t, k_ctx, k_lat, vt_ctx, vt_lat, *cast_weights)
    return outs[0], outs[1:]


_REF_ROWS = ((CHUNK // 2 - 1, CHUNK - 1), (CHUNK // 2, 0))
_CPB = HGRN_BLOCK // CHUNK
_FINISH_ROWS = 512


def _hgrn_kernel(vc_ref, kfc_ref, bfc_ref, kbc_ref, bbc_ref,
                 q_ref, v_ref, hg_ref, kf_ref, bf_ref, kb_ref, bb_ref, gon_ref,
                 o_ref, upd_ref, dec_ref, snap_ref, qd_ref, oin_ref):
    n_ctx = vc_ref.shape[2]
    n_lat = q_ref.shape[2]
    nc_ctx = n_ctx // CHUNK
    nc_lat = n_lat // CHUNK
    masks = _chunk_masks()

    def chunk_rows(x, r):
        return jnp.concatenate(
            [jnp.broadcast_to(x[c * CHUNK + r:c * CHUNK + r + 1], (CHUNK, x.shape[1])) for c in range(_CPB)],
            axis=0)

    def block_a(chunk0, rows, q, v, kks, bs):
        kds, decs, a, qds = [], [], None, []
        for d in range(2):
            r_ref, r_last = _REF_ROWS[d]
            kk, b = kks[d].astype(F32), bs[d]
            kds.append((kk * jnp.exp2(chunk_rows(b, r_last) - b)).astype(BF16))
            decs.append([jnp.exp2(b[c * CHUNK + r_last:c * CHUNK + r_last + 1]) for c in range(_CPB)])
            if q is not None:
                b_ref = chunk_rows(b, r_ref)
                qds.append((q * jnp.exp2(b)).astype(BF16))
                qa = (q * jnp.exp2(b - b_ref)).astype(BF16)
                ka = (kk * jnp.exp2(b_ref - b)).astype(BF16)
                a_d = jnp.where(masks[d], _dot_nt(qa, ka), 0.0)
                a = a_d if a is None else a + a_d
        kd = jnp.concatenate(kds, axis=1)
        for c in range(_CPB):
            rc = slice(c * CHUNK, (c + 1) * CHUNK)
            upd_ref[chunk0 + c] = _dot_tn(v[rc], kd[rc])
            dec_ref[chunk0 + c] = jnp.concatenate([decs[0][c], decs[1][c]], axis=1)
        if q is not None:
            qd_ref[rows, :] = jnp.concatenate(qds, axis=1)
            oin_ref[rows, :] = _dot(a.astype(BF16), v)

    for i in range(n_ctx // HGRN_BLOCK):
        r = slice(i * HGRN_BLOCK, (i + 1) * HGRN_BLOCK)
        block_a(i * _CPB, None, None, vc_ref[0, 0, r, :],
                (kfc_ref[0, 0, r, :], kbc_ref[0, 0, r, :]), (bfc_ref[0, 0, r, :], bbc_ref[0, 0, r, :]))

    def phase_a(i, _):
        r = pl.ds(pl.multiple_of(i * HGRN_BLOCK, HGRN_BLOCK), HGRN_BLOCK)
        block_a(nc_ctx + i * _CPB, r, q_ref[0, 0, r, :].astype(F32), v_ref[0, 0, r, :],
                (kf_ref[0, 0, r, :], kb_ref[0, 0, r, :]), (bf_ref[0, 0, r, :], bb_ref[0, 0, r, :]))
        return 0

    lax.fori_loop(0, n_lat // HGRN_BLOCK, phase_a, 0, unroll=16)

    def advance(st, cf, cb):
        dec = jnp.concatenate([dec_ref[cf][:, :HGRN_DIM], dec_ref[cb][:, HGRN_DIM:]], axis=1)
        upd = jnp.concatenate([upd_ref[cf][:, :HGRN_DIM], upd_ref[cb][:, HGRN_DIM:]], axis=1)
        return st * dec + upd

    st = jnp.zeros((HGRN_DIM, 2 * HGRN_DIM), F32)
    for i in range(nc_ctx):
        st = advance(st, i, nc_ctx - 1 - i)

    def phase_b(i, st):
        cf, cb = i, nc_lat - 1 - i
        sb = st.astype(BF16)
        snap_ref[cf, :, :HGRN_DIM] = sb[:, :HGRN_DIM]
        snap_ref[cb, :, HGRN_DIM:] = sb[:, HGRN_DIM:]
        return advance(st, nc_ctx + cf, nc_ctx + cb)

    lax.fori_loop(0, nc_lat, phase_b, st, unroll=4)

    cpf = _FINISH_ROWS // CHUNK
    def phase_c(i, _):
        r0 = pl.multiple_of(i * _FINISH_ROWS, _FINISH_ROWS)
        inter = [_dot_nt(qd_ref[pl.ds(r0 + c * CHUNK, CHUNK), :], snap_ref[i * cpf + c]) for c in range(cpf)]
        o = oin_ref[pl.ds(r0, _FINISH_ROWS), :] + jnp.concatenate(inter, axis=0)
        y = _rms(o) * gon_ref[...]
        gate = _silu(hg_ref[0, 0, pl.ds(r0, _FINISH_ROWS), :].astype(F32))
        o_ref[0, pl.ds(r0, _FINISH_ROWS), :] = (y * gate).astype(o_ref.dtype)
        return 0

    lax.fori_loop(0, n_lat // _FINISH_ROWS, phase_c, 0, unroll=8)


def _hgrn_call(ctx_ops, lat_ops, g_on):
    B, H, n, _ = lat_ops[0].shape
    n_ctx = ctx_ops[0].shape[2]
    n_chunks = (n + n_ctx) // CHUNK
    ctx_blk = pl.BlockSpec((1, 1, n_ctx, 128), lambda b, h: (b, h, 0, 0))
    lat_blk = pl.BlockSpec((1, 1, n, 128), lambda b, h: (b, h, 0, 0))
    return pl.pallas_call(
        _hgrn_kernel,
        grid=(B, H),
        in_specs=[ctx_blk] * len(ctx_ops) + [lat_blk] * len(lat_ops)
                 + [pl.BlockSpec((1, 128), lambda b, h: (0, 0))],
        out_specs=pl.BlockSpec((1, n, 128), lambda b, h: (b, 0, h)),
        out_shape=jax.ShapeDtypeStruct((B, n, H * 128), BF16),
        scratch_shapes=[pltpu.VMEM((n_chunks, HGRN_DIM, 2 * HGRN_DIM), F32),
                        pltpu.VMEM((n_chunks, 1, 2 * HGRN_DIM), F32),
                        pltpu.VMEM((n // CHUNK, HGRN_DIM, 2 * HGRN_DIM), BF16),
                        pltpu.VMEM((n, 2 * HGRN_DIM), BF16),
                        pltpu.VMEM((n, HGRN_DIM), F32)],
        compiler_params=pltpu.CompilerParams(
            dimension_semantics=("parallel", "parallel"), vmem_limit_bytes=VMEM_LIMIT_BYTES),
        name="hgrn",
    )(*ctx_ops, *lat_ops, g_on)


def _out_ffn_kernel(x_ref, om_ref, oh_ref, mod_ref, gffn_ref, wout_ref, wg_ref, wu_ref, wd_ref,
                    gfin_ref, o_ref):
    x = x_ref[0]
    mix = _dot(om_ref[0], wout_ref[:HEADS * V_HEAD_DIM, :]) + _dot(oh_ref[0], wout_ref[HEADS * V_HEAD_DIM:, :])
    x1 = x + mod_ref[0, 2:3, :] * mix
    gain = gffn_ref[...] * (1.0 + mod_ref[0, 4:5, :])
    h2 = (_rms(x1) * gain + mod_ref[0, 3:4, :]).astype(BF16)
    acts = []
    for j in range(D_FF // FF_CHUNK):
        c0 = j * FF_CHUNK
        g = _dot(h2, wg_ref[:, c0:c0 + FF_CHUNK])
        u = _dot(h2, wu_ref[:, c0:c0 + FF_CHUNK])
        acts.append((_silu(g) * u).astype(BF16))
    acc = _dot(jnp.concatenate(acts, axis=1), wd_ref[...])
    x2 = x1 + mod_ref[0, 5:6, :] * acc
    o_ref[0] = _rms(x2) * gfin_ref[...]


def _out_ffn_call(x, o_mla, o_hgrn, mod_rows, g_ffn, w_out, w_gate, w_up, w_down, g_final, tm):
    B, n, _ = x.shape
    const = lambda b, i: (0, 0)
    resident = lambda shape: pl.BlockSpec(shape, const, pipeline_mode=pl.Buffered(1))
    return pl.pallas_call(
        _out_ffn_kernel,
        grid=(B, n // tm),
        in_specs=[pl.BlockSpec((1, tm, D_MODEL), lambda b, i: (b, i, 0)),
                  pl.BlockSpec((1, tm, 512), lambda b, i: (b, i, 0)),
                  pl.BlockSpec((1, tm, 512), lambda b, i: (b, i, 0)),
                  pl.BlockSpec((1, 8, D_MODEL), lambda b, i: (b, 0, 0)),
                  pl.BlockSpec((1, D_MODEL), const),
                  resident((D_MODEL, D_MODEL)),
                  resident((D_MODEL, D_FF)),
                  resident((D_MODEL, D_FF)),
                  resident((D_FF, D_MODEL)),
                  pl.BlockSpec((1, D_MODEL), const)],
        out_specs=pl.BlockSpec((1, tm, D_MODEL), lambda b, i: (b, i, 0)),
        out_shape=jax.ShapeDtypeStruct((B, n, D_MODEL), F32),
        compiler_params=pltpu.CompilerParams(
            dimension_semantics=("parallel", "parallel"), vmem_limit_bytes=VMEM_LIMIT_BYTES),
        name="out_ffn",
    )(x, o_mla, o_hgrn, mod_rows, g_ffn, w_out, w_gate, w_up, w_down, g_final)


_HALF_SWAP = np.concatenate([np.arange(16, 32), np.arange(0, 16), np.arange(48, 64), np.arange(32, 48)])


def _prep_weights(w_uq, w_ukv):
    uq = w_uq.reshape(Q_LORA_RANK, HEADS, QK_HEAD_DIM)
    rope = uq[:, :, QK_NOPE_DIM:]
    w_uq_p = jnp.concatenate([uq[:, :, :QK_NOPE_DIM].reshape(Q_LORA_RANK, -1),
                              rope.reshape(Q_LORA_RANK, -1),
                              rope[:, :, _HALF_SWAP].reshape(Q_LORA_RANK, -1)], axis=1).astype(BF16)
    ukv = w_ukv.reshape(KV_LORA_RANK, HEADS, QK_NOPE_DIM + V_HEAD_DIM)
    w_ukv_p = jnp.concatenate([ukv[:, :, :QK_NOPE_DIM].reshape(KV_LORA_RANK, -1),
                               ukv[:, :, QK_NOPE_DIM:].reshape(KV_LORA_RANK, -1)], axis=1).astype(BF16)
    return w_uq_p, w_ukv_p


def _rope_tables(n):
    rows = n // GRID_W
    row = np.repeat(np.arange(rows), GRID_W).astype(np.float32)
    col = np.tile(np.arange(GRID_W), rows).astype(np.float32)
    axis_dim = QK_ROPE_DIM // 2
    inv = (1.0 / (np.float32(ROPE_THETA) ** (np.arange(0, axis_dim, 2, dtype=np.float32) / np.float32(axis_dim)))
           ).astype(np.float32)
    ang_r = row[:, None] * inv
    ang_c = col[:, None] * inv
    cos = np.concatenate([np.cos(ang_r)] * 2 + [np.cos(ang_c)] * 2, axis=-1)
    sin = np.concatenate([-np.sin(ang_r), np.sin(ang_r), -np.sin(ang_c), np.sin(ang_c)], axis=-1)
    return jnp.asarray(cos, F32), jnp.asarray(sin, F32)


def kernel(x, c, ctx, c_ctx, w_mod, b_mod, g_norm_mix, g_norm_ffn, w_in, g_q_norm, w_uq, g_kv_norm,
           w_ukv, lb_fwd, lb_bwd, g_hgrn_norm, w_out, w_gate, w_up, w_down, g_final):
    B, N, D = x.shape
    L = ctx.shape[1]
    layer = 0

    cc = jnp.concatenate([c, c_ctx[None, :], jnp.zeros((7, D), F32)], axis=0)
    mod, *w_in_parts = _mod_call(cc, w_mod[layer], b_mod[layer][None, :], w_in[layer].T)
    pad = jnp.zeros((B, 2, D), F32)
    mod_lat = jnp.concatenate([mod[:B].reshape(B, 6, D), pad], axis=1)
    mod_ctx = jnp.concatenate([jnp.broadcast_to(mod[B].reshape(1, 6, D), (B, 6, D)), pad], axis=1)

    w_uq_p, w_ukv_p = _prep_weights(w_uq[layer], w_ukv[layer])
    cos, sin = _rope_tables(N)
    row2 = lambda v: v.reshape(1, -1)
    proj_args = (row2(g_norm_mix[layer]), *w_in_parts, row2(g_q_norm[layer]), w_uq_p,
                 row2(g_kv_norm[layer]), w_ukv_p)

    q_l, k_l, v_l, hq_l, hi_l, hg_l, *decay_l = _in_proj_call(
        x, mod_lat, *proj_args, cos, sin, lb_fwd, lb_bwd, tm=PROJ_ROWS, queries=True)
    k_c, v_c, hi_c, *decay_c = _in_proj_call(
        ctx, mod_ctx, *proj_args, jnp.ones((L, QK_ROPE_DIM), F32), jnp.zeros((L, QK_ROPE_DIM), F32), lb_fwd, lb_bwd,
        tm=L, queries=False)

    o_mla, (w_out16, w_gate16, w_up16, w_down16) = _attn_call(
        q_l, k_c, k_l, v_c, v_l, (w_out[layer], w_gate[layer], w_up[layer], w_down[layer]),
        tq=ATTN_Q_TILE, tk=ATTN_K_TILE)
    o_hgrn = _hgrn_call((hi_c, *decay_c), (hq_l, hi_l, hg_l, *decay_l), row2(g_hgrn_norm[layer]))

    return _out_ffn_call(x, o_mla, o_hgrn, mod_lat, row2(g_norm_ffn[layer]),
                         w_out16, w_gate16, w_up16, w_down16,
                         row2(g_final), tm=FFN_ROWS)
```

```python
import functools

import numpy as np
import jax
import jax.numpy as jnp
from jax import lax
from jax.experimental import pallas as pl
from jax.experimental.pallas import tpu as pltpu

D_MODEL = 1024
GRID_W = 64
EPS = 1e-6
ROPE_THETA = 10000.0
V_HEAD_DIM = 128
QK_NOPE_DIM = 128
QK_ROPE_DIM = 64
Q_LORA_RANK = 256
KV_LORA_RANK = 256
HEADS = 4
QK_HEAD_DIM = QK_NOPE_DIM + QK_ROPE_DIM
HGRN_DIM = 128
HGRN_WIDTH = HEADS * HGRN_DIM
CHUNK = 64
IN_SIZES = (Q_LORA_RANK, KV_LORA_RANK, QK_ROPE_DIM,
            HGRN_WIDTH, HGRN_WIDTH, HGRN_WIDTH, HGRN_WIDTH, HGRN_WIDTH)
D_FF = 2816
FF_CHUNK = 256
HGRN_BLOCK = 256
PROJ_ROWS = 1024
ATTN_Q_TILE = 512
ATTN_K_TILE = 1024
FFN_ROWS = 1024
CAST_BLOCKS = 16
VMEM_LIMIT_BYTES = 56 * 1024 * 1024
Q_SCALE = float(np.log2(np.e) / np.sqrt(QK_HEAD_DIM))

BF16 = jnp.bfloat16
F32 = jnp.float32


def _dot(a, b):
    return jnp.dot(a, b, preferred_element_type=F32)


def _dot_nt(a, b):
    return lax.dot_general(a, b, (((1,), (1,)), ((), ())), preferred_element_type=F32)


def _dot_tn(a, b):
    return lax.dot_general(a, b, (((0,), (0,)), ((), ())), preferred_element_type=F32)


def _silu(x):
    return x * jax.nn.sigmoid(x)


def _rms(x):
    return x * lax.rsqrt(jnp.mean(x * x, axis=-1, keepdims=True) + EPS)


_MOD_STEPS = 8
_HG_COLS = 512


def _mod_kernel(c_ref, w_ref, b_ref, wt_ref, o_ref, wlat_ref, whg_ref, wkp_ref):
    a = _silu(c_ref[...]).astype(BF16)
    o_ref[...] = _dot(a, w_ref[...].astype(BF16)) + b_ref[...]

    j = pl.program_id(0)
    offs = np.cumsum((0,) + IN_SIZES)
    n_hg = (offs[-1] - offs[3]) // _HG_COLS

    @pl.when(j < n_hg)
    def _():
        r0 = pl.multiple_of(offs[3] + j * _HG_COLS, CHUNK)
        whg_ref[...] = wt_ref[pl.ds(r0, _HG_COLS), :].T.astype(BF16)

    @pl.when(j == n_hg)
    def _():
        wlat_ref[...] = wt_ref[:offs[2], :].T.astype(BF16)

    @pl.when(j == n_hg + 1)
    def _():
        kpe = wt_ref[offs[2]:offs[3], :]
        quarter = QK_ROPE_DIM // 4
        swapped = [kpe[q * quarter:(q + 1) * quarter] for q in (1, 0, 3, 2)]
        wkp_ref[...] = jnp.concatenate([kpe] + swapped, axis=0).T.astype(BF16)


def _mod_call(cc, w_mod, b_mod, w_in_t):
    rows = cc.shape[0]
    cols = w_mod.shape[1]
    tn = cols // _MOD_STEPS
    n_lat, n_hg, n_kp = Q_LORA_RANK + KV_LORA_RANK, 5 * HGRN_WIDTH, 2 * QK_ROPE_DIM
    assert n_hg // _HG_COLS + 2 <= _MOD_STEPS
    const = lambda j: (0, 0)
    return pl.pallas_call(
        _mod_kernel,
        grid=(_MOD_STEPS,),
        in_specs=[pl.BlockSpec((rows, D_MODEL), const),
                  pl.BlockSpec((D_MODEL, tn), lambda j: (0, j)),
                  pl.BlockSpec((1, tn), lambda j: (0, j)),
                  pl.BlockSpec(w_in_t.shape, const, pipeline_mode=pl.Buffered(1))],
        out_specs=[pl.BlockSpec((rows, tn), lambda j: (0, j)),
                   pl.BlockSpec((D_MODEL, n_lat), const),
                   pl.BlockSpec((D_MODEL, _HG_COLS), lambda j: (0, jnp.minimum(j, n_hg // _HG_COLS - 1))),
                   pl.BlockSpec((D_MODEL, n_kp), const)],
        out_shape=[jax.ShapeDtypeStruct((rows, cols), F32)]
                  + [jax.ShapeDtypeStruct((D_MODEL, w), BF16) for w in (n_lat, n_hg, n_kp)],
        compiler_params=pltpu.CompilerParams(dimension_semantics=("arbitrary",),
                                             vmem_limit_bytes=VMEM_LIMIT_BYTES),
        name="mod",
    )(cc, w_mod, b_mod, w_in_t)


def _chunk_cumsum(g, reverse):
    rows, w = g.shape
    x = g.reshape(rows // 8, 8, w)
    sub = lax.broadcasted_iota(jnp.int32, (1, 8, w), 1)
    for s in (1, 2, 4):
        if reverse:
            x = x + jnp.where(sub < 8 - s, pltpu.roll(x, 8 - s, axis=1), 0.0)
        else:
            x = x + jnp.where(sub >= s, pltpu.roll(x, s, axis=1), 0.0)
    groups = CHUNK // 8
    x = x.reshape(rows // CHUNK, groups, 8, w)
    edge = 0 if reverse else 7
    outs = [None] * groups
    carry = None
    for j in (range(groups - 1, -1, -1) if reverse else range(groups)):
        blk = x[:, j] if carry is None else x[:, j] + carry
        outs[j] = blk
        carry = jnp.broadcast_to(blk[:, edge:edge + 1, :], blk.shape)
    return jnp.stack(outs, axis=1).reshape(rows, w)


def _chunk_masks():
    row = lax.broadcasted_iota(jnp.int32, (HGRN_BLOCK, HGRN_BLOCK), 0)
    col = lax.broadcasted_iota(jnp.int32, (HGRN_BLOCK, HGRN_BLOCK), 1)
    same_chunk = (row // CHUNK) == (col // CHUNK)
    return same_chunk & (col <= row), same_chunk & (col >= row)


def _in_proj_kernel(x_ref, mod_ref, gmix_ref, wlat_ref, whg_ref, wkp_ref, gq_ref, wuq_ref, gkv_ref, wukv_ref,
                    cos_ref, sin_ref, lbf_ref, lbb_ref, *out_refs, queries):
    if queries:
        qt_ref, k_ref, vt_ref, hq_ref, hi_ref, hg_ref, kf_ref, bf_ref, kb_ref, bb_ref = out_refs
        hgrn_outs = ((0, hq_ref), (1, hi_ref), (2, hg_ref))
    else:
        k_ref, vt_ref, hi_ref, kf_ref, bf_ref, kb_ref, bb_ref = out_refs
        hgrn_outs = ((1, hi_ref),)
    x = x_ref[0]
    shift = mod_ref[0, 0:1, :]
    gain = gmix_ref[...] * (1.0 + mod_ref[0, 1:2, :])
    h = (_rms(x) * gain + shift).astype(BF16)

    cos1 = cos_ref[...]
    sin1 = sin_ref[...]

    lat = _dot(h, wlat_ref[...])
    ckv = (_rms(lat[:, Q_LORA_RANK:]) * gkv_ref[...]).astype(BF16)

    if queries:
        q = _dot((_rms(lat[:, :Q_LORA_RANK]) * gq_ref[...]).astype(BF16), wuq_ref[...])
        cos4 = jnp.concatenate([cos1] * HEADS, axis=1)
        sin4 = jnp.concatenate([sin1] * HEADS, axis=1)
        q_nope_t = (q[:, :512] * Q_SCALE).T
        q_rope_t = ((q[:, 512:768] * cos4 + q[:, 768:1024] * sin4) * Q_SCALE).T
        for hd in range(HEADS):
            qt_ref[0, hd, :QK_NOPE_DIM, :] = q_nope_t[hd * 128:(hd + 1) * 128].astype(BF16)
            qt_ref[0, hd, QK_NOPE_DIM:, :] = q_rope_t[hd * 64:(hd + 1) * 64].astype(BF16)
    kv = _dot(ckv, wukv_ref[...])
    kp = _dot(h, wkp_ref[...])
    k_rope = (kp[:, :QK_ROPE_DIM] * cos1 + kp[:, QK_ROPE_DIM:] * sin1).astype(BF16)
    v_t = kv[:, 512:].T
    for hd in range(HEADS):
        k_ref[0, hd, :, :QK_NOPE_DIM] = kv[:, hd * 128:(hd + 1) * 128].astype(BF16)
        k_ref[0, hd, :, QK_NOPE_DIM:] = k_rope
        vt_ref[0, hd] = v_t[hd * 128:(hd + 1) * 128].astype(BF16)

    for d, (lb_ref, kk_ref, b_ref) in enumerate(((lbf_ref, kf_ref, bf_ref), (lbb_ref, kb_ref, bb_ref))):
        c0 = (3 + d) * HGRN_WIDTH
        t = lb_ref[...]
        e = jnp.exp(t - jnp.max(t, axis=0, keepdims=True))
        lb = e[0:1] / jnp.sum(e, axis=0, keepdims=True)
        f = lb + (1.0 - lb) * jax.nn.sigmoid(_dot(h, whg_ref[:, c0:c0 + HGRN_WIDTH]))
        kk = (1.0 - f).astype(BF16)
        b = _chunk_cumsum(jnp.log2(f), reverse=(d == 1))
        for hd in range(HEADS):
            b_ref[0, hd] = b[:, hd * 128:(hd + 1) * 128]
            kk_ref[0, hd] = kk[:, hd * 128:(hd + 1) * 128]

    for j, o_ref in hgrn_outs:
        t = _dot(h, whg_ref[:, j * HGRN_WIDTH:(j + 1) * HGRN_WIDTH])
        for hd in range(HEADS):
            o_ref[0, hd] = t[:, hd * 128:(hd + 1) * 128].astype(o_ref.dtype)


def _in_proj_call(x, mod_rows, g_mix, w_lat, w_hgrn, w_kpe, g_q, w_uq, g_kv, w_ukv, cos, sin,
                  lb_fwd, lb_bwd, tm, queries):
    B, n, _ = x.shape
    const = lambda b, i: (0, 0)
    head_blk = lambda w: pl.BlockSpec((1, HEADS, tm, w), lambda b, i: (b, 0, i, 0))
    hshape = lambda w, dt: jax.ShapeDtypeStruct((B, HEADS, n, w), dt)
    head_blk_t = lambda w: pl.BlockSpec((1, HEADS, w, tm), lambda b, i: (b, 0, 0, i))
    hshape_t = lambda w: jax.ShapeDtypeStruct((B, HEADS, w, n), BF16)
    outs = [(head_blk(QK_HEAD_DIM), hshape(QK_HEAD_DIM, BF16)), (head_blk_t(V_HEAD_DIM), hshape_t(V_HEAD_DIM)),
            (head_blk(128), hshape(128, BF16)),
            (head_blk(128), hshape(128, BF16)), (head_blk(128), hshape(128, F32)),
            (head_blk(128), hshape(128, BF16)), (head_blk(128), hshape(128, F32))]
    if queries:
        outs = ([(head_blk_t(QK_HEAD_DIM), hshape_t(QK_HEAD_DIM))] + outs[:2]
                + [(head_blk(128), hshape(128, BF16)), outs[2], (head_blk(128), hshape(128, BF16))] + outs[3:])
    return pl.pallas_call(
        functools.partial(_in_proj_kernel, queries=queries),
        grid=(B, n // tm),
        in_specs=[pl.BlockSpec((1, tm, D_MODEL), lambda b, i: (b, i, 0)),
                  pl.BlockSpec((1, 8, D_MODEL), lambda b, i: (b, 0, 0)),
                  pl.BlockSpec((1, D_MODEL), const),
                  pl.BlockSpec(w_lat.shape, const),
                  pl.BlockSpec(w_hgrn.shape, const),
                  pl.BlockSpec(w_kpe.shape, const),
                  pl.BlockSpec((1, Q_LORA_RANK), const),
                  pl.BlockSpec((Q_LORA_RANK, 1024), const),
                  pl.BlockSpec((1, KV_LORA_RANK), const),
                  pl.BlockSpec((KV_LORA_RANK, 1024), const),
                  pl.BlockSpec((tm, QK_ROPE_DIM), lambda b, i: (i, 0)),
                  pl.BlockSpec((tm, QK_ROPE_DIM), lambda b, i: (i, 0)),
                  pl.BlockSpec(lb_fwd.shape, const),
                  pl.BlockSpec(lb_bwd.shape, const)],
        out_specs=[spec for spec, _ in outs],
        out_shape=[shape for _, shape in outs],
        compiler_params=pltpu.CompilerParams(
            dimension_semantics=("parallel", "parallel"), vmem_limit_bytes=VMEM_LIMIT_BYTES),
        name="in_proj",
    )(x, mod_rows, g_mix, w_lat, w_hgrn, w_kpe, g_q, w_uq, g_kv, w_ukv, cos, sin, lb_fwd, lb_bwd)


def _attn_kernel(qt_ref, kc_ref, kl_ref, vtc_ref, vtl_ref, *rest, tq, tk):
    n_cast = (len(rest) - 3) // 2
    o_ref, s_ref, m_ref = rest[n_cast], rest[-2], rest[-1]
    for w_ref, w16_ref in zip(rest[:n_cast], rest[n_cast + 1:-2]):
        w16_ref[...] = w_ref[...].astype(BF16)

    n_ctx = kc_ref.shape[2]
    n_lat = kl_ref.shape[2]
    chunks = [(n_ctx, lambda: kc_ref[0, 0], lambda: vtc_ref[0, 0])]
    for j in range(n_lat // tk):
        chunks.append((tk, lambda j=j: kl_ref[0, 0, j * tk:(j + 1) * tk, :],
                       lambda j=j: vtl_ref[0, 0, :, j * tk:(j + 1) * tk]))

    def scores(j, qt, slot):
        rows, keys, _ = chunks[j]
        s = _dot(keys(), qt)
        s_ref[slot, :rows, :] = s
        m_ref[slot, 0:1, :] = jnp.max(s, axis=0, keepdims=True)

    def load_q(t):
        return qt_ref[0, 0, :, pl.ds(pl.multiple_of(t * tq, tq), tq)]

    def q_tile(t, t_next, slot0):
        qt = load_q(t)
        m = l = acc = None
        for j, (rows, _, values_t) in enumerate(chunks):
            slot = (slot0 + j) % 2
            if j + 1 < len(chunks):
                scores(j + 1, qt, 1 - slot)
            else:
                scores(0, load_q(t_next), 1 - slot)
            m_chunk = m_ref[slot, 0:1, :]
            m_new = m_chunk if m is None else jnp.maximum(m, m_chunk)
            p = jnp.exp2(s_ref[slot, :rows, :] - m_new)
            l_chunk = jnp.sum(p, axis=0, keepdims=True)
            pv = _dot(values_t(), p.astype(BF16))
            if m is None:
                l, acc = l_chunk, pv
            else:
                alpha = jnp.exp2(m - m_new)
                l = alpha * l + l_chunk
                acc = alpha * acc + pv
            m = m_new
        o_ref[0, pl.ds(pl.multiple_of(t * tq, tq), tq), :] = (acc * (1.0 / l)).T.astype(o_ref.dtype)

    n_tiles = n_lat // tq
    scores(0, load_q(0), 0)

    def tile_pair(i, _):
        t = 2 * i
        q_tile(t, t + 1, 0)
        q_tile(t + 1, jnp.minimum(t + 2, n_tiles - 1), len(chunks) % 2)
        return 0

    lax.fori_loop(0, n_tiles // 2, tile_pair, 0)


def _attn_call(q_t, k_ctx, k_lat, vt_ctx, vt_lat, cast_weights, tq, tk):
    B, H, _, n = q_t.shape
    n_ctx = k_ctx.shape[2]
    steps_per_block = B * H // CAST_BLOCKS
    w_blk = lambda w: pl.BlockSpec((w.shape[0] // CAST_BLOCKS, w.shape[1]),
                                   lambda b, h: ((b * H + h) // steps_per_block, 0))
    outs = pl.pallas_call(
        functools.partial(_attn_kernel, tq=tq, tk=tk),
        grid=(B, H),
        in_specs=[pl.BlockSpec((1, 1, QK_HEAD_DIM, n), lambda b, h: (b, h, 0, 0)),
                  pl.BlockSpec((1, 1, n_ctx, QK_HEAD_DIM), lambda b, h: (b, h, 0, 0)),
                  pl.BlockSpec((1, 1, n, QK_HEAD_DIM), lambda b, h: (b, h, 0, 0)),
                  pl.BlockSpec((1, 1, V_HEAD_DIM, n_ctx), lambda b, h: (b, h, 0, 0)),
                  pl.BlockSpec((1, 1, V_HEAD_DIM, n), lambda b, h: (b, h, 0, 0))]
                 + [w_blk(w) for w in cast_weights],
        out_specs=[pl.BlockSpec((1, n, V_HEAD_DIM), lambda b, h: (b, 0, h))]
                  + [w_blk(w) for w in cast_weights],
        out_shape=[jax.ShapeDtypeStruct((B, n, H * V_HEAD_DIM), BF16)]
                  + [jax.ShapeDtypeStruct(w.shape, BF16) for w in cast_weights],
        scratch_shapes=[pltpu.VMEM((2, max(tk, n_ctx), tq), F32),
                        pltpu.VMEM((2, 8, tq), F32)],
        compiler_params=pltpu.CompilerParams(
            dimension_semantics=("arbitrary", "arbitrary"), vmem_limit_bytes=VMEM_LIMIT_BYTES),
        name="attn",
    )(q_t, k_ctx, k_lat, vt_ctx, vt_lat, *cast_weights)
    return outs[0], outs[1:]


_REF_ROWS = ((CHUNK // 2 - 1, CHUNK - 1), (CHUNK // 2, 0))
_CPB = HGRN_BLOCK // CHUNK
_FINISH_ROWS = 512


def _hgrn_kernel(vc_ref, kfc_ref, bfc_ref, kbc_ref, bbc_ref,
                 q_ref, v_ref, hg_ref, kf_ref, bf_ref, kb_ref, bb_ref, gon_ref,
                 o_ref, upd_ref, dec_ref, snap_ref, qd_ref, oin_ref):
    n_ctx = vc_ref.shape[2]
    n_lat = q_ref.shape[2]
    nc_ctx = n_ctx // CHUNK
    nc_lat = n_lat // CHUNK
    masks = _chunk_masks()

    def chunk_rows(x, r):
        return jnp.concatenate(
            [jnp.broadcast_to(x[c * CHUNK + r:c * CHUNK + r + 1], (CHUNK, x.shape[1])) for c in range(_CPB)],
            axis=0)

    def block_a(chunk0, rows, q, v, kks, bs):
        kds, decs, a, qds = [], [], None, []
        for d in range(2):
            r_ref, r_last = _REF_ROWS[d]
            kk, b = kks[d].astype(F32), bs[d]
            kds.append((kk * jnp.exp2(chunk_rows(b, r_last) - b)).astype(BF16))
            decs.append([jnp.exp2(b[c * CHUNK + r_last:c * CHUNK + r_last + 1]) for c in range(_CPB)])
            if q is not None:
                b_ref = chunk_rows(b, r_ref)
                qds.append((q * jnp.exp2(b)).astype(BF16))
                qa = (q * jnp.exp2(b - b_ref)).astype(BF16)
                ka = (kk * jnp.exp2(b_ref - b)).astype(BF16)
                a_d = jnp.where(masks[d], _dot_nt(qa, ka), 0.0)
                a = a_d if a is None else a + a_d
        kd = jnp.concatenate(kds, axis=1)
        for c in range(_CPB):
            rc = slice(c * CHUNK, (c + 1) * CHUNK)
            upd_ref[chunk0 + c] = _dot_tn(v[rc], kd[rc])
            dec_ref[chunk0 + c] = jnp.concatenate([decs[0][c], decs[1][c]], axis=1)
        if q is not None:
            qd_ref[rows, :] = jnp.concatenate(qds, axis=1)
            oin_ref[rows, :] = _dot(a.astype(BF16), v)

    for i in range(n_ctx // HGRN_BLOCK):
        r = slice(i * HGRN_BLOCK, (i + 1) * HGRN_BLOCK)
        block_a(i * _CPB, None, None, vc_ref[0, 0, r, :],
                (kfc_ref[0, 0, r, :], kbc_ref[0, 0, r, :]), (bfc_ref[0, 0, r, :], bbc_ref[0, 0, r, :]))

    def phase_a(i, _):
        r = pl.ds(pl.multiple_of(i * HGRN_BLOCK, HGRN_BLOCK), HGRN_BLOCK)
        block_a(nc_ctx + i * _CPB, r, q_ref[0, 0, r, :].astype(F32), v_ref[0, 0, r, :],
                (kf_ref[0, 0, r, :], kb_ref[0, 0, r, :]), (bf_ref[0, 0, r, :], bb_ref[0, 0, r, :]))
        return 0

    lax.fori_loop(0, n_lat // HGRN_BLOCK, phase_a, 0, unroll=16)

    def advance(st, cf, cb):
        dec = jnp.concatenate([dec_ref[cf][:, :HGRN_DIM], dec_ref[cb][:, HGRN_DIM:]], axis=1)
        upd = jnp.concatenate([upd_ref[cf][:, :HGRN_DIM], upd_ref[cb][:, HGRN_DIM:]], axis=1)
        return st * dec + upd

    st = jnp.zeros((HGRN_DIM, 2 * HGRN_DIM), F32)
    for i in range(nc_ctx):
        st = advance(st, i, nc_ctx - 1 - i)

    def phase_b(i, st):
        cf, cb = i, nc_lat - 1 - i
        sb = st.astype(BF16)
        snap_ref[cf, :, :HGRN_DIM] = sb[:, :HGRN_DIM]
        snap_ref[cb, :, HGRN_DIM:] = sb[:, HGRN_DIM:]
        return advance(st, nc_ctx + cf, nc_ctx + cb)

    lax.fori_loop(0, nc_lat, phase_b, st, unroll=4)

    cpf = _FINISH_ROWS // CHUNK
    def phase_c(i, _):
        r0 = pl.multiple_of(i * _FINISH_ROWS, _FINISH_ROWS)
        inter = [_dot_nt(qd_ref[pl.ds(r0 + c * CHUNK, CHUNK), :], snap_ref[i * cpf + c]) for c in range(cpf)]
        o = oin_ref[pl.ds(r0, _FINISH_ROWS), :] + jnp.concatenate(inter, axis=0)
        y = _rms(o) * gon_ref[...]
        gate = _silu(hg_ref[0, 0, pl.ds(r0, _FINISH_ROWS), :].astype(F32))
        o_ref[0, pl.ds(r0, _FINISH_ROWS), :] = (y * gate).astype(o_ref.dtype)
        return 0

    lax.fori_loop(0, n_lat // _FINISH_ROWS, phase_c, 0, unroll=8)


def _hgrn_call(ctx_ops, lat_ops, g_on):
    B, H, n, _ = lat_ops[0].shape
    n_ctx = ctx_ops[0].shape[2]
    n_chunks = (n + n_ctx) // CHUNK
    ctx_blk = pl.BlockSpec((1, 1, n_ctx, 128), lambda b, h: (b, h, 0, 0))
    lat_blk = pl.BlockSpec((1, 1, n, 128), lambda b, h: (b, h, 0, 0))
    return pl.pallas_call(
        _hgrn_kernel,
        grid=(B, H),
        in_specs=[ctx_blk] * len(ctx_ops) + [lat_blk] * len(lat_ops)
                 + [pl.BlockSpec((1, 128), lambda b, h: (0, 0))],
        out_specs=pl.BlockSpec((1, n, 128), lambda b, h: (b, 0, h)),
        out_shape=jax.ShapeDtypeStruct((B, n, H * 128), BF16),
        scratch_shapes=[pltpu.VMEM((n_chunks, HGRN_DIM, 2 * HGRN_DIM), F32),
                        pltpu.VMEM((n_chunks, 1, 2 * HGRN_DIM), F32),
                        pltpu.VMEM((n // CHUNK, HGRN_DIM, 2 * HGRN_DIM), BF16),
                        pltpu.VMEM((n, 2 * HGRN_DIM), BF16),
                        pltpu.VMEM((n, HGRN_DIM), F32)],
        compiler_params=pltpu.CompilerParams(
            dimension_semantics=("parallel", "parallel"), vmem_limit_bytes=VMEM_LIMIT_BYTES),
        name="hgrn",
    )(*ctx_ops, *lat_ops, g_on)


def _out_ffn_kernel(x_ref, om_ref, oh_ref, mod_ref, gffn_ref, wout_ref, wg_ref, wu_ref, wd_ref,
                    gfin_ref, o_ref):
    x = x_ref[0]
    mix = _dot(om_ref[0], wout_ref[:HEADS * V_HEAD_DIM, :]) + _dot(oh_ref[0], wout_ref[HEADS * V_HEAD_DIM:, :])
    x1 = x + mod_ref[0, 2:3, :] * mix
    gain = gffn_ref[...] * (1.0 + mod_ref[0, 4:5, :])
    h2 = (_rms(x1) * gain + mod_ref[0, 3:4, :]).astype(BF16)
    acts = []
    for j in range(D_FF // FF_CHUNK):
        c0 = j * FF_CHUNK
        g = _dot(h2, wg_ref[:, c0:c0 + FF_CHUNK])
        u = _dot(h2, wu_ref[:, c0:c0 + FF_CHUNK])
        acts.append((_silu(g) * u).astype(BF16))
    acc = _dot(jnp.concatenate(acts, axis=1), wd_ref[...])
    x2 = x1 + mod_ref[0, 5:6, :] * acc
    o_ref[0] = _rms(x2) * gfin_ref[...]


def _out_ffn_call(x, o_mla, o_hgrn, mod_rows, g_ffn, w_out, w_gate, w_up, w_down, g_final, tm):
    B, n, _ = x.shape
    const = lambda b, i: (0, 0)
    resident = lambda shape: pl.BlockSpec(shape, const, pipeline_mode=pl.Buffered(1))
    return pl.pallas_call(
        _out_ffn_kernel,
        grid=(B, n // tm),
        in_specs=[pl.BlockSpec((1, tm, D_MODEL), lambda b, i: (b, i, 0)),
                  pl.BlockSpec((1, tm, 512), lambda b, i: (b, i, 0)),
                  pl.BlockSpec((1, tm, 512), lambda b, i: (b, i, 0)),
                  pl.BlockSpec((1, 8, D_MODEL), lambda b, i: (b, 0, 0)),
                  pl.BlockSpec((1, D_MODEL), const),
                  resident((D_MODEL, D_MODEL)),
                  resident((D_MODEL, D_FF)),
                  resident((D_MODEL, D_FF)),
                  resident((D_FF, D_MODEL)),
                  pl.BlockSpec((1, D_MODEL), const)],
        out_specs=pl.BlockSpec((1, tm, D_MODEL), lambda b, i: (b, i, 0)),
        out_shape=jax.ShapeDtypeStruct((B, n, D_MODEL), F32),
        compiler_params=pltpu.CompilerParams(
            dimension_semantics=("parallel", "parallel"), vmem_limit_bytes=VMEM_LIMIT_BYTES),
        name="out_ffn",
    )(x, o_mla, o_hgrn, mod_rows, g_ffn, w_out, w_gate, w_up, w_down, g_final)


_HALF_SWAP = np.concatenate([np.arange(16, 32), np.arange(0, 16), np.arange(48, 64), np.arange(32, 48)])


def _prep_weights(w_uq, w_ukv):
    uq = w_uq.reshape(Q_LORA_RANK, HEADS, QK_HEAD_DIM)
    rope = uq[:, :, QK_NOPE_DIM:]
    w_uq_p = jnp.concatenate([uq[:, :, :QK_NOPE_DIM].reshape(Q_LORA_RANK, -1),
                              rope.reshape(Q_LORA_RANK, -1),
                              rope[:, :, _HALF_SWAP].reshape(Q_LORA_RANK, -1)], axis=1).astype(BF16)
    ukv = w_ukv.reshape(KV_LORA_RANK, HEADS, QK_NOPE_DIM + V_HEAD_DIM)
    w_ukv_p = jnp.concatenate([ukv[:, :, :QK_NOPE_DIM].reshape(KV_LORA_RANK, -1),
                               ukv[:, :, QK_NOPE_DIM:].reshape(KV_LORA_RANK, -1)], axis=1).astype(BF16)
    return w_uq_p, w_ukv_p


def _rope_tables(n):
    rows = n // GRID_W
    row = np.repeat(np.arange(rows), GRID_W).astype(np.float32)
    col = np.tile(np.arange(GRID_W), rows).astype(np.float32)
    axis_dim = QK_ROPE_DIM // 2
    inv = (1.0 / (np.float32(ROPE_THETA) ** (np.arange(0, axis_dim, 2, dtype=np.float32) / np.float32(axis_dim)))
           ).astype(np.float32)
    ang_r = row[:, None] * inv
    ang_c = col[:, None] * inv
    cos = np.concatenate([np.cos(ang_r)] * 2 + [np.cos(ang_c)] * 2, axis=-1)
    sin = np.concatenate([-np.sin(ang_r), np.sin(ang_r), -np.sin(ang_c), np.sin(ang_c)], axis=-1)
    return jnp.asarray(cos, F32), jnp.asarray(sin, F32)


def kernel(x, c, ctx, c_ctx, w_mod, b_mod, g_norm_mix, g_norm_ffn, w_in, g_q_norm, w_uq, g_kv_norm,
           w_ukv, lb_fwd, lb_bwd, g_hgrn_norm, w_out, w_gate, w_up, w_down, g_final):
    B, N, D = x.shape
    L = ctx.shape[1]
    layer = 0

    cc = jnp.concatenate([c, c_ctx[None, :], jnp.zeros((7, D), F32)], axis=0)
    mod, *w_in_parts = _mod_call(cc, w_mod[layer], b_mod[layer][None, :], w_in[layer].T)
    pad = jnp.zeros((B, 2, D), F32)
    mod_lat = jnp.concatenate([mod[:B].reshape(B, 6, D), pad], axis=1)
    mod_ctx = jnp.concatenate([jnp.broadcast_to(mod[B].reshape(1, 6, D), (B, 6, D)), pad], axis=1)

    w_uq_p, w_ukv_p = _prep_weights(w_uq[layer], w_ukv[layer])
    cos, sin = _rope_tables(N)
    row2 = lambda v: v.reshape(1, -1)
    proj_args = (row2(g_norm_mix[layer]), *w_in_parts, row2(g_q_norm[layer]), w_uq_p,
                 row2(g_kv_norm[layer]), w_ukv_p)

    q_l, k_l, v_l, hq_l, hi_l, hg_l, *decay_l = _in_proj_call(
        x, mod_lat, *proj_args, cos, sin, lb_fwd, lb_bwd, tm=PROJ_ROWS, queries=True)
    k_c, v_c, hi_c, *decay_c = _in_proj_call(
        ctx, mod_ctx, *proj_args, jnp.ones((L, QK_ROPE_DIM), F32), jnp.zeros((L, QK_ROPE_DIM), F32), lb_fwd, lb_bwd,
        tm=L, queries=False)

    o_mla, (w_out16, w_gate16, w_up16, w_down16) = _attn_call(
        q_l, k_c, k_l, v_c, v_l, (w_out[layer], w_gate[layer], w_up[layer], w_down[layer]),
        tq=ATTN_Q_TILE, tk=ATTN_K_TILE)
    o_hgrn = _hgrn_call((hi_c, *decay_c), (hq_l, hi_l, hg_l, *decay_l), row2(g_hgrn_norm[layer]))

    return _out_ffn_call(x, o_mla, o_hgrn, mod_lat, row2(g_norm_ffn[layer]),
                         w_out16, w_gate16, w_up16, w_down16,
                         row2(g_final), tm=FFN_ROWS)
```

```python
import functools

import numpy as np
import jax
import jax.numpy as jnp
from jax import lax
from jax.experimental import pallas as pl
from jax.experimental.pallas import tpu as pltpu

D_MODEL = 1024
GRID_W = 64
EPS = 1e-6
ROPE_THETA = 10000.0
V_HEAD_DIM = 128
QK_NOPE_DIM = 128
QK_ROPE_DIM = 64
Q_LORA_RANK = 256
KV_LORA_RANK = 256
HEADS = 4
QK_HEAD_DIM = QK_NOPE_DIM + QK_ROPE_DIM
HGRN_DIM = 128
HGRN_WIDTH = HEADS * HGRN_DIM
CHUNK = 64
IN_SIZES = (Q_LORA_RANK, KV_LORA_RANK, QK_ROPE_DIM,
            HGRN_WIDTH, HGRN_WIDTH, HGRN_WIDTH, HGRN_WIDTH, HGRN_WIDTH)
D_FF = 2816
FF_CHUNK = 256
HGRN_BLOCK = 256
PROJ_ROWS = 1024
ATTN_Q_TILE = 512
ATTN_K_TILE = 1024
FFN_ROWS = 512
CAST_BLOCKS = 16
VMEM_LIMIT_BYTES = 56 * 1024 * 1024
Q_SCALE = float(np.log2(np.e) / np.sqrt(QK_HEAD_DIM))

BF16 = jnp.bfloat16
F32 = jnp.float32


def _dot(a, b):
    return jnp.dot(a, b, preferred_element_type=F32)


def _dot_nt(a, b):
    return lax.dot_general(a, b, (((1,), (1,)), ((), ())), preferred_element_type=F32)


def _dot_tn(a, b):
    return lax.dot_general(a, b, (((0,), (0,)), ((), ())), preferred_element_type=F32)


def _silu(x):
    return x * jax.nn.sigmoid(x)


def _rms(x):
    return x * lax.rsqrt(jnp.mean(x * x, axis=-1, keepdims=True) + EPS)


_MOD_STEPS = 8
_HG_COLS = 512


def _mod_kernel(c_ref, w_ref, b_ref, wt_ref, o_ref, wlat_ref, whg_ref, wkp_ref):
    a = _silu(c_ref[...]).astype(BF16)
    o_ref[...] = _dot(a, w_ref[...].astype(BF16)) + b_ref[...]

    j = pl.program_id(0)
    offs = np.cumsum((0,) + IN_SIZES)
    n_hg = (offs[-1] - offs[3]) // _HG_COLS

    @pl.when(j < n_hg)
    def _():
        r0 = pl.multiple_of(offs[3] + j * _HG_COLS, CHUNK)
        whg_ref[...] = wt_ref[pl.ds(r0, _HG_COLS), :].T.astype(BF16)

    @pl.when(j == n_hg)
    def _():
        wlat_ref[...] = wt_ref[:offs[2], :].T.astype(BF16)

    @pl.when(j == n_hg + 1)
    def _():
        kpe = wt_ref[offs[2]:offs[3], :]
        quarter = QK_ROPE_DIM // 4
        swapped = [kpe[q * quarter:(q + 1) * quarter] for q in (1, 0, 3, 2)]
        wkp_ref[...] = jnp.concatenate([kpe] + swapped, axis=0).T.astype(BF16)


def _mod_call(cc, w_mod, b_mod, w_in_t):
    rows = cc.shape[0]
    cols = w_mod.shape[1]
    tn = cols // _MOD_STEPS
    n_lat, n_hg, n_kp = Q_LORA_RANK + KV_LORA_RANK, 5 * HGRN_WIDTH, 2 * QK_ROPE_DIM
    assert n_hg // _HG_COLS + 2 <= _MOD_STEPS
    const = lambda j: (0, 0)
    return pl.pallas_call(
        _mod_kernel,
        grid=(_MOD_STEPS,),
        in_specs=[pl.BlockSpec((rows, D_MODEL), const),
                  pl.BlockSpec((D_MODEL, tn), lambda j: (0, j)),
                  pl.BlockSpec((1, tn), lambda j: (0, j)),
                  pl.BlockSpec(w_in_t.shape, const, pipeline_mode=pl.Buffered(1))],
        out_specs=[pl.BlockSpec((rows, tn), lambda j: (0, j)),
                   pl.BlockSpec((D_MODEL, n_lat), const),
                   pl.BlockSpec((D_MODEL, _HG_COLS), lambda j: (0, jnp.minimum(j, n_hg // _HG_COLS - 1))),
                   pl.BlockSpec((D_MODEL, n_kp), const)],
        out_shape=[jax.ShapeDtypeStruct((rows, cols), F32)]
                  + [jax.ShapeDtypeStruct((D_MODEL, w), BF16) for w in (n_lat, n_hg, n_kp)],
        compiler_params=pltpu.CompilerParams(dimension_semantics=("arbitrary",),
                                             vmem_limit_bytes=VMEM_LIMIT_BYTES),
        name="mod",
    )(cc, w_mod, b_mod, w_in_t)


def _chunk_cumsum(g, reverse):
    rows, w = g.shape
    x = g.reshape(rows // 8, 8, w)
    sub = lax.broadcasted_iota(jnp.int32, (1, 8, w), 1)
    for s in (1, 2, 4):
        if reverse:
            x = x + jnp.where(sub < 8 - s, pltpu.roll(x, 8 - s, axis=1), 0.0)
        else:
            x = x + jnp.where(sub >= s, pltpu.roll(x, s, axis=1), 0.0)
    groups = CHUNK // 8
    x = x.reshape(rows // CHUNK, groups, 8, w)
    edge = 0 if reverse else 7
    outs = [None] * groups
    carry = None
    for j in (range(groups - 1, -1, -1) if reverse else range(groups)):
        blk = x[:, j] if carry is None else x[:, j] + carry
        outs[j] = blk
        carry = jnp.broadcast_to(blk[:, edge:edge + 1, :], blk.shape)
    return jnp.stack(outs, axis=1).reshape(rows, w)


def _chunk_masks():
    row = lax.broadcasted_iota(jnp.int32, (HGRN_BLOCK, HGRN_BLOCK), 0)
    col = lax.broadcasted_iota(jnp.int32, (HGRN_BLOCK, HGRN_BLOCK), 1)
    same_chunk = (row // CHUNK) == (col // CHUNK)
    return same_chunk & (col <= row), same_chunk & (col >= row)


def _in_proj_kernel(x_ref, mod_ref, gmix_ref, wlat_ref, whg_ref, wkp_ref, gq_ref, wuq_ref, gkv_ref, wukv_ref,
                    cos_ref, sin_ref, lbf_ref, lbb_ref, *out_refs, queries):
    if queries:
        qt_ref, k_ref, vt_ref, hq_ref, hi_ref, hg_ref, kf_ref, bf_ref, kb_ref, bb_ref = out_refs
        hgrn_outs = ((0, hq_ref), (1, hi_ref), (2, hg_ref))
    else:
        k_ref, vt_ref, hi_ref, kf_ref, bf_ref, kb_ref, bb_ref = out_refs
        hgrn_outs = ((1, hi_ref),)
    x = x_ref[0]
    shift = mod_ref[0, 0:1, :]
    gain = gmix_ref[...] * (1.0 + mod_ref[0, 1:2, :])
    h = (_rms(x) * gain + shift).astype(BF16)

    cos1 = cos_ref[...]
    sin1 = sin_ref[...]

    lat = _dot(h, wlat_ref[...])
    ckv = (_rms(lat[:, Q_LORA_RANK:]) * gkv_ref[...]).astype(BF16)

    if queries:
        q = _dot((_rms(lat[:, :Q_LORA_RANK]) * gq_ref[...]).astype(BF16), wuq_ref[...])
        cos4 = jnp.concatenate([cos1] * HEADS, axis=1)
        sin4 = jnp.concatenate([sin1] * HEADS, axis=1)
        q_nope_t = (q[:, :512] * Q_SCALE).T
        q_rope_t = ((q[:, 512:768] * cos4 + q[:, 768:1024] * sin4) * Q_SCALE).T
        for hd in range(HEADS):
            qt_ref[0, hd, :QK_NOPE_DIM, :] = q_nope_t[hd * 128:(hd + 1) * 128].astype(BF16)
            qt_ref[0, hd, QK_NOPE_DIM:, :] = q_rope_t[hd * 64:(hd + 1) * 64].astype(BF16)
    kv = _dot(ckv, wukv_ref[...])
    kp = _dot(h, wkp_ref[...])
    k_rope = (kp[:, :QK_ROPE_DIM] * cos1 + kp[:, QK_ROPE_DIM:] * sin1).astype(BF16)
    v_t = kv[:, 512:].T
    for hd in range(HEADS):
        k_ref[0, hd, :, :QK_NOPE_DIM] = kv[:, hd * 128:(hd + 1) * 128].astype(BF16)
        k_ref[0, hd, :, QK_NOPE_DIM:] = k_rope
        vt_ref[0, hd] = v_t[hd * 128:(hd + 1) * 128].astype(BF16)

    for d, (lb_ref, kk_ref, b_ref) in enumerate(((lbf_ref, kf_ref, bf_ref), (lbb_ref, kb_ref, bb_ref))):
        c0 = (3 + d) * HGRN_WIDTH
        t = lb_ref[...]
        e = jnp.exp(t - jnp.max(t, axis=0, keepdims=True))
        lb = e[0:1] / jnp.sum(e, axis=0, keepdims=True)
        f = lb + (1.0 - lb) * jax.nn.sigmoid(_dot(h, whg_ref[:, c0:c0 + HGRN_WIDTH]))
        kk = (1.0 - f).astype(BF16)
        b = _chunk_cumsum(jnp.log2(f), reverse=(d == 1))
        for hd in range(HEADS):
            b_ref[0, hd] = b[:, hd * 128:(hd + 1) * 128]
            kk_ref[0, hd] = kk[:, hd * 128:(hd + 1) * 128]

    for j, o_ref in hgrn_outs:
        t = _dot(h, whg_ref[:, j * HGRN_WIDTH:(j + 1) * HGRN_WIDTH])
        for hd in range(HEADS):
            o_ref[0, hd] = t[:, hd * 128:(hd + 1) * 128].astype(o_ref.dtype)


def _in_proj_call(x, mod_rows, g_mix, w_lat, w_hgrn, w_kpe, g_q, w_uq, g_kv, w_ukv, cos, sin,
                  lb_fwd, lb_bwd, tm, queries):
    B, n, _ = x.shape
    const = lambda b, i: (0, 0)
    head_blk = lambda w: pl.BlockSpec((1, HEADS, tm, w), lambda b, i: (b, 0, i, 0))
    hshape = lambda w, dt: jax.ShapeDtypeStruct((B, HEADS, n, w), dt)
    head_blk_t = lambda w: pl.BlockSpec((1, HEADS, w, tm), lambda b, i: (b, 0, 0, i))
    hshape_t = lambda w: jax.ShapeDtypeStruct((B, HEADS, w, n), BF16)
    outs = [(head_blk(QK_HEAD_DIM), hshape(QK_HEAD_DIM, BF16)), (head_blk_t(V_HEAD_DIM), hshape_t(V_HEAD_DIM)),
            (head_blk(128), hshape(128, BF16)),
            (head_blk(128), hshape(128, BF16)), (head_blk(128), hshape(128, F32)),
            (head_blk(128), hshape(128, BF16)), (head_blk(128), hshape(128, F32))]
    if queries:
        outs = ([(head_blk_t(QK_HEAD_DIM), hshape_t(QK_HEAD_DIM))] + outs[:2]
                + [(head_blk(128), hshape(128, BF16)), outs[2], (head_blk(128), hshape(128, BF16))] + outs[3:])
    return pl.pallas_call(
        functools.partial(_in_proj_kernel, queries=queries),
        grid=(B, n // tm),
        in_specs=[pl.BlockSpec((1, tm, D_MODEL), lambda b, i: (b, i, 0)),
                  pl.BlockSpec((1, 8, D_MODEL), lambda b, i: (b, 0, 0)),
                  pl.BlockSpec((1, D_MODEL), const),
                  pl.BlockSpec(w_lat.shape, const),
                  pl.BlockSpec(w_hgrn.shape, const),
                  pl.BlockSpec(w_kpe.shape, const),
                  pl.BlockSpec((1, Q_LORA_RANK), const),
                  pl.BlockSpec((Q_LORA_RANK, 1024), const),
                  pl.BlockSpec((1, KV_LORA_RANK), const),
                  pl.BlockSpec((KV_LORA_RANK, 1024), const),
                  pl.BlockSpec((tm, QK_ROPE_DIM), lambda b, i: (i, 0)),
                  pl.BlockSpec((tm, QK_ROPE_DIM), lambda b, i: (i, 0)),
                  pl.BlockSpec(lb_fwd.shape, const),
                  pl.BlockSpec(lb_bwd.shape, const)],
        out_specs=[spec for spec, _ in outs],
        out_shape=[shape for _, shape in outs],
        compiler_params=pltpu.CompilerParams(
            dimension_semantics=("parallel", "parallel"), vmem_limit_bytes=VMEM_LIMIT_BYTES),
        name="in_proj",
    )(x, mod_rows, g_mix, w_lat, w_hgrn, w_kpe, g_q, w_uq, g_kv, w_ukv, cos, sin, lb_fwd, lb_bwd)


def _attn_kernel(qt_ref, kc_ref, kl_ref, vtc_ref, vtl_ref, *rest, tq, tk):
    n_cast = (len(rest) - 3) // 2
    o_ref, s_ref, m_ref = rest[n_cast], rest[-2], rest[-1]
    for w_ref, w16_ref in zip(rest[:n_cast], rest[n_cast + 1:-2]):
        w16_ref[...] = w_ref[...].astype(BF16)

    n_ctx = kc_ref.shape[2]
    n_lat = kl_ref.shape[2]
    chunks = [(n_ctx, lambda: kc_ref[0, 0], lambda: vtc_ref[0, 0])]
    for j in range(n_lat // tk):
        chunks.append((tk, lambda j=j: kl_ref[0, 0, j * tk:(j + 1) * tk, :],
                       lambda j=j: vtl_ref[0, 0, :, j * tk:(j + 1) * tk]))

    def scores(j, qt, slot):
        rows, keys, _ = chunks[j]
        s = _dot(keys(), qt)
        s_ref[slot, :rows, :] = s
        m_ref[slot, 0:1, :] = jnp.max(s, axis=0, keepdims=True)

    def load_q(t):
        return qt_ref[0, 0, :, pl.ds(pl.multiple_of(t * tq, tq), tq)]

    def q_tile(t, t_next, slot0):
        qt = load_q(t)
        m = l = acc = None
        for j, (rows, _, values_t) in enumerate(chunks):
            slot = (slot0 + j) % 2
            if j + 1 < len(chunks):
                scores(j + 1, qt, 1 - slot)
            else:
                scores(0, load_q(t_next), 1 - slot)
            m_chunk = m_ref[slot, 0:1, :]
            m_new = m_chunk if m is None else jnp.maximum(m, m_chunk)
            p = jnp.exp2(s_ref[slot, :rows, :] - m_new)
            l_chunk = jnp.sum(p, axis=0, keepdims=True)
            pv = _dot(values_t(), p.astype(BF16))
            if m is None:
                l, acc = l_chunk, pv
            else:
                alpha = jnp.exp2(m - m_new)
                l = alpha * l + l_chunk
                acc = alpha * acc + pv
            m = m_new
        o_ref[0, pl.ds(pl.multiple_of(t * tq, tq), tq), :] = (acc * (1.0 / l)).T.astype(o_ref.dtype)

    n_tiles = n_lat // tq
    scores(0, load_q(0), 0)

    def tile_pair(i, _):
        t = 2 * i
        q_tile(t, t + 1, 0)
        q_tile(t + 1, jnp.minimum(t + 2, n_tiles - 1), len(chunks) % 2)
        return 0

    lax.fori_loop(0, n_tiles // 2, tile_pair, 0)


def _attn_call(q_t, k_ctx, k_lat, vt_ctx, vt_lat, cast_weights, tq, tk):
    B, H, _, n = q_t.shape
    n_ctx = k_ctx.shape[2]
    steps_per_block = B * H // CAST_BLOCKS
    w_blk = lambda w: pl.BlockSpec((w.shape[0] // CAST_BLOCKS, w.shape[1]),
                                   lambda b, h: ((b * H + h) // steps_per_block, 0))
    outs = pl.pallas_call(
        functools.partial(_attn_kernel, tq=tq, tk=tk),
        grid=(B, H),
        in_specs=[pl.BlockSpec((1, 1, QK_HEAD_DIM, n), lambda b, h: (b, h, 0, 0)),
                  pl.BlockSpec((1, 1, n_ctx, QK_HEAD_DIM), lambda b, h: (b, h, 0, 0)),
                  pl.BlockSpec((1, 1, n, QK_HEAD_DIM), lambda b, h: (b, h, 0, 0)),
                  pl.BlockSpec((1, 1, V_HEAD_DIM, n_ctx), lambda b, h: (b, h, 0, 0)),
                  pl.BlockSpec((1, 1, V_HEAD_DIM, n), lambda b, h: (b, h, 0, 0))]
                 + [w_blk(w) for w in cast_weights],
        out_specs=[pl.BlockSpec((1, n, V_HEAD_DIM), lambda b, h: (b, 0, h))]
                  + [w_blk(w) for w in cast_weights],
        out_shape=[jax.ShapeDtypeStruct((B, n, H * V_HEAD_DIM), BF16)]
                  + [jax.ShapeDtypeStruct(w.shape, BF16) for w in cast_weights],
        scratch_shapes=[pltpu.VMEM((2, max(tk, n_ctx), tq), F32),
                        pltpu.VMEM((2, 8, tq), F32)],
        compiler_params=pltpu.CompilerParams(
            dimension_semantics=("arbitrary", "arbitrary"), vmem_limit_bytes=VMEM_LIMIT_BYTES),
        name="attn",
    )(q_t, k_ctx, k_lat, vt_ctx, vt_lat, *cast_weights)
    return outs[0], outs[1:]


_REF_ROWS = ((CHUNK // 2 - 1, CHUNK - 1), (CHUNK // 2, 0))
_CPB = HGRN_BLOCK // CHUNK
_FINISH_ROWS = 512


def _hgrn_kernel(vc_ref, kfc_ref, bfc_ref, kbc_ref, bbc_ref,
                 q_ref, v_ref, hg_ref, kf_ref, bf_ref, kb_ref, bb_ref, gon_ref,
                 o_ref, upd_ref, dec_ref, snap_ref, qd_ref, oin_ref):
    n_ctx = vc_ref.shape[2]
    n_lat = q_ref.shape[2]
    nc_ctx = n_ctx // CHUNK
    nc_lat = n_lat // CHUNK
    masks = _chunk_masks()

    def chunk_rows(x, r):
        return jnp.concatenate(
            [jnp.broadcast_to(x[c * CHUNK + r:c * CHUNK + r + 1], (CHUNK, x.shape[1])) for c in range(_CPB)],
            axis=0)

    def block_a(chunk0, rows, q, v, kks, bs):
        kds, decs, a, qds = [], [], None, []
        for d in range(2):
            r_ref, r_last = _REF_ROWS[d]
            kk, b = kks[d].astype(F32), bs[d]
            kds.append((kk * jnp.exp2(chunk_rows(b, r_last) - b)).astype(BF16))
            decs.append([jnp.exp2(b[c * CHUNK + r_last:c * CHUNK + r_last + 1]) for c in range(_CPB)])
            if q is not None:
                b_ref = chunk_rows(b, r_ref)
                qds.append((q * jnp.exp2(b)).astype(BF16))
                qa = (q * jnp.exp2(b - b_ref)).astype(BF16)
                ka = (kk * jnp.exp2(b_ref - b)).astype(BF16)
                a_d = jnp.where(masks[d], _dot_nt(qa, ka), 0.0)
                a = a_d if a is None else a + a_d
        kd = jnp.concatenate(kds, axis=1)
        for c in range(_CPB):
            rc = slice(c * CHUNK, (c + 1) * CHUNK)
            upd_ref[chunk0 + c] = _dot_tn(v[rc], kd[rc])
            dec_ref[chunk0 + c] = jnp.concatenate([decs[0][c], decs[1][c]], axis=1)
        if q is not None:
            qd_ref[rows, :] = jnp.concatenate(qds, axis=1)
            oin_ref[rows, :] = _dot(a.astype(BF16), v)

    for i in range(n_ctx // HGRN_BLOCK):
        r = slice(i * HGRN_BLOCK, (i + 1) * HGRN_BLOCK)
        block_a(i * _CPB, None, None, vc_ref[0, 0, r, :],
                (kfc_ref[0, 0, r, :], kbc_ref[0, 0, r, :]), (bfc_ref[0, 0, r, :], bbc_ref[0, 0, r, :]))

    def phase_a(i, _):
        r = pl.ds(pl.multiple_of(i * HGRN_BLOCK, HGRN_BLOCK), HGRN_BLOCK)
        block_a(nc_ctx + i * _CPB, r, q_ref[0, 0, r, :].astype(F32), v_ref[0, 0, r, :],
                (kf_ref[0, 0, r, :], kb_ref[0, 0, r, :]), (bf_ref[0, 0, r, :], bb_ref[0, 0, r, :]))
        return 0

    lax.fori_loop(0, n_lat // HGRN_BLOCK, phase_a, 0, unroll=16)

    def advance(st, cf, cb):
        dec = jnp.concatenate([dec_ref[cf][:, :HGRN_DIM], dec_ref[cb][:, HGRN_DIM:]], axis=1)
        upd = jnp.concatenate([upd_ref[cf][:, :HGRN_DIM], upd_ref[cb][:, HGRN_DIM:]], axis=1)
        return st * dec + upd

    st = jnp.zeros((HGRN_DIM, 2 * HGRN_DIM), F32)
    for i in range(nc_ctx):
        st = advance(st, i, nc_ctx - 1 - i)

    def phase_b(i, st):
        cf, cb = i, nc_lat - 1 - i
        sb = st.astype(BF16)
        snap_ref[cf, :, :HGRN_DIM] = sb[:, :HGRN_DIM]
        snap_ref[cb, :, HGRN_DIM:] = sb[:, HGRN_DIM:]
        return advance(st, nc_ctx + cf, nc_ctx + cb)

    lax.fori_loop(0, nc_lat, phase_b, st)

    cpf = _FINISH_ROWS // CHUNK
    def phase_c(i, _):
        r0 = pl.multiple_of(i * _FINISH_ROWS, _FINISH_ROWS)
        inter = [_dot_nt(qd_ref[pl.ds(r0 + c * CHUNK, CHUNK), :], snap_ref[i * cpf + c]) for c in range(cpf)]
        o = oin_ref[pl.ds(r0, _FINISH_ROWS), :] + jnp.concatenate(inter, axis=0)
        y = _rms(o) * gon_ref[...]
        gate = _silu(hg_ref[0, 0, pl.ds(r0, _FINISH_ROWS), :].astype(F32))
        o_ref[0, pl.ds(r0, _FINISH_ROWS), :] = (y * gate).astype(o_ref.dtype)
        return 0

    lax.fori_loop(0, n_lat // _FINISH_ROWS, phase_c, 0, unroll=8)


def _hgrn_call(ctx_ops, lat_ops, g_on):
    B, H, n, _ = lat_ops[0].shape
    n_ctx = ctx_ops[0].shape[2]
    n_chunks = (n + n_ctx) // CHUNK
    ctx_blk = pl.BlockSpec((1, 1, n_ctx, 128), lambda b, h: (b, h, 0, 0))
    lat_blk = pl.BlockSpec((1, 1, n, 128), lambda b, h: (b, h, 0, 0))
    return pl.pallas_call(
        _hgrn_kernel,
        grid=(B, H),
        in_specs=[ctx_blk] * len(ctx_ops) + [lat_blk] * len(lat_ops)
                 + [pl.BlockSpec((1, 128), lambda b, h: (0, 0))],
        out_specs=pl.BlockSpec((1, n, 128), lambda b, h: (b, 0, h)),
        out_shape=jax.ShapeDtypeStruct((B, n, H * 128), BF16),
        scratch_shapes=[pltpu.VMEM((n_chunks, HGRN_DIM, 2 * HGRN_DIM), F32),
                        pltpu.VMEM((n_chunks, 1, 2 * HGRN_DIM), F32),
                        pltpu.VMEM((n // CHUNK, HGRN_DIM, 2 * HGRN_DIM), BF16),
                        pltpu.VMEM((n, 2 * HGRN_DIM), BF16),
                        pltpu.VMEM((n, HGRN_DIM), F32)],
        compiler_params=pltpu.CompilerParams(
            dimension_semantics=("parallel", "parallel"), vmem_limit_bytes=VMEM_LIMIT_BYTES),
        name="hgrn",
    )(*ctx_ops, *lat_ops, g_on)


def _out_ffn_kernel(x_ref, om_ref, oh_ref, mod_ref, gffn_ref, wout_ref, wg_ref, wu_ref, wd_ref,
                    gfin_ref, o_ref):
    x = x_ref[0]
    mix = _dot(om_ref[0], wout_ref[:HEADS * V_HEAD_DIM, :]) + _dot(oh_ref[0], wout_ref[HEADS * V_HEAD_DIM:, :])
    x1 = x + mod_ref[0, 2:3, :] * mix
    gain = gffn_ref[...] * (1.0 + mod_ref[0, 4:5, :])
    h2 = (_rms(x1) * gain + mod_ref[0, 3:4, :]).astype(BF16)
    acts = []
    for j in range(D_FF // FF_CHUNK):
        c0 = j * FF_CHUNK
        g = _dot(h2, wg_ref[:, c0:c0 + FF_CHUNK])
        u = _dot(h2, wu_ref[:, c0:c0 + FF_CHUNK])
        acts.append((_silu(g) * u).astype(BF16))
    acc = _dot(jnp.concatenate(acts, axis=1), wd_ref[...])
    x2 = x1 + mod_ref[0, 5:6, :] * acc
    o_ref[0] = _rms(x2) * gfin_ref[...]


def _out_ffn_call(x, o_mla, o_hgrn, mod_rows, g_ffn, w_out, w_gate, w_up, w_down, g_final, tm):
    B, n, _ = x.shape
    const = lambda b, i: (0, 0)
    resident = lambda shape: pl.BlockSpec(shape, const, pipeline_mode=pl.Buffered(1))
    return pl.pallas_call(
        _out_ffn_kernel,
        grid=(B, n // tm),
        in_specs=[pl.BlockSpec((1, tm, D_MODEL), lambda b, i: (b, i, 0)),
                  pl.BlockSpec((1, tm, 512), lambda b, i: (b, i, 0)),
                  pl.BlockSpec((1, tm, 512), lambda b, i: (b, i, 0)),
                  pl.BlockSpec((1, 8, D_MODEL), lambda b, i: (b, 0, 0)),
                  pl.BlockSpec((1, D_MODEL), const),
                  resident((D_MODEL, D_MODEL)),
                  resident((D_MODEL, D_FF)),
                  resident((D_MODEL, D_FF)),
                  resident((D_FF, D_MODEL)),
                  pl.BlockSpec((1, D_MODEL), const)],
        out_specs=pl.BlockSpec((1, tm, D_MODEL), lambda b, i: (b, i, 0)),
        out_shape=jax.ShapeDtypeStruct((B, n, D_MODEL), F32),
        compiler_params=pltpu.CompilerParams(
            dimension_semantics=("parallel", "parallel"), vmem_limit_bytes=VMEM_LIMIT_BYTES),
        name="out_ffn",
    )(x, o_mla, o_hgrn, mod_rows, g_ffn, w_out, w_gate, w_up, w_down, g_final)


_HALF_SWAP = np.concatenate([np.arange(16, 32), np.arange(0, 16), np.arange(48, 64), np.arange(32, 48)])


def _prep_weights(w_uq, w_ukv):
    uq = w_uq.reshape(Q_LORA_RANK, HEADS, QK_HEAD_DIM)
    rope = uq[:, :, QK_NOPE_DIM:]
    w_uq_p = jnp.concatenate([uq[:, :, :QK_NOPE_DIM].reshape(Q_LORA_RANK, -1),
                              rope.reshape(Q_LORA_RANK, -1),
                              rope[:, :, _HALF_SWAP].reshape(Q_LORA_RANK, -1)], axis=1).astype(BF16)
    ukv = w_ukv.reshape(KV_LORA_RANK, HEADS, QK_NOPE_DIM + V_HEAD_DIM)
    w_ukv_p = jnp.concatenate([ukv[:, :, :QK_NOPE_DIM].reshape(KV_LORA_RANK, -1),
                               ukv[:, :, QK_NOPE_DIM:].reshape(KV_LORA_RANK, -1)], axis=1).astype(BF16)
    return w_uq_p, w_ukv_p


def _rope_tables(n):
    rows = n // GRID_W
    row = np.repeat(np.arange(rows), GRID_W).astype(np.float32)
    col = np.tile(np.arange(GRID_W), rows).astype(np.float32)
    axis_dim = QK_ROPE_DIM // 2
    inv = (1.0 / (np.float32(ROPE_THETA) ** (np.arange(0, axis_dim, 2, dtype=np.float32) / np.float32(axis_dim)))
           ).astype(np.float32)
    ang_r = row[:, None] * inv
    ang_c = col[:, None] * inv
    cos = np.concatenate([np.cos(ang_r)] * 2 + [np.cos(ang_c)] * 2, axis=-1)
    sin = np.concatenate([-np.sin(ang_r), np.sin(ang_r), -np.sin(ang_c), np.sin(ang_c)], axis=-1)
    return jnp.asarray(cos, F32), jnp.asarray(sin, F32)


def kernel(x, c, ctx, c_ctx, w_mod, b_mod, g_norm_mix, g_norm_ffn, w_in, g_q_norm, w_uq, g_kv_norm,
           w_ukv, lb_fwd, lb_bwd, g_hgrn_norm, w_out, w_gate, w_up, w_down, g_final):
    B, N, D = x.shape
    L = ctx.shape[1]
    layer = 0

    cc = jnp.concatenate([c, c_ctx[None, :], jnp.zeros((7, D), F32)], axis=0)
    mod, *w_in_parts = _mod_call(cc, w_mod[layer], b_mod[layer][None, :], w_in[layer].T)
    pad = jnp.zeros((B, 2, D), F32)
    mod_lat = jnp.concatenate([mod[:B].reshape(B, 6, D), pad], axis=1)
    mod_ctx = jnp.concatenate([jnp.broadcast_to(mod[B].reshape(1, 6, D), (B, 6, D)), pad], axis=1)

    w_uq_p, w_ukv_p = _prep_weights(w_uq[layer], w_ukv[layer])
    cos, sin = _rope_tables(N)
    row2 = lambda v: v.reshape(1, -1)
    proj_args = (row2(g_norm_mix[layer]), *w_in_parts, row2(g_q_norm[layer]), w_uq_p,
                 row2(g_kv_norm[layer]), w_ukv_p)

    q_l, k_l, v_l, hq_l, hi_l, hg_l, *decay_l = _in_proj_call(
        x, mod_lat, *proj_args, cos, sin, lb_fwd, lb_bwd, tm=PROJ_ROWS, queries=True)
    k_c, v_c, hi_c, *decay_c = _in_proj_call(
        ctx, mod_ctx, *proj_args, jnp.ones((L, QK_ROPE_DIM), F32), jnp.zeros((L, QK_ROPE_DIM), F32), lb_fwd, lb_bwd,
        tm=L, queries=False)

    o_mla, (w_out16, w_gate16, w_up16, w_down16) = _attn_call(
        q_l, k_c, k_l, v_c, v_l, (w_out[layer], w_gate[layer], w_up[layer], w_down[layer]),
        tq=ATTN_Q_TILE, tk=ATTN_K_TILE)
    o_hgrn = _hgrn_call((hi_c, *decay_c), (hq_l, hi_l, hg_l, *decay_l), row2(g_hgrn_norm[layer]))

    return _out_ffn_call(x, o_mla, o_hgrn, mod_lat, row2(g_norm_ffn[layer]),
                         w_out16, w_gate16, w_up16, w_down16,
                         row2(g_final), tm=FFN_ROWS)
```

```python
import functools

import numpy as np
import jax
import jax.numpy as jnp
from jax import lax
from jax.experimental import pallas as pl
from jax.experimental.pallas import tpu as pltpu

D_MODEL = 1024
GRID_W = 64
EPS = 1e-6
ROPE_THETA = 10000.0
V_HEAD_DIM = 128
QK_NOPE_DIM = 128
QK_ROPE_DIM = 64
Q_LORA_RANK = 256
KV_LORA_RANK = 256
HEADS = 4
QK_HEAD_DIM = QK_NOPE_DIM + QK_ROPE_DIM
HGRN_DIM = 128
HGRN_WIDTH = HEADS * HGRN_DIM
CHUNK = 64
IN_SIZES = (Q_LORA_RANK, KV_LORA_RANK, QK_ROPE_DIM,
            HGRN_WIDTH, HGRN_WIDTH, HGRN_WIDTH, HGRN_WIDTH, HGRN_WIDTH)
D_FF = 2816
FF_CHUNK = 256
HGRN_BLOCK = 256
PROJ_ROWS = 1024
ATTN_Q_TILE = 512
ATTN_K_TILE = 1024
FFN_ROWS = 512
CAST_BLOCKS = 16
VMEM_LIMIT_BYTES = 56 * 1024 * 1024
Q_SCALE = float(np.log2(np.e) / np.sqrt(QK_HEAD_DIM))

BF16 = jnp.bfloat16
F32 = jnp.float32


def _dot(a, b):
    return jnp.dot(a, b, preferred_element_type=F32)


def _dot_nt(a, b):
    return lax.dot_general(a, b, (((1,), (1,)), ((), ())), preferred_element_type=F32)


def _dot_tn(a, b):
    return lax.dot_general(a, b, (((0,), (0,)), ((), ())), preferred_element_type=F32)


def _silu(x):
    return x * jax.nn.sigmoid(x)


def _rms(x):
    return x * lax.rsqrt(jnp.mean(x * x, axis=-1, keepdims=True) + EPS)


_MOD_STEPS = 8
_HG_COLS = 512


def _mod_kernel(c_ref, w_ref, b_ref, wt_ref, o_ref, wlat_ref, whg_ref, wkp_ref):
    a = _silu(c_ref[...]).astype(BF16)
    o_ref[...] = _dot(a, w_ref[...].astype(BF16)) + b_ref[...]

    j = pl.program_id(0)
    offs = np.cumsum((0,) + IN_SIZES)
    n_hg = (offs[-1] - offs[3]) // _HG_COLS

    @pl.when(j < n_hg)
    def _():
        r0 = pl.multiple_of(offs[3] + j * _HG_COLS, CHUNK)
        whg_ref[...] = wt_ref[pl.ds(r0, _HG_COLS), :].T.astype(BF16)

    @pl.when(j == n_hg)
    def _():
        wlat_ref[...] = wt_ref[:offs[2], :].T.astype(BF16)

    @pl.when(j == n_hg + 1)
    def _():
        kpe = wt_ref[offs[2]:offs[3], :]
        quarter = QK_ROPE_DIM // 4
        swapped = [kpe[q * quarter:(q + 1) * quarter] for q in (1, 0, 3, 2)]
        wkp_ref[...] = jnp.concatenate([kpe] + swapped, axis=0).T.astype(BF16)


def _mod_call(cc, w_mod, b_mod, w_in_t):
    rows = cc.shape[0]
    cols = w_mod.shape[1]
    tn = cols // _MOD_STEPS
    n_lat, n_hg, n_kp = Q_LORA_RANK + KV_LORA_RANK, 5 * HGRN_WIDTH, 2 * QK_ROPE_DIM
    assert n_hg // _HG_COLS + 2 <= _MOD_STEPS
    const = lambda j: (0, 0)
    return pl.pallas_call(
        _mod_kernel,
        grid=(_MOD_STEPS,),
        in_specs=[pl.BlockSpec((rows, D_MODEL), const),
                  pl.BlockSpec((D_MODEL, tn), lambda j: (0, j)),
                  pl.BlockSpec((1, tn), lambda j: (0, j)),
                  pl.BlockSpec(w_in_t.shape, const, pipeline_mode=pl.Buffered(1))],
        out_specs=[pl.BlockSpec((rows, tn), lambda j: (0, j)),
                   pl.BlockSpec((D_MODEL, n_lat), const),
                   pl.BlockSpec((D_MODEL, _HG_COLS), lambda j: (0, jnp.minimum(j, n_hg // _HG_COLS - 1))),
                   pl.BlockSpec((D_MODEL, n_kp), const)],
        out_shape=[jax.ShapeDtypeStruct((rows, cols), F32)]
                  + [jax.ShapeDtypeStruct((D_MODEL, w), BF16) for w in (n_lat, n_hg, n_kp)],
        compiler_params=pltpu.CompilerParams(dimension_semantics=("arbitrary",),
                                             vmem_limit_bytes=VMEM_LIMIT_BYTES),
        name="mod",
    )(cc, w_mod, b_mod, w_in_t)


def _chunk_cumsum(g, reverse):
    rows, w = g.shape
    x = g.reshape(rows // 8, 8, w)
    sub = lax.broadcasted_iota(jnp.int32, (1, 8, w), 1)
    for s in (1, 2, 4):
        if reverse:
            x = x + jnp.where(sub < 8 - s, pltpu.roll(x, 8 - s, axis=1), 0.0)
        else:
            x = x + jnp.where(sub >= s, pltpu.roll(x, s, axis=1), 0.0)
    groups = CHUNK // 8
    x = x.reshape(rows // CHUNK, groups, 8, w)
    edge = 0 if reverse else 7
    outs = [None] * groups
    carry = None
    for j in (range(groups - 1, -1, -1) if reverse else range(groups)):
        blk = x[:, j] if carry is None else x[:, j] + carry
        outs[j] = blk
        carry = jnp.broadcast_to(blk[:, edge:edge + 1, :], blk.shape)
    return jnp.stack(outs, axis=1).reshape(rows, w)


def _chunk_masks():
    row = lax.broadcasted_iota(jnp.int32, (HGRN_BLOCK, HGRN_BLOCK), 0)
    col = lax.broadcasted_iota(jnp.int32, (HGRN_BLOCK, HGRN_BLOCK), 1)
    same_chunk = (row // CHUNK) == (col // CHUNK)
    return same_chunk & (col <= row), same_chunk & (col >= row)


def _in_proj_kernel(x_ref, mod_ref, gmix_ref, wlat_ref, whg_ref, wkp_ref, gq_ref, wuq_ref, gkv_ref, wukv_ref,
                    cos_ref, sin_ref, lbf_ref, lbb_ref, *out_refs, queries):
    if queries:
        qt_ref, k_ref, vt_ref, hq_ref, hi_ref, hg_ref, kf_ref, bf_ref, kb_ref, bb_ref = out_refs
        hgrn_outs = ((0, hq_ref), (1, hi_ref), (2, hg_ref))
    else:
        k_ref, vt_ref, hi_ref, kf_ref, bf_ref, kb_ref, bb_ref = out_refs
        hgrn_outs = ((1, hi_ref),)
    x = x_ref[0]
    shift = mod_ref[0, 0:1, :]
    gain = gmix_ref[...] * (1.0 + mod_ref[0, 1:2, :])
    h = (_rms(x) * gain + shift).astype(BF16)

    cos1 = cos_ref[...]
    sin1 = sin_ref[...]

    lat = _dot(h, wlat_ref[...])
    ckv = (_rms(lat[:, Q_LORA_RANK:]) * gkv_ref[...]).astype(BF16)

    if queries:
        q = _dot((_rms(lat[:, :Q_LORA_RANK]) * gq_ref[...]).astype(BF16), wuq_ref[...])
        cos4 = jnp.concatenate([cos1] * HEADS, axis=1)
        sin4 = jnp.concatenate([sin1] * HEADS, axis=1)
        q_nope_t = (q[:, :512] * Q_SCALE).T
        q_rope_t = ((q[:, 512:768] * cos4 + q[:, 768:1024] * sin4) * Q_SCALE).T
        for hd in range(HEADS):
            qt_ref[0, hd, :QK_NOPE_DIM, :] = q_nope_t[hd * 128:(hd + 1) * 128].astype(BF16)
            qt_ref[0, hd, QK_NOPE_DIM:, :] = q_rope_t[hd * 64:(hd + 1) * 64].astype(BF16)
    kv = _dot(ckv, wukv_ref[...])
    kp = _dot(h, wkp_ref[...])
    k_rope = (kp[:, :QK_ROPE_DIM] * cos1 + kp[:, QK_ROPE_DIM:] * sin1).astype(BF16)
    v_t = kv[:, 512:].T
    for hd in range(HEADS):
        k_ref[0, hd, :, :QK_NOPE_DIM] = kv[:, hd * 128:(hd + 1) * 128].astype(BF16)
        k_ref[0, hd, :, QK_NOPE_DIM:] = k_rope
        vt_ref[0, hd] = v_t[hd * 128:(hd + 1) * 128].astype(BF16)

    for d, (lb_ref, kk_ref, b_ref) in enumerate(((lbf_ref, kf_ref, bf_ref), (lbb_ref, kb_ref, bb_ref))):
        c0 = (3 + d) * HGRN_WIDTH
        t = lb_ref[...]
        e = jnp.exp(t - jnp.max(t, axis=0, keepdims=True))
        lb = e[0:1] / jnp.sum(e, axis=0, keepdims=True)
        f = lb + (1.0 - lb) * jax.nn.sigmoid(_dot(h, whg_ref[:, c0:c0 + HGRN_WIDTH]))
        kk = (1.0 - f).astype(BF16)
        b = _chunk_cumsum(jnp.log2(f), reverse=(d == 1))
        for hd in range(HEADS):
            b_ref[0, hd] = b[:, hd * 128:(hd + 1) * 128]
            kk_ref[0, hd] = kk[:, hd * 128:(hd + 1) * 128]

    for j, o_ref in hgrn_outs:
        t = _dot(h, whg_ref[:, j * HGRN_WIDTH:(j + 1) * HGRN_WIDTH])
        for hd in range(HEADS):
            o_ref[0, hd] = t[:, hd * 128:(hd + 1) * 128].astype(o_ref.dtype)


def _in_proj_call(x, mod_rows, g_mix, w_lat, w_hgrn, w_kpe, g_q, w_uq, g_kv, w_ukv, cos, sin,
                  lb_fwd, lb_bwd, tm, queries):
    B, n, _ = x.shape
    const = lambda b, i: (0, 0)
    head_blk = lambda w: pl.BlockSpec((1, HEADS, tm, w), lambda b, i: (b, 0, i, 0))
    hshape = lambda w, dt: jax.ShapeDtypeStruct((B, HEADS, n, w), dt)
    head_blk_t = lambda w: pl.BlockSpec((1, HEADS, w, tm), lambda b, i: (b, 0, 0, i))
    hshape_t = lambda w: jax.ShapeDtypeStruct((B, HEADS, w, n), BF16)
    outs = [(head_blk(QK_HEAD_DIM), hshape(QK_HEAD_DIM, BF16)), (head_blk_t(V_HEAD_DIM), hshape_t(V_HEAD_DIM)),
            (head_blk(128), hshape(128, BF16)),
            (head_blk(128), hshape(128, BF16)), (head_blk(128), hshape(128, F32)),
            (head_blk(128), hshape(128, BF16)), (head_blk(128), hshape(128, F32))]
    if queries:
        outs = ([(head_blk_t(QK_HEAD_DIM), hshape_t(QK_HEAD_DIM))] + outs[:2]
                + [(head_blk(128), hshape(128, BF16)), outs[2], (head_blk(128), hshape(128, BF16))] + outs[3:])
    return pl.pallas_call(
        functools.partial(_in_proj_kernel, queries=queries),
        grid=(B, n // tm),
        in_specs=[pl.BlockSpec((1, tm, D_MODEL), lambda b, i: (b, i, 0)),
                  pl.BlockSpec((1, 8, D_MODEL), lambda b, i: (b, 0, 0)),
                  pl.BlockSpec((1, D_MODEL), const),
                  pl.BlockSpec(w_lat.shape, const),
                  pl.BlockSpec(w_hgrn.shape, const),
                  pl.BlockSpec(w_kpe.shape, const),
                  pl.BlockSpec((1, Q_LORA_RANK), const),
                  pl.BlockSpec((Q_LORA_RANK, 1024), const),
                  pl.BlockSpec((1, KV_LORA_RANK), const),
                  pl.BlockSpec((KV_LORA_RANK, 1024), const),
                  pl.BlockSpec((tm, QK_ROPE_DIM), lambda b, i: (i, 0)),
                  pl.BlockSpec((tm, QK_ROPE_DIM), lambda b, i: (i, 0)),
                  pl.BlockSpec(lb_fwd.shape, const),
                  pl.BlockSpec(lb_bwd.shape, const)],
        out_specs=[spec for spec, _ in outs],
        out_shape=[shape for _, shape in outs],
        compiler_params=pltpu.CompilerParams(
            dimension_semantics=("parallel", "parallel"), vmem_limit_bytes=VMEM_LIMIT_BYTES),
        name="in_proj",
    )(x, mod_rows, g_mix, w_lat, w_hgrn, w_kpe, g_q, w_uq, g_kv, w_ukv, cos, sin, lb_fwd, lb_bwd)


def _attn_kernel(qt_ref, kc_ref, kl_ref, vtc_ref, vtl_ref, *rest, tq, tk):
    n_cast = (len(rest) - 3) // 2
    o_ref, s_ref, m_ref = rest[n_cast], rest[-2], rest[-1]
    for w_ref, w16_ref in zip(rest[:n_cast], rest[n_cast + 1:-2]):
        w16_ref[...] = w_ref[...].astype(BF16)

    n_ctx = kc_ref.shape[2]
    n_lat = kl_ref.shape[2]
    chunks = [(n_ctx, lambda: kc_ref[0, 0], lambda: vtc_ref[0, 0])]
    for j in range(n_lat // tk):
        chunks.append((tk, lambda j=j: kl_ref[0, 0, j * tk:(j + 1) * tk, :],
                       lambda j=j: vtl_ref[0, 0, :, j * tk:(j + 1) * tk]))

    def scores(j, qt, slot):
        rows, keys, _ = chunks[j]
        s = _dot(keys(), qt)
        s_ref[slot, :rows, :] = s
        m_ref[slot, 0:1, :] = jnp.max(s, axis=0, keepdims=True)

    def load_q(t):
        return qt_ref[0, 0, :, pl.ds(pl.multiple_of(t * tq, tq), tq)]

    def q_tile(t, t_next, slot0):
        qt = load_q(t)
        m = l = acc = None
        for j, (rows, _, values_t) in enumerate(chunks):
            slot = (slot0 + j) % 2
            if j + 1 < len(chunks):
                scores(j + 1, qt, 1 - slot)
            else:
                scores(0, load_q(t_next), 1 - slot)
            m_chunk = m_ref[slot, 0:1, :]
            m_new = m_chunk if m is None else jnp.maximum(m, m_chunk)
            p = jnp.exp2(s_ref[slot, :rows, :] - m_new)
            l_chunk = jnp.sum(p, axis=0, keepdims=True)
            pv = _dot(values_t(), p.astype(BF16))
            if m is None:
                l, acc = l_chunk, pv
            else:
                alpha = jnp.exp2(m - m_new)
                l = alpha * l + l_chunk
                acc = alpha * acc + pv
            m = m_new
        o_ref[0, pl.ds(pl.multiple_of(t * tq, tq), tq), :] = (acc * (1.0 / l)).T.astype(o_ref.dtype)

    n_tiles = n_lat // tq
    scores(0, load_q(0), 0)

    def tile_pair(i, _):
        t = 2 * i
        q_tile(t, t + 1, 0)
        q_tile(t + 1, jnp.minimum(t + 2, n_tiles - 1), len(chunks) % 2)
        return 0

    lax.fori_loop(0, n_tiles // 2, tile_pair, 0)


def _attn_call(q_t, k_ctx, k_lat, vt_ctx, vt_lat, cast_weights, tq, tk):
    B, H, _, n = q_t.shape
    n_ctx = k_ctx.shape[2]
    steps_per_block = B * H // CAST_BLOCKS
    w_blk = lambda w: pl.BlockSpec((w.shape[0] // CAST_BLOCKS, w.shape[1]),
                                   lambda b, h: ((b * H + h) // steps_per_block, 0))
    outs = pl.pallas_call(
        functools.partial(_attn_kernel, tq=tq, tk=tk),
        grid=(B, H),
        in_specs=[pl.BlockSpec((1, 1, QK_HEAD_DIM, n), lambda b, h: (b, h, 0, 0)),
                  pl.BlockSpec((1, 1, n_ctx, QK_HEAD_DIM), lambda b, h: (b, h, 0, 0)),
                  pl.BlockSpec((1, 1, n, QK_HEAD_DIM), lambda b, h: (b, h, 0, 0)),
                  pl.BlockSpec((1, 1, V_HEAD_DIM, n_ctx), lambda b, h: (b, h, 0, 0)),
                  pl.BlockSpec((1, 1, V_HEAD_DIM, n), lambda b, h: (b, h, 0, 0))]
                 + [w_blk(w) for w in cast_weights],
        out_specs=[pl.BlockSpec((1, n, V_HEAD_DIM), lambda b, h: (b, 0, h))]
                  + [w_blk(w) for w in cast_weights],
        out_shape=[jax.ShapeDtypeStruct((B, n, H * V_HEAD_DIM), BF16)]
                  + [jax.ShapeDtypeStruct(w.shape, BF16) for w in cast_weights],
        scratch_shapes=[pltpu.VMEM((2, max(tk, n_ctx), tq), F32),
                        pltpu.VMEM((2, 8, tq), F32)],
        compiler_params=pltpu.CompilerParams(
            dimension_semantics=("arbitrary", "arbitrary"), vmem_limit_bytes=VMEM_LIMIT_BYTES),
        name="attn",
    )(q_t, k_ctx, k_lat, vt_ctx, vt_lat, *cast_weights)
    return outs[0], outs[1:]


_REF_ROWS = ((CHUNK // 2 - 1, CHUNK - 1), (CHUNK // 2, 0))
_CPB = HGRN_BLOCK // CHUNK
_FINISH_ROWS = 512


def _hgrn_kernel(vc_ref, kfc_ref, bfc_ref, kbc_ref, bbc_ref,
                 q_ref, v_ref, hg_ref, kf_ref, bf_ref, kb_ref, bb_ref, gon_ref,
                 o_ref, upd_ref, dec_ref, snap_ref, qd_ref, oin_ref):
    n_ctx = vc_ref.shape[2]
    n_lat = q_ref.shape[2]
    nc_ctx = n_ctx // CHUNK
    nc_lat = n_lat // CHUNK
    masks = _chunk_masks()

    def chunk_rows(x, r):
        return jnp.concatenate(
            [jnp.broadcast_to(x[c * CHUNK + r:c * CHUNK + r + 1], (CHUNK, x.shape[1])) for c in range(_CPB)],
            axis=0)

    def block_a(chunk0, rows, q, v, kks, bs):
        kds, decs, a, qds = [], [], None, []
        for d in range(2):
            r_ref, r_last = _REF_ROWS[d]
            kk, b = kks[d].astype(F32), bs[d]
            row = lambda c, r: b[c * CHUNK + r:c * CHUNK + r + 1]
            spread = lambda rows_: jnp.concatenate(
                [jnp.broadcast_to(x, (CHUNK, x.shape[1])) for x in rows_], axis=0)
            decs.append([jnp.exp2(row(c, r_last)) for c in range(_CPB)])
            if q is None:
                kds.append((kk * jnp.exp2(chunk_rows(b, r_last) - b)).astype(BF16))
            else:
                b_ref = chunk_rows(b, r_ref)
                qa = q * jnp.exp2(b - b_ref)
                ka = kk * jnp.exp2(b_ref - b)
                qds.append((qa * spread([jnp.exp2(row(c, r_ref)) for c in range(_CPB)])).astype(BF16))
                kds.append((ka * spread([jnp.exp2(row(c, r_last) - row(c, r_ref)) for c in range(_CPB)])
                            ).astype(BF16))
                a_d = jnp.where(masks[d], _dot_nt(qa.astype(BF16), ka.astype(BF16)), 0.0)
                a = a_d if a is None else a + a_d
        kd = jnp.concatenate(kds, axis=1)
        for c in range(_CPB):
            rc = slice(c * CHUNK, (c + 1) * CHUNK)
            upd_ref[chunk0 + c] = _dot_tn(v[rc], kd[rc])
            dec_ref[chunk0 + c] = jnp.concatenate([decs[0][c], decs[1][c]], axis=1)
        if q is not None:
            qd_ref[rows, :] = jnp.concatenate(qds, axis=1)
            oin_ref[rows, :] = _dot(a.astype(BF16), v)

    for i in range(n_ctx // HGRN_BLOCK):
        r = slice(i * HGRN_BLOCK, (i + 1) * HGRN_BLOCK)
        block_a(i * _CPB, None, None, vc_ref[0, 0, r, :],
                (kfc_ref[0, 0, r, :], kbc_ref[0, 0, r, :]), (bfc_ref[0, 0, r, :], bbc_ref[0, 0, r, :]))

    def phase_a(i, _):
        r = pl.ds(pl.multiple_of(i * HGRN_BLOCK, HGRN_BLOCK), HGRN_BLOCK)
        block_a(nc_ctx + i * _CPB, r, q_ref[0, 0, r, :].astype(F32), v_ref[0, 0, r, :],
                (kf_ref[0, 0, r, :], kb_ref[0, 0, r, :]), (bf_ref[0, 0, r, :], bb_ref[0, 0, r, :]))
        return 0

    lax.fori_loop(0, n_lat // HGRN_BLOCK, phase_a, 0, unroll=16)

    def advance(st, cf, cb):
        dec = jnp.concatenate([dec_ref[cf][:, :HGRN_DIM], dec_ref[cb][:, HGRN_DIM:]], axis=1)
        upd = jnp.concatenate([upd_ref[cf][:, :HGRN_DIM], upd_ref[cb][:, HGRN_DIM:]], axis=1)
        return st * dec + upd

    st = jnp.zeros((HGRN_DIM, 2 * HGRN_DIM), F32)
    for i in range(nc_ctx):
        st = advance(st, i, nc_ctx - 1 - i)

    def phase_b(i, st):
        cf, cb = i, nc_lat - 1 - i
        sb = st.astype(BF16)
        snap_ref[cf, :, :HGRN_DIM] = sb[:, :HGRN_DIM]
        snap_ref[cb, :, HGRN_DIM:] = sb[:, HGRN_DIM:]
        return advance(st, nc_ctx + cf, nc_ctx + cb)

    lax.fori_loop(0, nc_lat, phase_b, st)

    cpf = _FINISH_ROWS // CHUNK
    def phase_c(i, _):
        r0 = pl.multiple_of(i * _FINISH_ROWS, _FINISH_ROWS)
        inter = [_dot_nt(qd_ref[pl.ds(r0 + c * CHUNK, CHUNK), :], snap_ref[i * cpf + c]) for c in range(cpf)]
        o = oin_ref[pl.ds(r0, _FINISH_ROWS), :] + jnp.concatenate(inter, axis=0)
        y = _rms(o) * gon_ref[...]
        gate = _silu(hg_ref[0, 0, pl.ds(r0, _FINISH_ROWS), :].astype(F32))
        o_ref[0, pl.ds(r0, _FINISH_ROWS), :] = (y * gate).astype(o_ref.dtype)
        return 0

    lax.fori_loop(0, n_lat // _FINISH_ROWS, phase_c, 0, unroll=8)


def _hgrn_call(ctx_ops, lat_ops, g_on):
    B, H, n, _ = lat_ops[0].shape
    n_ctx = ctx_ops[0].shape[2]
    n_chunks = (n + n_ctx) // CHUNK
    ctx_blk = pl.BlockSpec((1, 1, n_ctx, 128), lambda b, h: (b, h, 0, 0))
    lat_blk = pl.BlockSpec((1, 1, n, 128), lambda b, h: (b, h, 0, 0))
    return pl.pallas_call(
        _hgrn_kernel,
        grid=(B, H),
        in_specs=[ctx_blk] * len(ctx_ops) + [lat_blk] * len(lat_ops)
                 + [pl.BlockSpec((1, 128), lambda b, h: (0, 0))],
        out_specs=pl.BlockSpec((1, n, 128), lambda b, h: (b, 0, h)),
        out_shape=jax.ShapeDtypeStruct((B, n, H * 128), BF16),
        scratch_shapes=[pltpu.VMEM((n_chunks, HGRN_DIM, 2 * HGRN_DIM), F32),
                        pltpu.VMEM((n_chunks, 1, 2 * HGRN_DIM), F32),
                        pltpu.VMEM((n // CHUNK, HGRN_DIM, 2 * HGRN_DIM), BF16),
                        pltpu.VMEM((n, 2 * HGRN_DIM), BF16),
                        pltpu.VMEM((n, HGRN_DIM), F32)],
        compiler_params=pltpu.CompilerParams(
            dimension_semantics=("parallel", "parallel"), vmem_limit_bytes=VMEM_LIMIT_BYTES),
        name="hgrn",
    )(*ctx_ops, *lat_ops, g_on)


def _out_ffn_kernel(x_ref, om_ref, oh_ref, mod_ref, gffn_ref, wout_ref, wg_ref, wu_ref, wd_ref,
                    gfin_ref, o_ref):
    x = x_ref[0]
    mix = _dot(om_ref[0], wout_ref[:HEADS * V_HEAD_DIM, :]) + _dot(oh_ref[0], wout_ref[HEADS * V_HEAD_DIM:, :])
    x1 = x + mod_ref[0, 2:3, :] * mix
    gain = gffn_ref[...] * (1.0 + mod_ref[0, 4:5, :])
    h2 = (_rms(x1) * gain + mod_ref[0, 3:4, :]).astype(BF16)
    acts = []
    for j in range(D_FF // FF_CHUNK):
        c0 = j * FF_CHUNK
        g = _dot(h2, wg_ref[:, c0:c0 + FF_CHUNK])
        u = _dot(h2, wu_ref[:, c0:c0 + FF_CHUNK])
        acts.append((_silu(g) * u).astype(BF16))
    acc = _dot(jnp.concatenate(acts, axis=1), wd_ref[...])
    x2 = x1 + mod_ref[0, 5:6, :] * acc
    o_ref[0] = _rms(x2) * gfin_ref[...]


def _out_ffn_call(x, o_mla, o_hgrn, mod_rows, g_ffn, w_out, w_gate, w_up, w_down, g_final, tm):
    B, n, _ = x.shape
    const = lambda b, i: (0, 0)
    resident = lambda shape: pl.BlockSpec(shape, const, pipeline_mode=pl.Buffered(1))
    return pl.pallas_call(
        _out_ffn_kernel,
        grid=(B, n // tm),
        in_specs=[pl.BlockSpec((1, tm, D_MODEL), lambda b, i: (b, i, 0)),
                  pl.BlockSpec((1, tm, 512), lambda b, i: (b, i, 0)),
                  pl.BlockSpec((1, tm, 512), lambda b, i: (b, i, 0)),
                  pl.BlockSpec((1, 8, D_MODEL), lambda b, i: (b, 0, 0)),
                  pl.BlockSpec((1, D_MODEL), const),
                  resident((D_MODEL, D_MODEL)),
                  resident((D_MODEL, D_FF)),
                  resident((D_MODEL, D_FF)),
                  resident((D_FF, D_MODEL)),
                  pl.BlockSpec((1, D_MODEL), const)],
        out_specs=pl.BlockSpec((1, tm, D_MODEL), lambda b, i: (b, i, 0)),
        out_shape=jax.ShapeDtypeStruct((B, n, D_MODEL), F32),
        compiler_params=pltpu.CompilerParams(
            dimension_semantics=("parallel", "parallel"), vmem_limit_bytes=VMEM_LIMIT_BYTES),
        name="out_ffn",
    )(x, o_mla, o_hgrn, mod_rows, g_ffn, w_out, w_gate, w_up, w_down, g_final)


_HALF_SWAP = np.concatenate([np.arange(16, 32), np.arange(0, 16), np.arange(48, 64), np.arange(32, 48)])


def _prep_weights(w_uq, w_ukv):
    uq = w_uq.reshape(Q_LORA_RANK, HEADS, QK_HEAD_DIM)
    rope = uq[:, :, QK_NOPE_DIM:]
    w_uq_p = jnp.concatenate([uq[:, :, :QK_NOPE_DIM].reshape(Q_LORA_RANK, -1),
                              rope.reshape(Q_LORA_RANK, -1),
                              rope[:, :, _HALF_SWAP].reshape(Q_LORA_RANK, -1)], axis=1).astype(BF16)
    ukv = w_ukv.reshape(KV_LORA_RANK, HEADS, QK_NOPE_DIM + V_HEAD_DIM)
    w_ukv_p = jnp.concatenate([ukv[:, :, :QK_NOPE_DIM].reshape(KV_LORA_RANK, -1),
                               ukv[:, :, QK_NOPE_DIM:].reshape(KV_LORA_RANK, -1)], axis=1).astype(BF16)
    return w_uq_p, w_ukv_p


def _rope_tables(n):
    rows = n // GRID_W
    row = np.repeat(np.arange(rows), GRID_W).astype(np.float32)
    col = np.tile(np.arange(GRID_W), rows).astype(np.float32)
    axis_dim = QK_ROPE_DIM // 2
    inv = (1.0 / (np.float32(ROPE_THETA) ** (np.arange(0, axis_dim, 2, dtype=np.float32) / np.float32(axis_dim)))
           ).astype(np.float32)
    ang_r = row[:, None] * inv
    ang_c = col[:, None] * inv
    cos = np.concatenate([np.cos(ang_r)] * 2 + [np.cos(ang_c)] * 2, axis=-1)
    sin = np.concatenate([-np.sin(ang_r), np.sin(ang_r), -np.sin(ang_c), np.sin(ang_c)], axis=-1)
    return jnp.asarray(cos, F32), jnp.asarray(sin, F32)


def kernel(x, c, ctx, c_ctx, w_mod, b_mod, g_norm_mix, g_norm_ffn, w_in, g_q_norm, w_uq, g_kv_norm,
           w_ukv, lb_fwd, lb_bwd, g_hgrn_norm, w_out, w_gate, w_up, w_down, g_final):
    B, N, D = x.shape
    L = ctx.shape[1]
    layer = 0

    cc = jnp.concatenate([c, c_ctx[None, :], jnp.zeros((7, D), F32)], axis=0)
    mod, *w_in_parts = _mod_call(cc, w_mod[layer], b_mod[layer][None, :], w_in[layer].T)
    pad = jnp.zeros((B, 2, D), F32)
    mod_lat = jnp.concatenate([mod[:B].reshape(B, 6, D), pad], axis=1)
    mod_ctx = jnp.concatenate([jnp.broadcast_to(mod[B].reshape(1, 6, D), (B, 6, D)), pad], axis=1)

    w_uq_p, w_ukv_p = _prep_weights(w_uq[layer], w_ukv[layer])
    cos, sin = _rope_tables(N)
    row2 = lambda v: v.reshape(1, -1)
    proj_args = (row2(g_norm_mix[layer]), *w_in_parts, row2(g_q_norm[layer]), w_uq_p,
                 row2(g_kv_norm[layer]), w_ukv_p)

    q_l, k_l, v_l, hq_l, hi_l, hg_l, *decay_l = _in_proj_call(
        x, mod_lat, *proj_args, cos, sin, lb_fwd, lb_bwd, tm=PROJ_ROWS, queries=True)
    k_c, v_c, hi_c, *decay_c = _in_proj_call(
        ctx, mod_ctx, *proj_args, jnp.ones((L, QK_ROPE_DIM), F32), jnp.zeros((L, QK_ROPE_DIM), F32), lb_fwd, lb_bwd,
        tm=L, queries=False)

    o_mla, (w_out16, w_gate16, w_up16, w_down16) = _attn_call(
        q_l, k_c, k_l, v_c, v_l, (w_out[layer], w_gate[layer], w_up[layer], w_down[layer]),
        tq=ATTN_Q_TILE, tk=ATTN_K_TILE)
    o_hgrn = _hgrn_call((hi_c, *decay_c), (hq_l, hi_l, hg_l, *decay_l), row2(g_hgrn_norm[layer]))

    return _out_ffn_call(x, o_mla, o_hgrn, mod_lat, row2(g_norm_ffn[layer]),
                         w_out16, w_gate16, w_up16, w_down16,
                         row2(g_final), tm=FFN_ROWS)
```
